```python
import jax, jax.numpy as jnp
from jax import lax
import numpy as np

D_MODEL = 1024
BATCH = 8
SEQ = 2048
DEPTH = 2

MOBA_HEADS = 8
MOBA_HEAD_DIM = D_MODEL // 16
MOBA_WIDTH = MOBA_HEADS * MOBA_HEAD_DIM
MOBA_BLOCK = 256
MOBA_TOPK = 3
MOBA_Q_CHUNK = 64
ROPE_THETA = 10000.0
GLA_HEADS = 4
GLA_DK = D_MODEL // 16
GLA_DV = D_MODEL // 8
GLA_KW = GLA_HEADS * GLA_DK
GLA_VW = GLA_HEADS * GLA_DV
GLA_RANK = 16
GLA_TEMP = 16.0
GLA_CHUNK = 64
CONV_CH = D_MODEL // 2
CONV_WIDTH = 31
D_FF = 4 * D_MODEL
N_BRANCH = 3
EPS = 1e-6
NEG = -1e30
IN_SIZES = (MOBA_WIDTH, MOBA_WIDTH, MOBA_WIDTH,
            GLA_KW, GLA_KW, GLA_VW, GLA_RANK, GLA_VW,
            CONV_CH, CONV_CH,
            N_BRANCH * D_MODEL)
IN_WIDTH = sum(IN_SIZES)

kernel_name = 'hybrid_moba_gla_conformer_gated'


def rms_norm(x, g):
    xf = x.astype(jnp.float32)
    y = xf * lax.rsqrt(jnp.mean(xf * xf, axis=-1, keepdims=True) + EPS)
    return (y * g.astype(jnp.float32)).astype(x.dtype)


def layer_norm(x, g, b):
    xf = x.astype(jnp.float32)
    mu = jnp.mean(xf, axis=-1, keepdims=True)
    var = jnp.mean(jnp.square(xf - mu), axis=-1, keepdims=True)
    y = (xf - mu) * lax.rsqrt(var + EPS)
    return (y * g.astype(jnp.float32) + b.astype(jnp.float32)).astype(x.dtype)


def rope_tables(positions):
    half = MOBA_HEAD_DIM // 2
    inv_freq = ROPE_THETA ** (-jnp.arange(half, dtype=jnp.float32) / half)
    ang = positions.astype(jnp.float32)[..., None] * inv_freq
    return jnp.cos(ang)[:, :, None, :], jnp.sin(ang)[:, :, None, :]


def apply_rope(x, cos, sin):
    xf = x.astype(jnp.float32)
    x1, x2 = jnp.split(xf, 2, axis=-1)
    return jnp.concatenate([x1 * cos - x2 * sin, x2 * cos + x1 * sin], axis=-1).astype(x.dtype)


def gather_blocks(blocks, idx):
    return jax.vmap(jax.vmap(lambda t, i: t[i]))(blocks, idx)


def moba_attention(q, k, v):
    B, H, S, Dh = q.shape
    nb = -(-S // MOBA_BLOCK)
    pad = nb * MOBA_BLOCK - S
    kb = jnp.pad(k, ((0, 0), (0, 0), (0, pad), (0, 0))).reshape(B, H, nb, MOBA_BLOCK, Dh)
    vb = jnp.pad(v, ((0, 0), (0, 0), (0, pad), (0, 0))).reshape(B, H, nb, MOBA_BLOCK, Dh)
    kmean = jnp.mean(kb.astype(jnp.float32), axis=3)
    gate = jnp.einsum('bhsd,bhnd->bhsn', q.astype(jnp.float32), kmean)
    qblk = jnp.arange(S) // MOBA_BLOCK
    past = jnp.arange(nb)[None, :] < qblk[:, None]
    gate = jnp.where(past, gate, -jnp.inf)
    n_sel = min(MOBA_TOPK, nb)
    _, gidx = lax.top_k(gate, n_sel)
    sel_valid = jnp.arange(n_sel)[None, :] < qblk[:, None]
    scale = Dh ** -0.5
    n_qc = S // MOBA_Q_CHUNK

    def chunk(c):
        start = c * MOBA_Q_CHUNK
        qc = lax.dynamic_slice_in_dim(q, start, MOBA_Q_CHUNK, axis=2)
        idx = lax.dynamic_slice_in_dim(gidx, start, MOBA_Q_CHUNK, axis=2)
        valid = lax.dynamic_slice_in_dim(sel_valid, start, MOBA_Q_CHUNK, axis=0)
        own = start // MOBA_BLOCK
        k_own = lax.dynamic_index_in_dim(kb, own, axis=2, keepdims=False)
        v_own = lax.dynamic_index_in_dim(vb, own, axis=2, keepdims=False)
        qpos = start + jnp.arange(MOBA_Q_CHUNK)
        kpos = own * MOBA_BLOCK + jnp.arange(MOBA_BLOCK)
        s_own = jnp.einsum('bhqd,bhkd->bhqk', qc, k_own).astype(jnp.float32) * scale
        s_own = jnp.where(kpos[None, :] <= qpos[:, None], s_own, NEG)
        scores = [s_own]
        for i in range(n_sel):
            k_g = gather_blocks(kb, idx[..., i])
            s = jnp.einsum('bhqd,bhqkd->bhqk', qc, k_g).astype(jnp.float32) * scale
            scores.append(jnp.where(valid[:, i][:, None], s, NEG))
        p = jax.nn.softmax(jnp.concatenate(scores, axis=-1), axis=-1).astype(v.dtype)
        out = jnp.einsum('bhqk,bhkd->bhqd', p[..., :MOBA_BLOCK], v_own)
        for i in range(n_sel):
            v_g = gather_blocks(vb, idx[..., i])
            p_i = p[..., MOBA_BLOCK * (i + 1):MOBA_BLOCK * (i + 2)]
            out = out + jnp.einsum('bhqk,bhqkd->bhqd', p_i, v_g)
        return out

    out = lax.map(chunk, jnp.arange(n_qc))
    return out.transpose(1, 2, 0, 3, 4).reshape(B, H, S, Dh)


def gla_attention(q, k, v, g):
    B, H, S, dk = q.shape
    dv = v.shape[-1]
    nc = S // GLA_CHUNK

    def to_chunks(t):
        return t.astype(jnp.float32).reshape(B, H, nc, GLA_CHUNK, t.shape[-1]).transpose(2, 0, 1, 3, 4)

    qc, kc, vc, gc = to_chunks(q * (dk ** -0.5)), to_chunks(k), to_chunks(v), to_chunks(g)
    causal = jnp.tril(jnp.ones((GLA_CHUNK, GLA_CHUNK), dtype=bool))[:, :, None]

    def step(state, inp):
        qi, ki, vi, gi = inp
        G = jnp.cumsum(gi, axis=2)
        diff = G[:, :, :, None, :] - G[:, :, None, :, :]
        decay = jnp.exp(jnp.where(causal, diff, -jnp.inf))
        A = jnp.einsum('bhid,bhjd,bhijd->bhij', qi, ki, decay)
        o = jnp.einsum('bhij,bhjv->bhiv', A, vi) + jnp.einsum('bhid,bhdv->bhiv', qi * jnp.exp(G), state)
        G_last = G[:, :, -1:, :]
        k_dec = ki * jnp.exp(G_last - G)
        state = state * jnp.exp(G_last[:, :, 0, :])[..., None] + jnp.einsum('bhjd,bhjv->bhdv', k_dec, vi)
        return state, o

    state0 = jnp.zeros((B, H, dk, dv), jnp.float32)
    _, o = lax.scan(step, state0, (qc, kc, vc, gc))
    return o.transpose(1, 2, 0, 3, 4).reshape(B, H, S, dv)


def conformer_conv(a, gate, w_dw, b_dw, g_ln, b_ln):
    u = a * jax.nn.sigmoid(gate)
    y = lax.conv_general_dilated(u, w_dw[:, None, :].astype(u.dtype), window_strides=(1,),
                                 padding=((CONV_WIDTH - 1, 0),),
                                 dimension_numbers=('NWC', 'WIO', 'NWC'),
                                 feature_group_count=CONV_CH) + b_dw
    return jax.nn.silu(layer_norm(y, g_ln, b_ln))


def hybrid_mixer(xn, cos, sin, w_in, b_gate, w_gla_a2, b_gla_a, g_gla_norm, w_dw, b_dw,
                 g_conv_ln, b_conv_ln, w_moba_o, w_gla_o, w_conv_o, w_mix_out):
    B, S, _ = xn.shape
    z = xn @ w_in
    points, acc = [], 0
    for s in IN_SIZES[:-1]:
        acc += s
        points.append(acc)
    mq, mk, mv, lq, lk, lv, la, lr, ca, cg, gates = jnp.split(z, points, axis=-1)

    def moba_heads(t, rot):
        t = t.reshape(B, S, MOBA_HEADS, MOBA_HEAD_DIM)
        t = apply_rope(t, cos, sin) if rot else t
        return t.transpose(0, 2, 1, 3)
    ya = moba_attention(moba_heads(mq, True), moba_heads(mk, True), moba_heads(mv, False))
    ya = ya.transpose(0, 2, 1, 3).reshape(B, S, MOBA_WIDTH) @ w_moba_o

    logdecay = jax.nn.log_sigmoid((la @ w_gla_a2 + b_gla_a).astype(jnp.float32)) / GLA_TEMP
    def gla_heads(t, d):
        return t.reshape(B, S, GLA_HEADS, d).transpose(0, 2, 1, 3)
    ob = gla_attention(gla_heads(lq, GLA_DK), gla_heads(lk, GLA_DK), gla_heads(lv, GLA_DV),
                       gla_heads(logdecay, GLA_DK))
    ob = ob.transpose(0, 2, 1, 3).astype(xn.dtype)
    ob = rms_norm(ob, g_gla_norm) * jax.nn.silu(lr.reshape(B, S, GLA_HEADS, GLA_DV))
    yb = ob.reshape(B, S, GLA_VW) @ w_gla_o

    yc = conformer_conv(ca, cg, w_dw, b_dw, g_conv_ln, b_conv_ln) @ w_conv_o

    g_a, g_b, g_c = jnp.split(jax.nn.sigmoid(gates + b_gate), N_BRANCH, axis=-1)
    return (g_a * ya + g_b * yb + g_c * yc) @ w_mix_out


def sq_relu_mlp(xn, w_up, w_down):
    return jnp.square(jax.nn.relu(xn @ w_up)) @ w_down


def setup_inputs(seed: int = 0) -> dict:
    key = jax.random.key(seed)
    ks = jax.random.split(key, 24)
    f32 = jnp.float32

    def nrm(k, shape, fan_in):
        return jax.random.normal(k, shape, f32) * (fan_in ** -0.5)

    def gain(k, shape):
        return 1.0 + 0.02 * jax.random.normal(k, shape, f32)

    def bias(k, shape, s):
        return s * jax.random.normal(k, shape, f32)

    L = DEPTH
    return {
        'x': jax.random.normal(ks[0], (BATCH, SEQ, D_MODEL), f32),
        'positions': jnp.broadcast_to(jnp.arange(SEQ, dtype=jnp.int32), (BATCH, SEQ)),
        'g_mix_pre': gain(ks[1], (L, D_MODEL)),
        'w_in': nrm(ks[2], (L, D_MODEL, IN_WIDTH), D_MODEL),
        'b_gate': bias(ks[3], (L, N_BRANCH * D_MODEL), 0.1),
        'w_gla_a2': nrm(ks[4], (L, GLA_RANK, GLA_KW), GLA_RANK),
        'b_gla_a': bias(ks[5], (L, GLA_KW), 0.1),
        'g_gla_norm': gain(ks[6], (L, GLA_DV)),
        'w_dw': nrm(ks[7], (L, CONV_WIDTH, CONV_CH), CONV_WIDTH),
        'b_dw': bias(ks[8], (L, CONV_CH), 0.02),
        'g_conv_ln': gain(ks[9], (L, CONV_CH)),
        'b_conv_ln': bias(ks[10], (L, CONV_CH), 0.02),
        'w_moba_o': nrm(ks[11], (L, MOBA_WIDTH, D_MODEL), MOBA_WIDTH),
        'w_gla_o': nrm(ks[12], (L, GLA_VW, D_MODEL), GLA_VW),
        'w_conv_o': nrm(ks[13], (L, CONV_CH, D_MODEL), CONV_CH),
        'w_mix_out': nrm(ks[14], (L, D_MODEL, D_MODEL), D_MODEL),
        'g_mix_post': gain(ks[15], (L, D_MODEL)),
        'g_mlp_pre': gain(ks[16], (L, D_MODEL)),
        'w_mlp_up': nrm(ks[17], (L, D_MODEL, D_FF), D_MODEL),
        'w_mlp_down': nrm(ks[18], (L, D_FF, D_MODEL), D_FF),
        'g_mlp_post': gain(ks[19], (L, D_MODEL)),
    }


def reference(x, positions, g_mix_pre, w_in, b_gate, w_gla_a2, b_gla_a, g_gla_norm, w_dw, b_dw,
              g_conv_ln, b_conv_ln, w_moba_o, w_gla_o, w_conv_o, w_mix_out, g_mix_post,
              g_mlp_pre, w_mlp_up, w_mlp_down, g_mlp_post):
    cos, sin = rope_tables(positions)
    h = x
    for l in range(DEPTH):
        y = hybrid_mixer(rms_norm(h, g_mix_pre[l]), cos, sin, w_in[l], b_gate[l], w_gla_a2[l],
                         b_gla_a[l], g_gla_norm[l], w_dw[l], b_dw[l], g_conv_ln[l], b_conv_ln[l],
                         w_moba_o[l], w_gla_o[l], w_conv_o[l], w_mix_out[l])
        h = h + rms_norm(y, g_mix_post[l])
        y = sq_relu_mlp(rms_norm(h, g_mlp_pre[l]), w_mlp_up[l], w_mlp_down[l])
        h = h + rms_norm(y, g_mlp_post[l])
    return h
```

```python
import functools

import jax
import jax.numpy as jnp
from jax import lax
from jax.experimental import pallas as pl
from jax.experimental.pallas import tpu as pltpu

D_MODEL = 1024
MOBA_HEADS = 8
MOBA_HEAD_DIM = 64
MOBA_WIDTH = MOBA_HEADS * MOBA_HEAD_DIM
MOBA_BLOCK = 256
MOBA_TOPK = 3
ROPE_THETA = 10000.0
GLA_HEADS = 4
GLA_DK = 64
GLA_DV = 128
GLA_KW = GLA_HEADS * GLA_DK
GLA_VW = GLA_HEADS * GLA_DV
GLA_RANK = 16
GLA_TEMP = 16.0
GLA_CHUNK = 64
CONV_CH = 512
CONV_WIDTH = 31
D_FF = 4 * D_MODEL
N_BRANCH = 3
EPS = 1e-6
NEG = -1e30

LANES = 128
SUBLANES = 8
VMEM_LIMIT_BYTES = 56 * 1024 * 1024

ROW_TILE = 512
CONV_ROW_TILE = 256
CONV_HALO = 32
CONV_ROW_CHUNK = 64

F32 = jnp.float32
BF16 = jnp.bfloat16


def _nt(a, b):
    return lax.dot_general(a, b, (((1,), (1,)), ((), ())), preferred_element_type=F32)


def _tn(a, b):
    return lax.dot_general(a, b, (((0,), (0,)), ((), ())), preferred_element_type=F32)


def _mm(a, b):
    return jnp.dot(a, b, preferred_element_type=F32)


def _rms(x, g):
    return x * lax.rsqrt(jnp.mean(x * x, axis=-1, keepdims=True) + EPS) * g


def _split3(x):
    hi = x.astype(BF16)
    r1 = x - hi.astype(F32)
    mid = r1.astype(BF16)
    lo = (r1 - mid.astype(F32)).astype(BF16)
    return hi, mid, lo


def _params(*sem):
    return pltpu.CompilerParams(dimension_semantics=sem, vmem_limit_bytes=VMEM_LIMIT_BYTES)


def _const_spec(shape):
    nd = len(shape)
    return pl.BlockSpec(shape, lambda *_: (0,) * nd)


def _rope_table_kernel(pos_ref, invf_ref, cos_ref, sina_ref, sinb_ref):
    ang = pos_ref[...].astype(F32) * invf_ref[...]
    lane = lax.broadcasted_iota(jnp.int32, ang.shape, 1)
    first_half = (lane % MOBA_HEAD_DIM) < (MOBA_HEAD_DIM // 2)
    s = jnp.sin(ang)
    cos_ref[...] = jnp.cos(ang)
    sina_ref[...] = jnp.where(first_half, -s, 0.0)
    sinb_ref[...] = jnp.where(first_half, 0.0, s)


def _rope_tables(positions):
    T = positions.size
    half = MOBA_HEAD_DIM // 2
    inv_freq = ROPE_THETA ** (-jnp.arange(half, dtype=F32) / half)
    invf = jnp.tile(inv_freq, LANES // half)[None, :]
    tab = jax.ShapeDtypeStruct((T, LANES), F32)
    row = pl.BlockSpec((ROW_TILE, LANES), lambda i: (i, 0))
    return pl.pallas_call(
        _rope_table_kernel,
        grid=(T // ROW_TILE,),
        in_specs=[pl.BlockSpec((ROW_TILE, 1), lambda i: (i, 0)), _const_spec((1, LANES))],
        out_specs=[row, row, row],
        out_shape=[tab, tab, tab],
        compiler_params=_params("parallel"),
        name="rope_tables",
    )(positions.reshape(T, 1), invf)


def _moba_proj_kernel(x_ref, g_ref, w_ref, cos_ref, sina_ref, sinb_ref, q_ref, k_ref, v_ref):
    xn = _rms(x_ref[...], g_ref[...]).astype(BF16)
    z = _mm(xn, w_ref[...])
    cos, sa, sb = cos_ref[...], sina_ref[...], sinb_ref[...]
    for out_ref, base in ((q_ref, 0), (k_ref, MOBA_WIDTH)):
        for c in range(MOBA_WIDTH // LANES):
            s = z[:, base + c * LANES: base + (c + 1) * LANES]
            r = s * cos + pltpu.roll(s, LANES - 32, 1) * sa + pltpu.roll(s, 32, 1) * sb
            out_ref[:, c * LANES:(c + 1) * LANES] = r.astype(BF16)
    v_ref[...] = z[:, 2 * MOBA_WIDTH:].astype(BF16)


def _moba_proj(x, g, w, cos, sina, sinb):
    T = x.shape[0]
    row = lambda n: pl.BlockSpec((ROW_TILE, n), lambda i: (i, 0))
    o = jax.ShapeDtypeStruct((T, MOBA_WIDTH), BF16)
    return pl.pallas_call(
        _moba_proj_kernel,
        grid=(T // ROW_TILE,),
        in_specs=[row(D_MODEL), _const_spec((1, D_MODEL)), _const_spec(w.shape),
                  row(LANES), row(LANES), row(LANES)],
        out_specs=[row(MOBA_WIDTH)] * 3,
        out_shape=[o, o, o],
        compiler_params=_params("parallel"),
        name="moba_proj",
    )(x, g, w, cos, sina, sinb)


def _moba_attn_kernel(q_ref, k_ref, v_ref, o_ref, kmean_ref, vt_ref, sel_ref, m_ref, l_ref, acc_ref,
                      *, n_blocks):
    i = pl.program_id(2)
    blk = MOBA_BLOCK
    pair = LANES // MOBA_HEAD_DIM

    @pl.when(i == 0)
    def _():
        kmean_ref[...] = jnp.zeros_like(kmean_ref)
        for j in range(n_blocks):
            kj = k_ref[j * blk:(j + 1) * blk, :].astype(F32)
            kmean_ref[j:j + 1, :] = jnp.mean(kj, axis=0, keepdims=True)
            vt_ref[j] = v_ref[j * blk:(j + 1) * blk, :].astype(F32).T.astype(BF16)

    q = q_ref[...]
    qlane = lax.broadcasted_iota(jnp.int32, q.shape, 1) // MOBA_HEAD_DIM
    km_hi, km_mid, km_lo = _split3(kmean_ref[...])
    nrow = kmean_ref.shape[0]
    jidx = lax.broadcasted_iota(jnp.int32, (nrow, blk), 0)
    pastf = jnp.where(jidx < i, 1.0, 0.0)
    kpos = lax.broadcasted_iota(jnp.int32, (blk, blk), 0)
    qpos = lax.broadcasted_iota(jnp.int32, (blk, blk), 1)
    causal = kpos <= qpos
    k_own = k_ref[pl.ds(pl.multiple_of(i * blk, blk), blk), :]
    qs = []
    for h in range(pair):
        qh = jnp.where(qlane == h, q, jnp.zeros_like(q))
        gate = _nt(km_hi, qh) + _nt(km_mid, qh) + _nt(km_lo, qh)
        for n in range(n_blocks):
            gn = gate[n:n + 1, :]
            beats = (gate > gn) | ((gate == gn) & (jidx < n))
            rank = jnp.sum(jnp.where(beats, pastf, 0.0), axis=0, keepdims=True)
            sel_ref[h, n:n + 1, :] = jnp.where(rank < MOBA_TOPK, 1.0, 0.0) * jnp.where(i > n, 1.0, 0.0)
        qh = qh * jnp.asarray(MOBA_HEAD_DIM ** -0.5, BF16)
        qs.append(qh)
        s = jnp.where(causal, _nt(k_own, qh), NEG)
        m = jnp.max(s, axis=0, keepdims=True)
        p = jnp.exp(s - m)
        m_ref[h] = m
        l_ref[h] = jnp.sum(p, axis=0, keepdims=True)
        acc_ref[h] = _mm(vt_ref[i], p.astype(BF16))

    def body(j, carry):
        kj = k_ref[pl.ds(pl.multiple_of(j * blk, blk), blk), :]
        vt = vt_ref[j]
        for h in range(pair):
            valid = sel_ref[h, pl.ds(j, 1), :] > 0.5
            s = jnp.where(valid, _nt(kj, qs[h]), NEG)
            m_old = m_ref[h]
            m_new = jnp.maximum(m_old, jnp.max(s, axis=0, keepdims=True))
            alpha = jnp.exp(m_old - m_new)
            p = jnp.exp(s - m_new)
            m_ref[h] = m_new
            l_ref[h] = alpha * l_ref[h] + jnp.sum(p, axis=0, keepdims=True)
            acc_ref[h] = alpha * acc_ref[h] + _mm(vt, p.astype(BF16))
        return carry

    lax.fori_loop(0, i, body, 0)

    row = lax.broadcasted_iota(jnp.int32, (LANES, blk), 0) // MOBA_HEAD_DIM
    out_t = jnp.zeros((LANES, blk), F32)
    for h in range(pair):
        out_t = jnp.where(row == h, acc_ref[h] / l_ref[h], out_t)
    o_ref[...] = out_t.T.astype(BF16)


def _moba_attn(q, k, v, batch, seq):
    n_blocks = seq // MOBA_BLOCK
    n_pairs = MOBA_WIDTH // LANES
    pair = LANES // MOBA_HEAD_DIM
    qspec = pl.BlockSpec((MOBA_BLOCK, LANES), lambda b, p, i: (b * n_blocks + i, p))
    kvspec = pl.BlockSpec((seq, LANES), lambda b, p, i: (b, p))
    gate_rows = 2 * SUBLANES
    assert n_blocks <= gate_rows
    return pl.pallas_call(
        functools.partial(_moba_attn_kernel, n_blocks=n_blocks),
        grid=(batch, n_pairs, n_blocks),
        in_specs=[qspec, kvspec, kvspec],
        out_specs=qspec,
        out_shape=jax.ShapeDtypeStruct(q.shape, BF16),
        scratch_shapes=[
            pltpu.VMEM((gate_rows, LANES), F32),
            pltpu.VMEM((n_blocks, LANES, MOBA_BLOCK), BF16),
            pltpu.VMEM((pair, n_blocks, MOBA_BLOCK), F32),
            pltpu.VMEM((pair, 1, MOBA_BLOCK), F32),
            pltpu.VMEM((pair, 1, MOBA_BLOCK), F32),
            pltpu.VMEM((pair, LANES, MOBA_BLOCK), F32),
        ],
        compiler_params=_params("parallel", "parallel", "arbitrary"),
        name="moba_attn",
    )(q, k, v)


def _gla_proj_kernel(x_ref, g_ref, w_ref, wla_ref, wa2_ref, ba_ref, q_ref, k_ref, v_ref, r_ref, d_ref):
    xn = _rms(x_ref[...], g_ref[...]).astype(BF16)
    z = _mm(xn, w_ref[...])
    q_ref[...] = z[:, :GLA_KW].astype(BF16)
    k_ref[...] = z[:, GLA_KW:2 * GLA_KW].astype(BF16)
    v_ref[...] = z[:, 2 * GLA_KW:2 * GLA_KW + GLA_VW].astype(BF16)
    r_ref[...] = z[:, 2 * GLA_KW + GLA_VW:].astype(BF16)
    la = _mm(xn, wla_ref[...])
    pre = _mm(la.astype(BF16), wa2_ref[...]) + ba_ref[...]
    log_sig = jnp.minimum(pre, 0.0) - jnp.log1p(jnp.exp(-jnp.abs(pre)))
    d_ref[...] = log_sig / GLA_TEMP


def _gla_proj(x, g, w, wla, wa2, ba):
    T = x.shape[0]
    row = lambda n: pl.BlockSpec((ROW_TILE, n), lambda i: (i, 0))
    sds = lambda n, dt: jax.ShapeDtypeStruct((T, n), dt)
    return pl.pallas_call(
        _gla_proj_kernel,
        grid=(T // ROW_TILE,),
        in_specs=[row(D_MODEL), _const_spec((1, D_MODEL)), _const_spec(w.shape), _const_spec(wla.shape),
                  _const_spec(wa2.shape), _const_spec(ba.shape)],
        out_specs=[row(GLA_KW), row(GLA_KW), row(GLA_VW), row(GLA_VW), row(GLA_KW)],
        out_shape=[sds(GLA_KW, BF16), sds(GLA_KW, BF16), sds(GLA_VW, BF16), sds(GLA_VW, BF16),
                   sds(GLA_KW, F32)],
        compiler_params=_params("parallel"),
        name="gla_proj",
    )(x, g, w, wla, wa2, ba)


def _gla_scan_kernel(q_ref, k_ref, v_ref, r_ref, d_ref, gn_ref, o_ref, st_ref, *, seq):
    ch = GLA_CHUNK
    pair = LANES // GLA_DK
    st_ref[...] = jnp.zeros_like(st_ref)
    ri = lax.broadcasted_iota(jnp.int32, (ch, ch), 0)
    ci = lax.broadcasted_iota(jnp.int32, (ch, ch), 1)
    lower = ri >= ci
    tril = jnp.where(lower, 1.0, 0.0).astype(BF16)
    lane_head = lax.broadcasted_iota(jnp.int32, (ch, LANES), 1) // GLA_DK
    st_lane_head = lax.broadcasted_iota(jnp.int32, (GLA_DV, LANES), 1) // GLA_DK
    gn = gn_ref[...]

    def body(c, carry):
        rows = pl.ds(pl.multiple_of(c * ch, ch), ch)
        d_hi, d_mid, d_lo = _split3(d_ref[rows, :])
        G = _mm(tril, d_hi) + _mm(tril, d_mid) + _mm(tril, d_lo)
        q = q_ref[rows, :].astype(F32)
        k = k_ref[rows, :].astype(F32)
        g_last = G[ch - 1:ch, :]
        qt = (q * jnp.exp(G) * (GLA_DK ** -0.5)).astype(BF16)
        kt = (k * jnp.exp(-G)).astype(BF16)
        kd = (k * jnp.exp(g_last - G)).astype(BF16)
        st = st_ref[...]
        st_b = st.astype(BF16)
        upd = jnp.zeros_like(st)
        for h in range(pair):
            cols = slice(h * GLA_DV, (h + 1) * GLA_DV)
            qh = jnp.where(lane_head == h, qt, jnp.zeros_like(qt))
            a = jnp.where(lower, _nt(qh, kt), 0.0).astype(BF16)
            vh = v_ref[rows, cols]
            o = _mm(a, vh) + _nt(qh, st_b)
            o = _rms(o, gn)
            rh = r_ref[rows, cols].astype(F32)
            o_ref[rows, cols] = (o * (rh * jax.nn.sigmoid(rh))).astype(BF16)
            upd = jnp.where(st_lane_head == h, _tn(vh, kd), upd)
        st_ref[...] = st * jnp.exp(g_last) + upd
        return carry

    lax.fori_loop(0, seq // ch, body, 0)


def _gla_scan(q, k, v, r, d, gn, batch, seq):
    n_pairs = GLA_KW // LANES
    pair = LANES // GLA_DK
    kspec = pl.BlockSpec((seq, LANES), lambda b, p: (b, p))
    vspec = pl.BlockSpec((seq, pair * GLA_DV), lambda b, p: (b, p))
    return pl.pallas_call(
        functools.partial(_gla_scan_kernel, seq=seq),
        grid=(batch, n_pairs),
        in_specs=[kspec, kspec, vspec, vspec, kspec, _const_spec((1, GLA_DV))],
        out_specs=vspec,
        out_shape=jax.ShapeDtypeStruct(v.shape, BF16),
        scratch_shapes=[pltpu.VMEM((GLA_DV, LANES), F32)],
        compiler_params=_params("parallel", "parallel"),
        name="gla_scan",
    )(q, k, v, r, d, gn)


def _conv_kernel(x_ref, g_ref, w_ref, wdw_ref, bdw_ref, gln_ref, bln_ref, o_ref, u_ref, y_ref):
    ts, halo = CONV_ROW_TILE, CONV_HALO
    i = pl.program_id(1)

    @pl.when(i == 0)
    def _():
        u_ref[0:halo, :] = jnp.zeros((halo, CONV_CH), F32)

    @pl.when(i > 0)
    def _():
        u_ref[0:halo, :] = u_ref[ts:ts + halo, :]

    xn = _rms(x_ref[...], g_ref[...]).astype(BF16)
    z = _mm(xn, w_ref[...])
    u_ref[halo:halo + ts, :] = z[:, :CONV_CH] * jax.nn.sigmoid(z[:, CONV_CH:])
    first = halo - (CONV_WIDTH - 1)
    for c in range(CONV_CH // LANES):
        cols = slice(c * LANES, (c + 1) * LANES)
        for r in range(0, ts, CONV_ROW_CHUNK):
            acc = jnp.broadcast_to(bdw_ref[:, cols], (CONV_ROW_CHUNK, LANES))
            for w in range(CONV_WIDTH):
                acc = acc + wdw_ref[w:w + 1, cols] * u_ref[r + first + w:r + first + w + CONV_ROW_CHUNK, cols]
            y_ref[r:r + CONV_ROW_CHUNK, cols] = acc
    y = y_ref[...]
    mu = jnp.mean(y, axis=-1, keepdims=True)
    yc = y - mu
    var = jnp.mean(yc * yc, axis=-1, keepdims=True)
    yn = yc * lax.rsqrt(var + EPS) * gln_ref[...] + bln_ref[...]
    o_ref[...] = (yn * jax.nn.sigmoid(yn)).astype(BF16)


def _conv_branch(x, g, w, wdw, bdw, gln, bln, batch, seq):
    T = x.shape[0]
    nt = seq // CONV_ROW_TILE
    return pl.pallas_call(
        _conv_kernel,
        grid=(batch, nt),
        in_specs=[pl.BlockSpec((CONV_ROW_TILE, D_MODEL), lambda b, i: (b * nt + i, 0)),
                  _const_spec((1, D_MODEL)), _const_spec(w.shape), _const_spec(wdw.shape),
                  _const_spec((1, CONV_CH)), _const_spec((1, CONV_CH)), _const_spec((1, CONV_CH))],
        out_specs=pl.BlockSpec((CONV_ROW_TILE, CONV_CH), lambda b, i: (b * nt + i, 0)),
        out_shape=jax.ShapeDtypeStruct((T, CONV_CH), BF16),
        scratch_shapes=[pltpu.VMEM((CONV_ROW_TILE + CONV_HALO, CONV_CH), F32),
                        pltpu.VMEM((CONV_ROW_TILE, CONV_CH), F32)],
        compiler_params=_params("parallel", "arbitrary"),
        name="conv_branch",
    )(x, g, w, wdw, bdw, gln, bln)


def _combine_kernel(x_ref, g_ref, wg_ref, bg_ref, a_ref, b_ref, c_ref, wa_ref, wb_ref, wc_ref,
                    wo_ref, gp_ref, o_ref):
    x = x_ref[...]
    xn = _rms(x, g_ref[...]).astype(BF16)
    mix = None
    for n, (br_ref, wbr_ref) in enumerate(((a_ref, wa_ref), (b_ref, wb_ref), (c_ref, wc_ref))):
        cols = slice(n * D_MODEL, (n + 1) * D_MODEL)
        gate = jax.nn.sigmoid(_mm(xn, wg_ref[:, cols]) + bg_ref[:, cols])
        term = gate * _mm(br_ref[...], wbr_ref[...])
        mix = term if mix is None else mix + term
    y = _mm(mix.astype(BF16), wo_ref[...])
    o_ref[...] = x + _rms(y, gp_ref[...])


def _combine(x, g, wg, bg, a, b, c, wa, wb, wc, wo, gp):
    T = x.shape[0]
    row = lambda n: pl.BlockSpec((ROW_TILE, n), lambda i: (i, 0))
    return pl.pallas_call(
        _combine_kernel,
        grid=(T // ROW_TILE,),
        in_specs=[row(D_MODEL), _const_spec((1, D_MODEL)), _const_spec(wg.shape), _const_spec(bg.shape),
                  row(MOBA_WIDTH), row(GLA_VW), row(CONV_CH),
                  _const_spec(wa.shape), _const_spec(wb.shape), _const_spec(wc.shape),
                  _const_spec(wo.shape), _const_spec((1, D_MODEL))],
        out_specs=row(D_MODEL),
        out_shape=jax.ShapeDtypeStruct((T, D_MODEL), F32),
        compiler_params=_params("parallel"),
        name="combine",
    )(x, g, wg, bg, a, b, c, wa, wb, wc, wo, gp)


def _mlp_kernel(x_ref, g_ref, wu_ref, wd_ref, gp_ref, o_ref):
    x = x_ref[...]
    xn = _rms(x, g_ref[...]).astype(BF16)
    acc = None
    for c in range(D_FF // D_MODEL):
        cols = slice(c * D_MODEL, (c + 1) * D_MODEL)
        u = jnp.maximum(_mm(xn, wu_ref[:, cols]), 0.0)
        t = _mm((u * u).astype(BF16), wd_ref[cols, :])
        acc = t if acc is None else acc + t
    o_ref[...] = x + _rms(acc, gp_ref[...])


def _mlp(x, g, wu, wd, gp):
    T = x.shape[0]
    row = pl.BlockSpec((ROW_TILE, D_MODEL), lambda i: (i, 0))
    return pl.pallas_call(
        _mlp_kernel,
        grid=(T // ROW_TILE,),
        in_specs=[row, _const_spec((1, D_MODEL)), _const_spec(wu.shape), _const_spec(wd.shape),
                  _const_spec((1, D_MODEL))],
        out_specs=row,
        out_shape=jax.ShapeDtypeStruct((T, D_MODEL), F32),
        compiler_params=_params("parallel"),
        name="mlp",
    )(x, g, wu, wd, gp)


def kernel(x, positions, g_mix_pre, w_in, b_gate, w_gla_a2, b_gla_a, g_gla_norm, w_dw, b_dw, g_conv_ln,
           b_conv_ln, w_moba_o, w_gla_o, w_conv_o, w_mix_out, g_mix_post, g_mlp_pre, w_mlp_up, w_mlp_down,
           g_mlp_post):
    batch, seq, _ = x.shape
    depth = w_in.shape[0]
    assert seq % MOBA_BLOCK == 0 and seq % CONV_ROW_TILE == 0 and (batch * seq) % ROW_TILE == 0
    h = x.reshape(batch * seq, D_MODEL)
    cos, sina, sinb = _rope_tables(positions)

    o_gla = 3 * MOBA_WIDTH
    o_la = o_gla + 2 * GLA_KW + GLA_VW
    o_lr = o_la + GLA_RANK
    o_conv = o_lr + GLA_VW
    o_gate = o_conv + 2 * CONV_CH
    row = lambda v: v[None, :]

    for l in range(depth):
        w = w_in[l]
        g_pre = row(g_mix_pre[l])
        w_moba = w[:, :o_gla].astype(BF16)
        w_gla = jnp.concatenate([w[:, o_gla:o_la], w[:, o_lr:o_conv]], axis=1).astype(BF16)
        w_la = jnp.pad(w[:, o_la:o_lr], ((0, 0), (0, LANES - GLA_RANK))).astype(BF16)
        w_a2 = jnp.pad(w_gla_a2[l], ((0, LANES - GLA_RANK), (0, 0))).astype(BF16)
        w_conv = w[:, o_conv:o_gate].astype(BF16)
        w_gate = w[:, o_gate:].astype(BF16)

        mq, mk, mv = _moba_proj(h, g_pre, w_moba, cos, sina, sinb)
        oa = _moba_attn(mq, mk, mv, batch, seq)
        lq, lk, lv, lr, ld = _gla_proj(h, g_pre, w_gla, w_la, w_a2, row(b_gla_a[l]))
        ob = _gla_scan(lq, lk, lv, lr, ld, row(g_gla_norm[l]), batch, seq)
        oc = _conv_branch(h, g_pre, w_conv, w_dw[l], row(b_dw[l]), row(g_conv_ln[l]), row(b_conv_ln[l]),
                          batch, seq)
        h = _combine(h, g_pre, w_gate, row(b_gate[l]), oa, ob, oc,
                     w_moba_o[l].astype(BF16), w_gla_o[l].astype(BF16), w_conv_o[l].astype(BF16),
                     w_mix_out[l].astype(BF16), row(g_mix_post[l]))
        h = _mlp(h, row(g_mlp_pre[l]), w_mlp_up[l].astype(BF16), w_mlp_down[l].astype(BF16),
                 row(g_mlp_post[l]))
    return h.reshape(batch, seq, D_MODEL)
```

```python
import functools

import jax
import jax.numpy as jnp
from jax import lax
from jax.experimental import pallas as pl
from jax.experimental.pallas import tpu as pltpu

D_MODEL = 1024
MOBA_HEADS = 8
MOBA_HEAD_DIM = 64
MOBA_WIDTH = MOBA_HEADS * MOBA_HEAD_DIM
MOBA_BLOCK = 256
MOBA_TOPK = 3
ROPE_THETA = 10000.0
GLA_HEADS = 4
GLA_DK = 64
GLA_DV = 128
GLA_KW = GLA_HEADS * GLA_DK
GLA_VW = GLA_HEADS * GLA_DV
GLA_RANK = 16
GLA_TEMP = 16.0
GLA_CHUNK = 64
GLA_SUPER = 256
CONV_CH = 512
CONV_WIDTH = 31
D_FF = 4 * D_MODEL
N_BRANCH = 3
EPS = 1e-6
NEG = -1e30

LANES = 128
SUBLANES = 8
VMEM_LIMIT_BYTES = 56 * 1024 * 1024

ROW_TILE = 512
CONV_ROW_TILE = 256
CONV_HALO = 32
CONV_ROW_CHUNK = 64

F32 = jnp.float32
BF16 = jnp.bfloat16


def _nt(a, b):
    return lax.dot_general(a, b, (((1,), (1,)), ((), ())), preferred_element_type=F32)


def _tn(a, b):
    return lax.dot_general(a, b, (((0,), (0,)), ((), ())), preferred_element_type=F32)


def _mm(a, b):
    return jnp.dot(a, b, preferred_element_type=F32)


def _rms(x, g):
    return x * lax.rsqrt(jnp.mean(x * x, axis=-1, keepdims=True) + EPS) * g


def _split3(x):
    hi = x.astype(BF16)
    r1 = x - hi.astype(F32)
    mid = r1.astype(BF16)
    lo = (r1 - mid.astype(F32)).astype(BF16)
    return hi, mid, lo


def _params(*sem):
    return pltpu.CompilerParams(dimension_semantics=sem, vmem_limit_bytes=VMEM_LIMIT_BYTES)


def _const_spec(shape):
    nd = len(shape)
    return pl.BlockSpec(shape, lambda *_: (0,) * nd)


def _rope_table_kernel(pos_ref, invf_ref, cos_ref, sina_ref, sinb_ref):
    ang = pos_ref[...].astype(F32) * invf_ref[...]
    lane = lax.broadcasted_iota(jnp.int32, ang.shape, 1)
    first_half = (lane % MOBA_HEAD_DIM) < (MOBA_HEAD_DIM // 2)
    s = jnp.sin(ang)
    cos_ref[...] = jnp.cos(ang)
    sina_ref[...] = jnp.where(first_half, -s, 0.0)
    sinb_ref[...] = jnp.where(first_half, 0.0, s)


def _rope_tables(positions):
    T = positions.size
    half = MOBA_HEAD_DIM // 2
    inv_freq = ROPE_THETA ** (-jnp.arange(half, dtype=F32) / half)
    invf = jnp.tile(inv_freq, LANES // half)[None, :]
    tab = jax.ShapeDtypeStruct((T, LANES), F32)
    row = pl.BlockSpec((ROW_TILE, LANES), lambda i: (i, 0))
    return pl.pallas_call(
        _rope_table_kernel,
        grid=(T // ROW_TILE,),
        in_specs=[pl.BlockSpec((ROW_TILE, 1), lambda i: (i, 0)), _const_spec((1, LANES))],
        out_specs=[row, row, row],
        out_shape=[tab, tab, tab],
        compiler_params=_params("parallel"),
        name="rope_tables",
    )(positions.reshape(T, 1), invf)


def _moba_proj_kernel(x_ref, g_ref, w_ref, cos_ref, sina_ref, sinb_ref, q_ref, k_ref, v_ref):
    xn = _rms(x_ref[...], g_ref[...]).astype(BF16)
    z = _mm(xn, w_ref[...])
    cos, sa, sb = cos_ref[...], sina_ref[...], sinb_ref[...]
    for out_ref, base in ((q_ref, 0), (k_ref, MOBA_WIDTH)):
        for c in range(MOBA_WIDTH // LANES):
            s = z[:, base + c * LANES: base + (c + 1) * LANES]
            r = s * cos + pltpu.roll(s, LANES - 32, 1) * sa + pltpu.roll(s, 32, 1) * sb
            out_ref[:, c * LANES:(c + 1) * LANES] = r.astype(BF16)
    v_ref[...] = z[:, 2 * MOBA_WIDTH:].astype(BF16)


def _moba_proj(x, g, w, cos, sina, sinb):
    T = x.shape[0]
    row = lambda n: pl.BlockSpec((ROW_TILE, n), lambda i: (i, 0))
    o = jax.ShapeDtypeStruct((T, MOBA_WIDTH), BF16)
    return pl.pallas_call(
        _moba_proj_kernel,
        grid=(T // ROW_TILE,),
        in_specs=[row(D_MODEL), _const_spec((1, D_MODEL)), _const_spec(w.shape),
                  row(LANES), row(LANES), row(LANES)],
        out_specs=[row(MOBA_WIDTH)] * 3,
        out_shape=[o, o, o],
        compiler_params=_params("parallel"),
        name="moba_proj",
    )(x, g, w, cos, sina, sinb)


def _moba_attn_kernel(q_ref, k_ref, v_ref, o_ref, kmean_ref, vt_ref, s_ref, *, n_blocks):
    i = pl.program_id(2)
    blk = MOBA_BLOCK

    @pl.when(i == 0)
    def _():
        kmean_ref[...] = jnp.zeros_like(kmean_ref)
        for j in range(n_blocks):
            kj = k_ref[j * blk:(j + 1) * blk, :].astype(F32)
            kmean_ref[j:j + 1, :] = jnp.mean(kj, axis=0, keepdims=True)
            vt_ref[j] = v_ref[j * blk:(j + 1) * blk, :].astype(F32).T.astype(BF16)

    for ii in range(n_blocks):
        @pl.when(i == ii)
        def _(ii=ii):
            _moba_query_block(ii, q_ref, k_ref, o_ref, kmean_ref, vt_ref, s_ref)


def _moba_query_block(ii, q_ref, k_ref, o_ref, kmean_ref, vt_ref, s_ref):
    blk = MOBA_BLOCK
    pair = LANES // MOBA_HEAD_DIM
    q = q_ref[...]
    qlane = lax.broadcasted_iota(jnp.int32, q.shape, 1) // MOBA_HEAD_DIM
    kpos = lax.broadcasted_iota(jnp.int32, (blk, blk), 0)
    qpos = lax.broadcasted_iota(jnp.int32, (blk, blk), 1)
    causal = kpos <= qpos
    need_gate = ii > MOBA_TOPK
    if need_gate:
        km_hi, km_mid, km_lo = _split3(kmean_ref[...])
        jidx = lax.broadcasted_iota(jnp.int32, (kmean_ref.shape[0], blk), 0)
    sub = lambda t: t.reshape(blk // SUBLANES, SUBLANES, blk)
    outs = []
    for h in range(pair):
        qh = jnp.where(qlane == h, q, jnp.zeros_like(q))
        valid = [None] * ii
        if need_gate:
            gate = _nt(km_hi, qh) + _nt(km_mid, qh) + _nt(km_lo, qh)
            for n in range(ii):
                gn = gate[n:n + 1, :]
                beats = ((gate > gn) | ((gate == gn) & (jidx < n))) & (jidx < ii)
                rank = jnp.sum(jnp.where(beats, 1.0, 0.0), axis=0, keepdims=True)
                valid[n] = rank < MOBA_TOPK
        qh = qh * jnp.asarray(MOBA_HEAD_DIM ** -0.5, BF16)
        m8 = None
        for j in range(ii + 1):
            s = _nt(k_ref[j * blk:(j + 1) * blk, :], qh)
            if j == ii:
                s = jnp.where(causal, s, NEG)
            elif valid[j] is not None:
                s = jnp.where(valid[j], s, NEG)
            s_ref[h, j] = s
            t = jnp.max(sub(s), axis=0)
            m8 = t if m8 is None else jnp.maximum(m8, t)
        m = jnp.max(m8, axis=0, keepdims=True)
        l8 = None
        acc = None
        for j in range(ii + 1):
            p = jnp.exp(s_ref[h, j] - m)
            t = jnp.sum(sub(p), axis=0)
            l8 = t if l8 is None else l8 + t
            t = _mm(vt_ref[j], p.astype(BF16))
            acc = t if acc is None else acc + t
        outs.append(acc / jnp.sum(l8, axis=0, keepdims=True))
    row = lax.broadcasted_iota(jnp.int32, (LANES, blk), 0) // MOBA_HEAD_DIM
    out_t = outs[0]
    for h in range(1, pair):
        out_t = jnp.where(row == h, outs[h], out_t)
    o_ref[...] = out_t.T.astype(BF16)


def _moba_attn(q, k, v, batch, seq):
    n_blocks = seq // MOBA_BLOCK
    n_pairs = MOBA_WIDTH // LANES
    pair = LANES // MOBA_HEAD_DIM
    qspec = pl.BlockSpec((MOBA_BLOCK, LANES), lambda b, p, i: (b * n_blocks + i, p))
    kvspec = pl.BlockSpec((seq, LANES), lambda b, p, i: (b, p))
    gate_rows = 2 * SUBLANES
    assert n_blocks <= gate_rows
    return pl.pallas_call(
        functools.partial(_moba_attn_kernel, n_blocks=n_blocks),
        grid=(batch, n_pairs, n_blocks),
        in_specs=[qspec, kvspec, kvspec],
        out_specs=qspec,
        out_shape=jax.ShapeDtypeStruct(q.shape, BF16),
        scratch_shapes=[
            pltpu.VMEM((gate_rows, LANES), F32),
            pltpu.VMEM((n_blocks, LANES, MOBA_BLOCK), BF16),
            pltpu.VMEM((pair, n_blocks, MOBA_BLOCK, MOBA_BLOCK), F32),
        ],
        compiler_params=_params("parallel", "parallel", "arbitrary"),
        name="moba_attn",
    )(q, k, v)


def _gla_proj_kernel(x_ref, g_ref, w_ref, wla_ref, wa2_ref, ba_ref, q_ref, k_ref, v_ref, r_ref, d_ref):
    xn = _rms(x_ref[...], g_ref[...]).astype(BF16)
    z = _mm(xn, w_ref[...])
    q_ref[...] = z[:, :GLA_KW].astype(BF16)
    k_ref[...] = z[:, GLA_KW:2 * GLA_KW].astype(BF16)
    v_ref[...] = z[:, 2 * GLA_KW:2 * GLA_KW + GLA_VW].astype(BF16)
    r_ref[...] = z[:, 2 * GLA_KW + GLA_VW:].astype(BF16)
    la = _mm(xn, wla_ref[...])
    pre = _mm(la.astype(BF16), wa2_ref[...]) + ba_ref[...]
    log_sig = jnp.minimum(pre, 0.0) - jnp.log1p(jnp.exp(-jnp.abs(pre)))
    d_ref[...] = log_sig / GLA_TEMP


def _gla_proj(x, g, w, wla, wa2, ba):
    T = x.shape[0]
    row = lambda n: pl.BlockSpec((ROW_TILE, n), lambda i: (i, 0))
    sds = lambda n, dt: jax.ShapeDtypeStruct((T, n), dt)
    return pl.pallas_call(
        _gla_proj_kernel,
        grid=(T // ROW_TILE,),
        in_specs=[row(D_MODEL), _const_spec((1, D_MODEL)), _const_spec(w.shape), _const_spec(wla.shape),
                  _const_spec(wa2.shape), _const_spec(ba.shape)],
        out_specs=[row(GLA_KW), row(GLA_KW), row(GLA_VW), row(GLA_VW), row(GLA_KW)],
        out_shape=[sds(GLA_KW, BF16), sds(GLA_KW, BF16), sds(GLA_VW, BF16), sds(GLA_VW, BF16),
                   sds(GLA_KW, F32)],
        compiler_params=_params("parallel"),
        name="gla_proj",
    )(x, g, w, wla, wa2, ba)


def _gla_scan_kernel(q_ref, k_ref, v_ref, r_ref, d_ref, gn_ref, o_ref, *, seq):
    ch, sc = GLA_CHUNK, GLA_SUPER
    nc = sc // ch
    pair = LANES // GLA_DK
    ri = lax.broadcasted_iota(jnp.int32, (sc, sc), 0)
    ci = lax.broadcasted_iota(jnp.int32, (sc, sc), 1)
    lower = (ri >= ci) & (ri // ch == ci // ch)
    tril = jnp.where(lower, 1.0, 0.0).astype(BF16)
    lane_head = lax.broadcasted_iota(jnp.int32, (sc, LANES), 1) // GLA_DK
    st_lane_head = lax.broadcasted_iota(jnp.int32, (GLA_DV, LANES), 1) // GLA_DK
    gn = gn_ref[...]

    def body(t, st):
        rows = pl.ds(pl.multiple_of(t * sc, sc), sc)
        d_hi, d_mid, d_lo = _split3(d_ref[rows, :])
        G = _mm(tril, d_hi) + _mm(tril, d_mid) + _mm(tril, d_lo)
        g_last = [G[(c + 1) * ch - 1:(c + 1) * ch, :] for c in range(nc)]
        g_last_rows = jnp.concatenate([jnp.broadcast_to(g, (ch, LANES)) for g in g_last], axis=0)
        q = q_ref[rows, :].astype(F32)
        k = k_ref[rows, :].astype(F32)
        qt = (q * jnp.exp(G) * (GLA_DK ** -0.5)).astype(BF16)
        kt = (k * jnp.exp(-G)).astype(BF16)
        kd = (k * jnp.exp(g_last_rows - G)).astype(BF16)
        qh, vh, o_intra, upd = [], [], [], []
        for h in range(pair):
            cols = slice(h * GLA_DV, (h + 1) * GLA_DV)
            qh.append(jnp.where(lane_head == h, qt, jnp.zeros_like(qt)))
            a = jnp.where(lower, _nt(qh[h], kt), 0.0).astype(BF16)
            vh.append(v_ref[rows, cols])
            o_intra.append(_mm(a, vh[h]))
            upd.append([_tn(vh[h][c * ch:(c + 1) * ch], kd[c * ch:(c + 1) * ch]) for c in range(nc)])
        o_inter = [[] for _ in range(pair)]
        for c in range(nc):
            st_b = st.astype(BF16)
            u = upd[0][c]
            for h in range(pair):
                o_inter[h].append(_nt(qh[h][c * ch:(c + 1) * ch], st_b))
                if h:
                    u = jnp.where(st_lane_head == h, upd[h][c], u)
            st = st * jnp.exp(g_last[c]) + u
        for h in range(pair):
            cols = slice(h * GLA_DV, (h + 1) * GLA_DV)
            o = _rms(o_intra[h] + jnp.concatenate(o_inter[h], axis=0), gn)
            rh = r_ref[rows, cols].astype(F32)
            o_ref[rows, cols] = (o * (rh * jax.nn.sigmoid(rh))).astype(BF16)
        return st

    lax.fori_loop(0, seq // sc, body, jnp.zeros((GLA_DV, LANES), F32))


def _gla_scan(q, k, v, r, d, gn, batch, seq):
    n_pairs = GLA_KW // LANES
    pair = LANES // GLA_DK
    kspec = pl.BlockSpec((seq, LANES), lambda b, p: (b, p))
    vspec = pl.BlockSpec((seq, pair * GLA_DV), lambda b, p: (b, p))
    return pl.pallas_call(
        functools.partial(_gla_scan_kernel, seq=seq),
        grid=(batch, n_pairs),
        in_specs=[kspec, kspec, vspec, vspec, kspec, _const_spec((1, GLA_DV))],
        out_specs=vspec,
        out_shape=jax.ShapeDtypeStruct(v.shape, BF16),
        compiler_params=_params("parallel", "parallel"),
        name="gla_scan",
    )(q, k, v, r, d, gn)


def _conv_kernel(x_ref, g_ref, w_ref, wdw_ref, bdw_ref, gln_ref, bln_ref, o_ref, u_ref, us_ref, y_ref):
    ts, halo = CONV_ROW_TILE, CONV_HALO
    i = pl.program_id(1)

    @pl.when(i == 0)
    def _():
        u_ref[0:halo, :] = jnp.zeros((halo, CONV_CH), F32)

    @pl.when(i > 0)
    def _():
        u_ref[0:halo, :] = u_ref[ts:ts + halo, :]

    xn = _rms(x_ref[...], g_ref[...]).astype(BF16)
    z = _mm(xn, w_ref[...])
    u_ref[halo:halo + ts, :] = z[:, :CONV_CH] * jax.nn.sigmoid(z[:, CONV_CH:])
    first = halo - (CONV_WIDTH - 1)
    span = us_ref.shape[1]
    for b in range(1, SUBLANES):
        us_ref[b - 1] = u_ref[b:b + span, :]
    for c in range(CONV_CH // LANES):
        cols = slice(c * LANES, (c + 1) * LANES)
        for r in range(0, ts, CONV_ROW_CHUNK):
            acc = jnp.broadcast_to(bdw_ref[:, cols], (CONV_ROW_CHUNK, LANES))
            for w in range(CONV_WIDTH):
                a, b = divmod(first + w, SUBLANES)
                lo = r + a * SUBLANES
                tap = u_ref[lo:lo + CONV_ROW_CHUNK, cols] if b == 0 else us_ref[b - 1, lo:lo + CONV_ROW_CHUNK, cols]
                acc = acc + wdw_ref[w:w + 1, cols] * tap
            y_ref[r:r + CONV_ROW_CHUNK, cols] = acc
    y = y_ref[...]
    mu = jnp.mean(y, axis=-1, keepdims=True)
    yc = y - mu
    var = jnp.mean(yc * yc, axis=-1, keepdims=True)
    yn = yc * lax.rsqrt(var + EPS) * gln_ref[...] + bln_ref[...]
    o_ref[...] = (yn * jax.nn.sigmoid(yn)).astype(BF16)


def _conv_branch(x, g, w, wdw, bdw, gln, bln, batch, seq):
    T = x.shape[0]
    nt = seq // CONV_ROW_TILE
    return pl.pallas_call(
        _conv_kernel,
        grid=(batch, nt),
        in_specs=[pl.BlockSpec((CONV_ROW_TILE, D_MODEL), lambda b, i: (b * nt + i, 0)),
                  _const_spec((1, D_MODEL)), _const_spec(w.shape), _const_spec(wdw.shape),
                  _const_spec((1, CONV_CH)), _const_spec((1, CONV_CH)), _const_spec((1, CONV_CH))],
        out_specs=pl.BlockSpec((CONV_ROW_TILE, CONV_CH), lambda b, i: (b * nt + i, 0)),
        out_shape=jax.ShapeDtypeStruct((T, CONV_CH), BF16),
        scratch_shapes=[pltpu.VMEM((CONV_ROW_TILE + CONV_HALO, CONV_CH), F32),
                        pltpu.VMEM((SUBLANES - 1, CONV_ROW_TILE + CONV_HALO - SUBLANES, CONV_CH), F32),
                        pltpu.VMEM((CONV_ROW_TILE, CONV_CH), F32)],
        compiler_params=_params("parallel", "arbitrary"),
        name="conv_branch",
    )(x, g, w, wdw, bdw, gln, bln)


def _combine_kernel(x_ref, g_ref, wg_ref, bg_ref, a_ref, b_ref, c_ref, wa_ref, wb_ref, wc_ref,
                    wo_ref, gp_ref, o_ref):
    x = x_ref[...]
    xn = _rms(x, g_ref[...]).astype(BF16)
    mix = None
    for n, (br_ref, wbr_ref) in enumerate(((a_ref, wa_ref), (b_ref, wb_ref), (c_ref, wc_ref))):
        cols = slice(n * D_MODEL, (n + 1) * D_MODEL)
        gate = jax.nn.sigmoid(_mm(xn, wg_ref[:, cols]) + bg_ref[:, cols])
        term = gate * _mm(br_ref[...], wbr_ref[...])
        mix = term if mix is None else mix + term
    y = _mm(mix.astype(BF16), wo_ref[...])
    o_ref[...] = x + _rms(y, gp_ref[...])


def _combine(x, g, wg, bg, a, b, c, wa, wb, wc, wo, gp):
    T = x.shape[0]
    row = lambda n: pl.BlockSpec((ROW_TILE, n), lambda i: (i, 0))
    return pl.pallas_call(
        _combine_kernel,
        grid=(T // ROW_TILE,),
        in_specs=[row(D_MODEL), _const_spec((1, D_MODEL)), _const_spec(wg.shape), _const_spec(bg.shape),
                  row(MOBA_WIDTH), row(GLA_VW), row(CONV_CH),
                  _const_spec(wa.shape), _const_spec(wb.shape), _const_spec(wc.shape),
                  _const_spec(wo.shape), _const_spec((1, D_MODEL))],
        out_specs=row(D_MODEL),
        out_shape=jax.ShapeDtypeStruct((T, D_MODEL), F32),
        compiler_params=_params("parallel"),
        name="combine",
    )(x, g, wg, bg, a, b, c, wa, wb, wc, wo, gp)


def _mlp_kernel(x_ref, g_ref, wu_ref, wd_ref, gp_ref, o_ref):
    x = x_ref[...]
    xn = _rms(x, g_ref[...]).astype(BF16)
    acc = None
    for c in range(D_FF // D_MODEL):
        cols = slice(c * D_MODEL, (c + 1) * D_MODEL)
        u = jnp.maximum(_mm(xn, wu_ref[:, cols]), 0.0)
        t = _mm((u * u).astype(BF16), wd_ref[cols, :])
        acc = t if acc is None else acc + t
    o_ref[...] = x + _rms(acc, gp_ref[...])


def _mlp(x, g, wu, wd, gp):
    T = x.shape[0]
    row = pl.BlockSpec((ROW_TILE, D_MODEL), lambda i: (i, 0))
    return pl.pallas_call(
        _mlp_kernel,
        grid=(T // ROW_TILE,),
        in_specs=[row, _const_spec((1, D_MODEL)), _const_spec(wu.shape), _const_spec(wd.shape),
                  _const_spec((1, D_MODEL))],
        out_specs=row,
        out_shape=jax.ShapeDtypeStruct((T, D_MODEL), F32),
        compiler_params=_params("parallel"),
        name="mlp",
    )(x, g, wu, wd, gp)


def kernel(x, positions, g_mix_pre, w_in, b_gate, w_gla_a2, b_gla_a, g_gla_norm, w_dw, b_dw, g_conv_ln,
           b_conv_ln, w_moba_o, w_gla_o, w_conv_o, w_mix_out, g_mix_post, g_mlp_pre, w_mlp_up, w_mlp_down,
           g_mlp_post):
    batch, seq, _ = x.shape
    depth = w_in.shape[0]
    assert seq % MOBA_BLOCK == 0 and seq % CONV_ROW_TILE == 0 and (batch * seq) % ROW_TILE == 0
    h = x.reshape(batch * seq, D_MODEL)
    cos, sina, sinb = _rope_tables(positions)

    o_gla = 3 * MOBA_WIDTH
    o_la = o_gla + 2 * GLA_KW + GLA_VW
    o_lr = o_la + GLA_RANK
    o_conv = o_lr + GLA_VW
    o_gate = o_conv + 2 * CONV_CH
    row = lambda v: v[None, :]

    for l in range(depth):
        w = w_in[l]
        g_pre = row(g_mix_pre[l])
        w_moba = w[:, :o_gla].astype(BF16)
        w_gla = jnp.concatenate([w[:, o_gla:o_la], w[:, o_lr:o_conv]], axis=1).astype(BF16)
        w_la = jnp.pad(w[:, o_la:o_lr], ((0, 0), (0, LANES - GLA_RANK))).astype(BF16)
        w_a2 = jnp.pad(w_gla_a2[l], ((0, LANES - GLA_RANK), (0, 0))).astype(BF16)
        w_conv = w[:, o_conv:o_gate].astype(BF16)
        w_gate = w[:, o_gate:].astype(BF16)

        mq, mk, mv = _moba_proj(h, g_pre, w_moba, cos, sina, sinb)
        oa = _moba_attn(mq, mk, mv, batch, seq)
        lq, lk, lv, lr, ld = _gla_proj(h, g_pre, w_gla, w_la, w_a2, row(b_gla_a[l]))
        ob = _gla_scan(lq, lk, lv, lr, ld, row(g_gla_norm[l]), batch, seq)
        oc = _conv_branch(h, g_pre, w_conv, w_dw[l], row(b_dw[l]), row(g_conv_ln[l]), row(b_conv_ln[l]),
                          batch, seq)
        h = _combine(h, g_pre, w_gate, row(b_gate[l]), oa, ob, oc,
                     w_moba_o[l].astype(BF16), w_gla_o[l].astype(BF16), w_conv_o[l].astype(BF16),
                     w_mix_out[l].astype(BF16), row(g_mix_post[l]))
        h = _mlp(h, row(g_mlp_pre[l]), w_mlp_up[l].astype(BF16), w_mlp_down[l].astype(BF16),
                 row(g_mlp_post[l]))
    return h.reshape(batch, seq, D_MODEL)
```

```python
import functools

import jax
import jax.numpy as jnp
from jax import lax
from jax.experimental import pallas as pl
from jax.experimental.pallas import tpu as pltpu

D_MODEL = 1024
MOBA_HEADS = 8
MOBA_HEAD_DIM = 64
MOBA_WIDTH = MOBA_HEADS * MOBA_HEAD_DIM
MOBA_BLOCK = 256
MOBA_TOPK = 3
ROPE_THETA = 10000.0
GLA_HEADS = 4
GLA_DK = 64
GLA_DV = 128
GLA_KW = GLA_HEADS * GLA_DK
GLA_VW = GLA_HEADS * GLA_DV
GLA_RANK = 16
GLA_TEMP = 16.0
GLA_CHUNK = 64
GLA_SUPER = 256
CONV_CH = 512
CONV_WIDTH = 31
D_FF = 4 * D_MODEL
N_BRANCH = 3
EPS = 1e-6
NEG = -1e30
LOG2E = 1.4426950408889634
MOBA_Q_SCALE = MOBA_HEAD_DIM ** -0.5 * LOG2E

LANES = 128
SUBLANES = 8
VMEM_LIMIT_BYTES = 56 * 1024 * 1024
MOBA_VT_ROWS = MOBA_HEAD_DIM + 2 * SUBLANES

ROW_TILE = 512
CONV_ROW_TILE = 256
CONV_HALO = 32
CONV_ROW_CHUNK = 64

PK_MOBA = 0
PK_GLA = 3 * MOBA_WIDTH
PK_CONV = PK_GLA + 2 * GLA_KW + 2 * GLA_VW
PK_GATE = PK_CONV + 2 * CONV_CH
PK_LA = PK_GATE + N_BRANCH * D_MODEL

F32 = jnp.float32
BF16 = jnp.bfloat16


def _nt(a, b):
    return lax.dot_general(a, b, (((1,), (1,)), ((), ())), preferred_element_type=F32)


def _tn(a, b):
    return lax.dot_general(a, b, (((0,), (0,)), ((), ())), preferred_element_type=F32)


def _mm(a, b):
    return jnp.dot(a, b, preferred_element_type=F32)


def _rms(x, g):
    return x * lax.rsqrt(jnp.mean(x * x, axis=-1, keepdims=True) + EPS) * g


def _split3(x):
    hi = x.astype(BF16)
    r1 = x - hi.astype(F32)
    mid = r1.astype(BF16)
    lo = (r1 - mid.astype(F32)).astype(BF16)
    return hi, mid, lo


def _params(*sem):
    return pltpu.CompilerParams(dimension_semantics=sem, vmem_limit_bytes=VMEM_LIMIT_BYTES)


def _const_spec(shape):
    nd = len(shape)
    return pl.BlockSpec(shape, lambda *_: (0,) * nd)


def _layer_spec(layer, rows, cols, col_block=0):
    return pl.BlockSpec((None, rows, cols), lambda *_: (layer, 0, col_block))


def _vec_spec(layer, n):
    return _layer_spec(layer, 1, n)


def _rope_table_kernel(pos_ref, invf_ref, cos_ref, sina_ref, sinb_ref):
    ang = pos_ref[...].astype(F32) * invf_ref[...]
    lane = lax.broadcasted_iota(jnp.int32, ang.shape, 1)
    first_half = (lane % MOBA_HEAD_DIM) < (MOBA_HEAD_DIM // 2)
    s = jnp.sin(ang)
    cos_ref[...] = jnp.cos(ang)
    sina_ref[...] = jnp.where(first_half, -s, 0.0)
    sinb_ref[...] = jnp.where(first_half, 0.0, s)


def _rope_tables(positions):
    T = positions.size
    half = MOBA_HEAD_DIM // 2
    inv_freq = ROPE_THETA ** (-jnp.arange(half, dtype=F32) / half)
    invf = jnp.tile(inv_freq, LANES // half)[None, :]
    tab = jax.ShapeDtypeStruct((T, LANES), F32)
    row = pl.BlockSpec((ROW_TILE, LANES), lambda i: (i, 0))
    return pl.pallas_call(
        _rope_table_kernel,
        grid=(T // ROW_TILE,),
        in_specs=[pl.BlockSpec((ROW_TILE, 1), lambda i: (i, 0)), _const_spec((1, LANES))],
        out_specs=[row, row, row],
        out_shape=[tab, tab, tab],
        compiler_params=_params("parallel"),
        name="rope_tables",
    )(positions.reshape(T, 1), invf)


def _moba_proj_kernel(x_ref, g_ref, w_ref, cos_ref, sina_ref, sinb_ref, q_ref, k_ref, v_ref):
    xn = _rms(x_ref[...], g_ref[...]).astype(BF16)
    z = _mm(xn, w_ref[...])
    cos, sa, sb = cos_ref[...], sina_ref[...], sinb_ref[...]
    for out_ref, base, scale in ((q_ref, 0, MOBA_Q_SCALE), (k_ref, MOBA_WIDTH, None)):
        for c in range(MOBA_WIDTH // LANES):
            s = z[:, base + c * LANES: base + (c + 1) * LANES]
            r = s * cos + pltpu.roll(s, LANES - 32, 1) * sa + pltpu.roll(s, 32, 1) * sb
            if scale is not None:
                r = r * scale
            out_ref[:, c * LANES:(c + 1) * LANES] = r.astype(BF16)
    v_ref[...] = z[:, 2 * MOBA_WIDTH:].astype(BF16)


def _moba_proj(layer, x, g, wp, cos, sina, sinb):
    T = x.shape[0]
    row = lambda n: pl.BlockSpec((ROW_TILE, n), lambda i: (i, 0))
    o = jax.ShapeDtypeStruct((T, MOBA_WIDTH), BF16)
    width = 3 * MOBA_WIDTH
    return pl.pallas_call(
        _moba_proj_kernel,
        grid=(T // ROW_TILE,),
        in_specs=[row(D_MODEL), _vec_spec(layer, D_MODEL), _layer_spec(layer, D_MODEL, width, PK_MOBA // width),
                  row(LANES), row(LANES), row(LANES)],
        out_specs=[row(MOBA_WIDTH)] * 3,
        out_shape=[o, o, o],
        compiler_params=_params("parallel"),
        name="moba_proj",
    )(x, g, wp, cos, sina, sinb)


def _moba_attn_kernel(q_ref, k_ref, v_ref, o_ref, kmean_ref, vt_ref, s_ref, *, n_blocks):
    blk = MOBA_BLOCK
    pair = LANES // MOBA_HEAD_DIM
    hd = MOBA_HEAD_DIM
    kmean_ref[...] = jnp.zeros_like(kmean_ref)
    ones_row = lax.broadcasted_iota(jnp.int32, (MOBA_VT_ROWS - hd, blk), 0) == 0
    ones_pad = jnp.where(ones_row, 1.0, 0.0).astype(BF16)
    for j in range(n_blocks):
        kj = k_ref[j * blk:(j + 1) * blk, :].astype(F32)
        kmean_ref[j:j + 1, :] = jnp.mean(kj, axis=0, keepdims=True)
        vt = v_ref[j * blk:(j + 1) * blk, :].astype(F32).T.astype(BF16)
        for h in range(pair):
            vt_ref[j, h, 0:hd, :] = vt[h * hd:(h + 1) * hd, :]
            vt_ref[j, h, hd:, :] = ones_pad

    kpos = lax.broadcasted_iota(jnp.int32, (blk, blk), 0)
    qpos = lax.broadcasted_iota(jnp.int32, (blk, blk), 1)
    causal = kpos <= qpos
    qlane = lax.broadcasted_iota(jnp.int32, (blk, LANES), 1) // hd
    sub = lambda t: t.reshape(blk // SUBLANES, SUBLANES, blk)
    col_max = {}

    def scores(ii):
        q = q_ref[ii * blk:(ii + 1) * blk, :]
        need_gate = ii > MOBA_TOPK
        if need_gate:
            km_hi, km_mid, km_lo = _split3(kmean_ref[...])
            jidx = lax.broadcasted_iota(jnp.int32, (kmean_ref.shape[0], blk), 0)
        for h in range(pair):
            qh = jnp.where(qlane == h, q, jnp.zeros_like(q))
            valid = [None] * ii
            if need_gate:
                gate = _nt(km_hi, qh) + _nt(km_mid, qh) + _nt(km_lo, qh)
                for n in range(ii):
                    gn = gate[n:n + 1, :]
                    beats = ((gate > gn) | ((gate == gn) & (jidx < n))) & (jidx < ii)
                    rank = jnp.sum(jnp.where(beats, 1.0, 0.0), axis=0, keepdims=True)
                    valid[n] = rank < MOBA_TOPK
            m8 = None
            for j in range(ii + 1):
                s = _nt(k_ref[j * blk:(j + 1) * blk, :], qh)
                if j == ii:
                    s = jnp.where(causal, s, NEG)
                elif valid[j] is not None:
                    s = jnp.where(valid[j], s, NEG)
                s_ref[ii % 2, h, j] = s
                t = jnp.max(sub(s), axis=0)
                m8 = t if m8 is None else jnp.maximum(m8, t)
                yield
            col_max[ii, h] = jnp.max(m8, axis=0, keepdims=True)

    def values(ii):
        outs = []
        for h in range(pair):
            acc = None
            for j in range(ii + 1):
                p = jnp.exp2(s_ref[ii % 2, h, j] - col_max[ii, h])
                t = _mm(vt_ref[j, h], p.astype(BF16))
                acc = t if acc is None else acc + t
                yield
            outs.append(acc[0:hd, :] / acc[hd:hd + 1, :])
        o_ref[ii * blk:(ii + 1) * blk, :] = jnp.concatenate(outs, axis=0).T.astype(BF16)

    for _ in scores(0):
        pass
    for ii in range(n_blocks):
        live = [values(ii)]
        if ii + 1 < n_blocks:
            live.insert(0, scores(ii + 1))
        while live:
            for gen in list(live):
                if next(gen, StopIteration) is StopIteration:
                    live.remove(gen)


def _moba_attn(q, k, v, batch, seq):
    n_blocks = seq // MOBA_BLOCK
    n_pairs = MOBA_WIDTH // LANES
    pair = LANES // MOBA_HEAD_DIM
    spec = pl.BlockSpec((seq, LANES), lambda b, p: (b, p))
    gate_rows = 2 * SUBLANES
    assert n_blocks <= gate_rows
    return pl.pallas_call(
        functools.partial(_moba_attn_kernel, n_blocks=n_blocks),
        grid=(batch, n_pairs),
        in_specs=[spec, spec, spec],
        out_specs=spec,
        out_shape=jax.ShapeDtypeStruct(q.shape, BF16),
        scratch_shapes=[
            pltpu.VMEM((gate_rows, LANES), F32),
            pltpu.VMEM((n_blocks, pair, MOBA_VT_ROWS, MOBA_BLOCK), BF16),
            pltpu.VMEM((2, pair, n_blocks, MOBA_BLOCK, MOBA_BLOCK), F32),
        ],
        compiler_params=_params("parallel", "parallel"),
        name="moba_attn",
    )(q, k, v)


def _gla_proj_kernel(x_ref, g_ref, w_ref, wla_ref, wa2_ref, ba_ref, q_ref, k_ref, v_ref, r_ref, d_ref):
    xn = _rms(x_ref[...], g_ref[...]).astype(BF16)
    z = _mm(xn, w_ref[...])
    q_ref[...] = z[:, :GLA_KW].astype(BF16)
    k_ref[...] = z[:, GLA_KW:2 * GLA_KW].astype(BF16)
    v_ref[...] = z[:, 2 * GLA_KW:2 * GLA_KW + GLA_VW].astype(BF16)
    r_ref[...] = z[:, 2 * GLA_KW + GLA_VW:].astype(BF16)
    la = _mm(xn, wla_ref[...])
    pre = _mm(la.astype(BF16), wa2_ref[...]) + ba_ref[...]
    log_sig = jnp.minimum(pre, 0.0) - jnp.log1p(jnp.exp(-jnp.abs(pre)))
    d_ref[...] = log_sig / GLA_TEMP


def _gla_proj(layer, x, g, wp, wa2, ba):
    T = x.shape[0]
    row = lambda n: pl.BlockSpec((ROW_TILE, n), lambda i: (i, 0))
    sds = lambda n, dt: jax.ShapeDtypeStruct((T, n), dt)
    width = 2 * GLA_KW + 2 * GLA_VW
    return pl.pallas_call(
        _gla_proj_kernel,
        grid=(T // ROW_TILE,),
        in_specs=[row(D_MODEL), _vec_spec(layer, D_MODEL), _layer_spec(layer, D_MODEL, width, PK_GLA // width),
                  _layer_spec(layer, D_MODEL, LANES, PK_LA // LANES), _layer_spec(layer, LANES, GLA_KW),
                  _vec_spec(layer, GLA_KW)],
        out_specs=[row(GLA_KW), row(GLA_KW), row(GLA_VW), row(GLA_VW), row(GLA_KW)],
        out_shape=[sds(GLA_KW, BF16), sds(GLA_KW, BF16), sds(GLA_VW, BF16), sds(GLA_VW, BF16),
                   sds(GLA_KW, F32)],
        compiler_params=_params("parallel"),
        name="gla_proj",
    )(x, g, wp, wp, wa2, ba)


def _gla_scan_kernel(q_ref, k_ref, v_ref, r_ref, d_ref, gn_ref, o_ref, *, seq):
    ch, sc = GLA_CHUNK, GLA_SUPER
    nc = sc // ch
    pair = LANES // GLA_DK
    ri = lax.broadcasted_iota(jnp.int32, (sc, sc), 0)
    ci = lax.broadcasted_iota(jnp.int32, (sc, sc), 1)
    lower = (ri >= ci) & (ri // ch == ci // ch)
    tril = jnp.where(lower, 1.0, 0.0).astype(BF16)
    lane_head = lax.broadcasted_iota(jnp.int32, (sc, LANES), 1) // GLA_DK
    st_lane_head = lax.broadcasted_iota(jnp.int32, (GLA_DV, LANES), 1) // GLA_DK
    gn = gn_ref[...]

    def body(t, st):
        rows = pl.ds(pl.multiple_of(t * sc, sc), sc)
        d_hi, d_mid, d_lo = _split3(d_ref[rows, :])
        G = _mm(tril, d_hi) + _mm(tril, d_mid) + _mm(tril, d_lo)
        g_last = [G[(c + 1) * ch - 1:(c + 1) * ch, :] for c in range(nc)]
        g_last_rows = jnp.concatenate([jnp.broadcast_to(g, (ch, LANES)) for g in g_last], axis=0)
        q = q_ref[rows, :].astype(F32)
        k = k_ref[rows, :].astype(F32)
        qt = (q * jnp.exp(G) * (GLA_DK ** -0.5)).astype(BF16)
        kt = (k * jnp.exp(-G)).astype(BF16)
        kd = (k * jnp.exp(g_last_rows - G)).astype(BF16)
        qh, vh, o_intra, upd = [], [], [], []
        for h in range(pair):
            cols = slice(h * GLA_DV, (h + 1) * GLA_DV)
            qh.append(jnp.where(lane_head == h, qt, jnp.zeros_like(qt)))
            a = jnp.where(lower, _nt(qh[h], kt), 0.0).astype(BF16)
            vh.append(v_ref[rows, cols])
            o_intra.append(_mm(a, vh[h]))
            upd.append([_tn(vh[h][c * ch:(c + 1) * ch], kd[c * ch:(c + 1) * ch]) for c in range(nc)])
        o_inter = [[] for _ in range(pair)]
        for c in range(nc):
            st_b = st.astype(BF16)
            u = upd[0][c]
            for h in range(pair):
                o_inter[h].append(_nt(qh[h][c * ch:(c + 1) * ch], st_b))
                if h:
                    u = jnp.where(st_lane_head == h, upd[h][c], u)
            st = st * jnp.exp(g_last[c]) + u
        for h in range(pair):
            cols = slice(h * GLA_DV, (h + 1) * GLA_DV)
            o = _rms(o_intra[h] + jnp.concatenate(o_inter[h], axis=0), gn)
            rh = r_ref[rows, cols].astype(F32)
            o_ref[rows, cols] = (o * (rh * jax.nn.sigmoid(rh))).astype(BF16)
        return st

    lax.fori_loop(0, seq // sc, body, jnp.zeros((GLA_DV, LANES), F32))


def _gla_scan(layer, q, k, v, r, d, gn, batch, seq):
    n_pairs = GLA_KW // LANES
    pair = LANES // GLA_DK
    kspec = pl.BlockSpec((seq, LANES), lambda b, p: (b, p))
    vspec = pl.BlockSpec((seq, pair * GLA_DV), lambda b, p: (b, p))
    return pl.pallas_call(
        functools.partial(_gla_scan_kernel, seq=seq),
        grid=(batch, n_pairs),
        in_specs=[kspec, kspec, vspec, vspec, kspec, _vec_spec(layer, GLA_DV)],
        out_specs=vspec,
        out_shape=jax.ShapeDtypeStruct(v.shape, BF16),
        compiler_params=_params("parallel", "parallel"),
        name="gla_scan",
    )(q, k, v, r, d, gn)


def _conv_kernel(x_ref, g_ref, w_ref, wdw_ref, bdw_ref, gln_ref, bln_ref, o_ref, u_ref, us_ref, y_ref):
    ts, halo = CONV_ROW_TILE, CONV_HALO
    i = pl.program_id(1)

    @pl.when(i == 0)
    def _():
        u_ref[0:halo, :] = jnp.zeros((halo, CONV_CH), F32)

    @pl.when(i > 0)
    def _():
        u_ref[0:halo, :] = u_ref[ts:ts + halo, :]

    xn = _rms(x_ref[...], g_ref[...]).astype(BF16)
    z = _mm(xn, w_ref[...])
    u_ref[halo:halo + ts, :] = z[:, :CONV_CH] * jax.nn.sigmoid(z[:, CONV_CH:])
    first = halo - (CONV_WIDTH - 1)
    span = us_ref.shape[1]
    for b in range(1, SUBLANES):
        us_ref[b - 1] = u_ref[b:b + span, :]
    for c in range(CONV_CH // LANES):
        cols = slice(c * LANES, (c + 1) * LANES)
        for r in range(0, ts, CONV_ROW_CHUNK):
            acc = jnp.broadcast_to(bdw_ref[:, cols], (CONV_ROW_CHUNK, LANES))
            for w in range(CONV_WIDTH):
                a, b = divmod(first + w, SUBLANES)
                lo = r + a * SUBLANES
                tap = u_ref[lo:lo + CONV_ROW_CHUNK, cols] if b == 0 else us_ref[b - 1, lo:lo + CONV_ROW_CHUNK, cols]
                acc = acc + wdw_ref[w:w + 1, cols] * tap
            y_ref[r:r + CONV_ROW_CHUNK, cols] = acc
    y = y_ref[...]
    mu = jnp.mean(y, axis=-1, keepdims=True)
    yc = y - mu
    var = jnp.mean(yc * yc, axis=-1, keepdims=True)
    yn = yc * lax.rsqrt(var + EPS) * gln_ref[...] + bln_ref[...]
    o_ref[...] = (yn * jax.nn.sigmoid(yn)).astype(BF16)


def _conv_branch(layer, x, g, wp, wdw, bdw, gln, bln, batch, seq):
    T = x.shape[0]
    nt = seq // CONV_ROW_TILE
    width = 2 * CONV_CH
    return pl.pallas_call(
        _conv_kernel,
        grid=(batch, nt),
        in_specs=[pl.BlockSpec((CONV_ROW_TILE, D_MODEL), lambda b, i: (b * nt + i, 0)),
                  _vec_spec(layer, D_MODEL), _layer_spec(layer, D_MODEL, width, PK_CONV // width),
                  _layer_spec(layer, CONV_WIDTH, CONV_CH),
                  _vec_spec(layer, CONV_CH), _vec_spec(layer, CONV_CH), _vec_spec(layer, CONV_CH)],
        out_specs=pl.BlockSpec((CONV_ROW_TILE, CONV_CH), lambda b, i: (b * nt + i, 0)),
        out_shape=jax.ShapeDtypeStruct((T, CONV_CH), BF16),
        scratch_shapes=[pltpu.VMEM((CONV_ROW_TILE + CONV_HALO, CONV_CH), F32),
                        pltpu.VMEM((SUBLANES - 1, CONV_ROW_TILE + CONV_HALO - SUBLANES, CONV_CH), F32),
                        pltpu.VMEM((CONV_ROW_TILE, CONV_CH), F32)],
        compiler_params=_params("parallel", "arbitrary"),
        name="conv_branch",
    )(x, g, wp, wdw, bdw, gln, bln)


def _combine_kernel(x_ref, g_ref, wga_ref, wgb_ref, wgc_ref, bg_ref, a_ref, b_ref, c_ref, wa_ref, wb_ref,
                    wc_ref, wo_ref, gp_ref, o_ref):
    x = x_ref[...]
    xn = _rms(x, g_ref[...]).astype(BF16)
    mix = None
    branches = ((wga_ref, a_ref, wa_ref), (wgb_ref, b_ref, wb_ref), (wgc_ref, c_ref, wc_ref))
    for n, (wg_ref, br_ref, wbr_ref) in enumerate(branches):
        cols = slice(n * D_MODEL, (n + 1) * D_MODEL)
        gate = jax.nn.sigmoid(_mm(xn, wg_ref[...]) + bg_ref[:, cols])
        term = gate * _mm(br_ref[...], wbr_ref[...])
        mix = term if mix is None else mix + term
    y = _mm(mix.astype(BF16), wo_ref[...])
    o_ref[...] = x + _rms(y, gp_ref[...])


def _combine(layer, x, g, wp, bg, a, b, c, wa, wb, wc, wo, gp):
    T = x.shape[0]
    row = lambda n: pl.BlockSpec((ROW_TILE, n), lambda i: (i, 0))
    gate_w = lambda n: _layer_spec(layer, D_MODEL, D_MODEL, PK_GATE // D_MODEL + n)
    return pl.pallas_call(
        _combine_kernel,
        grid=(T // ROW_TILE,),
        in_specs=[row(D_MODEL), _vec_spec(layer, D_MODEL), gate_w(0), gate_w(1), gate_w(2),
                  _vec_spec(layer, N_BRANCH * D_MODEL),
                  row(MOBA_WIDTH), row(GLA_VW), row(CONV_CH),
                  _layer_spec(layer, MOBA_WIDTH, D_MODEL), _layer_spec(layer, GLA_VW, D_MODEL),
                  _layer_spec(layer, CONV_CH, D_MODEL), _layer_spec(layer, D_MODEL, D_MODEL),
                  _vec_spec(layer, D_MODEL)],
        out_specs=row(D_MODEL),
        out_shape=jax.ShapeDtypeStruct((T, D_MODEL), F32),
        compiler_params=_params("parallel"),
        name="combine",
    )(x, g, wp, wp, wp, bg, a, b, c, wa, wb, wc, wo, gp)


def _mlp_kernel(x_ref, g_ref, wu_ref, wd_ref, gp_ref, o_ref):
    x = x_ref[...]
    xn = _rms(x, g_ref[...]).astype(BF16)
    acc = None
    for c in range(D_FF // D_MODEL):
        cols = slice(c * D_MODEL, (c + 1) * D_MODEL)
        u = jnp.maximum(_mm(xn, wu_ref[:, cols]), 0.0)
        t = _mm((u * u).astype(BF16), wd_ref[cols, :])
        acc = t if acc is None else acc + t
    o_ref[...] = x + _rms(acc, gp_ref[...])


def _mlp(layer, x, g, wu, wd, gp):
    T = x.shape[0]
    row = pl.BlockSpec((ROW_TILE, D_MODEL), lambda i: (i, 0))
    return pl.pallas_call(
        _mlp_kernel,
        grid=(T // ROW_TILE,),
        in_specs=[row, _vec_spec(layer, D_MODEL), _layer_spec(layer, D_MODEL, D_FF),
                  _layer_spec(layer, D_FF, D_MODEL), _vec_spec(layer, D_MODEL)],
        out_specs=row,
        out_shape=jax.ShapeDtypeStruct((T, D_MODEL), F32),
        compiler_params=_params("parallel"),
        name="mlp",
    )(x, g, wu, wd, gp)


def kernel(x, positions, g_mix_pre, w_in, b_gate, w_gla_a2, b_gla_a, g_gla_norm, w_dw, b_dw, g_conv_ln,
           b_conv_ln, w_moba_o, w_gla_o, w_conv_o, w_mix_out, g_mix_post, g_mlp_pre, w_mlp_up, w_mlp_down,
           g_mlp_post):
    batch, seq, _ = x.shape
    depth = w_in.shape[0]
    assert seq % MOBA_BLOCK == 0 and seq % CONV_ROW_TILE == 0 and (batch * seq) % ROW_TILE == 0
    assert seq % GLA_SUPER == 0
    h = x.reshape(batch * seq, D_MODEL)
    cos, sina, sinb = _rope_tables(positions)

    o_la = 3 * MOBA_WIDTH + 2 * GLA_KW + GLA_VW
    o_lr = o_la + GLA_RANK
    pad_la = jnp.pad(w_in[:, :, o_la:o_lr], ((0, 0), (0, 0), (0, LANES - GLA_RANK)))
    wp = jnp.concatenate([w_in[:, :, :o_la], w_in[:, :, o_lr:], pad_la], axis=-1).astype(BF16)
    assert wp.shape[-1] == PK_LA + LANES
    wa2 = jnp.pad(w_gla_a2, ((0, 0), (0, LANES - GLA_RANK), (0, 0))).astype(BF16)
    vec = lambda v: v[:, None, :]
    g_pre, g_post, g_mpre, g_mpost = vec(g_mix_pre), vec(g_mix_post), vec(g_mlp_pre), vec(g_mlp_post)
    bg, ba, gn = vec(b_gate), vec(b_gla_a), vec(g_gla_norm)
    bdw, gln, bln = vec(b_dw), vec(g_conv_ln), vec(b_conv_ln)
    wa, wb, wc, wo = (w.astype(BF16) for w in (w_moba_o, w_gla_o, w_conv_o, w_mix_out))
    wu, wd = w_mlp_up.astype(BF16), w_mlp_down.astype(BF16)

    for l in range(depth):
        mq, mk, mv = _moba_proj(l, h, g_pre, wp, cos, sina, sinb)
        oa = _moba_attn(mq, mk, mv, batch, seq)
        lq, lk, lv, lr, ld = _gla_proj(l, h, g_pre, wp, wa2, ba)
        ob = _gla_scan(l, lq, lk, lv, lr, ld, gn, batch, seq)
        oc = _conv_branch(l, h, g_pre, wp, w_dw, bdw, gln, bln, batch, seq)
        h = _combine(l, h, g_pre, wp, bg, oa, ob, oc, wa, wb, wc, wo, g_post)
        h = _mlp(l, h, g_mpre, wu, wd, g_mpost)
    return h.reshape(batch, seq, D_MODEL)
```

```python
import functools

import jax
import jax.numpy as jnp
from jax import lax
from jax.experimental import pallas as pl
from jax.experimental.pallas import tpu as pltpu

D_MODEL = 1024
MOBA_HEADS = 8
MOBA_HEAD_DIM = 64
MOBA_WIDTH = MOBA_HEADS * MOBA_HEAD_DIM
MOBA_BLOCK = 256
MOBA_TOPK = 3
ROPE_THETA = 10000.0
GLA_HEADS = 4
GLA_DK = 64
GLA_DV = 128
GLA_KW = GLA_HEADS * GLA_DK
GLA_VW = GLA_HEADS * GLA_DV
GLA_RANK = 16
GLA_TEMP = 16.0
GLA_CHUNK = 64
CONV_CH = 512
CONV_WIDTH = 31
D_FF = 4 * D_MODEL
N_BRANCH = 3
EPS = 1e-6
NEG = -1e30
LOG2E = 1.4426950408889634
MOBA_Q_SCALE = MOBA_HEAD_DIM ** -0.5 * LOG2E

LANES = 128
SUBLANES = 8
VMEM_LIMIT_BYTES = 56 * 1024 * 1024
MOBA_VT_ROWS = MOBA_HEAD_DIM + 2 * SUBLANES

ROW_TILE = 512
FRONT_TILE = 256
CONV_HALO = 32
CONV_ROW_CHUNK = 64

PK_MOBA = 0
PK_GLA = 3 * MOBA_WIDTH
PK_CONV = PK_GLA + 2 * GLA_KW + 2 * GLA_VW
PK_GATE = PK_CONV + 2 * CONV_CH
PK_LA = PK_GATE + N_BRANCH * D_MODEL

F32 = jnp.float32
BF16 = jnp.bfloat16


def _nt(a, b):
    return lax.dot_general(a, b, (((1,), (1,)), ((), ())), preferred_element_type=F32)


def _tn(a, b):
    return lax.dot_general(a, b, (((0,), (0,)), ((), ())), preferred_element_type=F32)


def _mm(a, b):
    return jnp.dot(a, b, preferred_element_type=F32)


def _rms(x, g):
    return x * lax.rsqrt(jnp.mean(x * x, axis=-1, keepdims=True) + EPS) * g


def _split3(x):
    hi = x.astype(BF16)
    r1 = x - hi.astype(F32)
    mid = r1.astype(BF16)
    lo = (r1 - mid.astype(F32)).astype(BF16)
    return hi, mid, lo


def _params(*sem):
    return pltpu.CompilerParams(dimension_semantics=sem, vmem_limit_bytes=VMEM_LIMIT_BYTES)


def _const_spec(shape):
    nd = len(shape)
    return pl.BlockSpec(shape, lambda *_: (0,) * nd)


def _layer_spec(layer, rows, cols, col_block=0):
    return pl.BlockSpec((None, rows, cols), lambda *_: (layer, 0, col_block))


def _vec_spec(layer, n):
    return _layer_spec(layer, 1, n)


def _rope_table_kernel(pos_ref, invf_ref, cos_ref, sina_ref, sinb_ref):
    ang = pos_ref[...].astype(F32) * invf_ref[...]
    lane = lax.broadcasted_iota(jnp.int32, ang.shape, 1)
    first_half = (lane % MOBA_HEAD_DIM) < (MOBA_HEAD_DIM // 2)
    s = jnp.sin(ang)
    cos_ref[...] = jnp.cos(ang)
    sina_ref[...] = jnp.where(first_half, -s, 0.0)
    sinb_ref[...] = jnp.where(first_half, 0.0, s)


def _rope_tables(positions):
    T = positions.size
    half = MOBA_HEAD_DIM // 2
    inv_freq = ROPE_THETA ** (-jnp.arange(half, dtype=F32) / half)
    invf = jnp.tile(inv_freq, LANES // half)[None, :]
    tab = jax.ShapeDtypeStruct((T, LANES), F32)
    row = pl.BlockSpec((ROW_TILE, LANES), lambda i: (i, 0))
    return pl.pallas_call(
        _rope_table_kernel,
        grid=(T // ROW_TILE,),
        in_specs=[pl.BlockSpec((ROW_TILE, 1), lambda i: (i, 0)), _const_spec((1, LANES))],
        out_specs=[row, row, row],
        out_shape=[tab, tab, tab],
        compiler_params=_params("parallel"),
        name="rope_tables",
    )(positions.reshape(T, 1), invf)


def _rope(s, cos, sa, sb):
    return s * cos + pltpu.roll(s, LANES - 32, 1) * sa + pltpu.roll(s, 32, 1) * sb


def _gla_masks():
    ch, sc = GLA_CHUNK, FRONT_TILE
    ri = lax.broadcasted_iota(jnp.int32, (sc, sc), 0)
    ci = lax.broadcasted_iota(jnp.int32, (sc, sc), 1)
    lower = (ri >= ci) & (ri // ch == ci // ch)
    tril = jnp.where(lower, 1.0, 0.0).astype(BF16)
    lane_head = lax.broadcasted_iota(jnp.int32, (sc, LANES), 1) // GLA_DK
    st_lane_head = lax.broadcasted_iota(jnp.int32, (GLA_DV, LANES), 1) // GLA_DK
    return lower, tril, lane_head, st_lane_head


def _gla_tile(q, k, v, r, d, gn, st, masks):
    ch, sc = GLA_CHUNK, FRONT_TILE
    nc = sc // ch
    pair = LANES // GLA_DK
    lower, tril, lane_head, st_lane_head = masks
    d_hi, d_mid, d_lo = _split3(d)
    G = _mm(tril, d_hi) + _mm(tril, d_mid) + _mm(tril, d_lo)
    yield
    g_last = [G[(c + 1) * ch - 1:(c + 1) * ch, :] for c in range(nc)]
    g_last_rows = jnp.concatenate([jnp.broadcast_to(g, (ch, LANES)) for g in g_last], axis=0)
    qt = (q * jnp.exp(G) * (GLA_DK ** -0.5)).astype(BF16)
    kt = (k * jnp.exp(-G)).astype(BF16)
    kd = (k * jnp.exp(g_last_rows - G)).astype(BF16)
    yield
    qh, vh, o_intra, upd = [], [], [], []
    for h in range(pair):
        qh.append(jnp.where(lane_head == h, qt, jnp.zeros_like(qt)))
        a = jnp.where(lower, _nt(qh[h], kt), 0.0).astype(BF16)
        vh.append(v[:, h * GLA_DV:(h + 1) * GLA_DV])
        o_intra.append(_mm(a, vh[h]))
        upd.append([_tn(vh[h][c * ch:(c + 1) * ch], kd[c * ch:(c + 1) * ch]) for c in range(nc)])
        yield
    o_inter = [[] for _ in range(pair)]
    for c in range(nc):
        st_b = st.astype(BF16)
        u = upd[0][c]
        for h in range(pair):
            o_inter[h].append(_nt(qh[h][c * ch:(c + 1) * ch], st_b))
            if h:
                u = jnp.where(st_lane_head == h, upd[h][c], u)
        st = st * jnp.exp(g_last[c]) + u
    yield
    outs = []
    for h in range(pair):
        o = _rms(o_intra[h] + jnp.concatenate(o_inter[h], axis=0), gn)
        rh = r[:, h * GLA_DV:(h + 1) * GLA_DV]
        outs.append(o * (rh * jax.nn.sigmoid(rh)))
        yield
    return outs, st


def _drive(*gens):
    results = [None] * len(gens)
    live = list(enumerate(gens))
    while live:
        for item in list(live):
            n, gen = item
            try:
                next(gen)
            except StopIteration as stop:
                results[n] = stop.value
                live.remove(item)
    return results


def _front_kernel(x_ref, g_ref, w_ref, wla_ref, wa2_ref, ba_ref, cos_ref, sina_ref, sinb_ref,
                  wdw_ref, bdw_ref, gln_ref, bln_ref, gn_ref,
                  mq_ref, mk_ref, mv_ref, ob_ref, oc_ref, u_ref, us_ref, y_ref, st_ref):
    ts, halo = FRONT_TILE, CONV_HALO
    i = pl.program_id(1)

    @pl.when(i == 0)
    def _():
        u_ref[0:halo, :] = jnp.zeros((halo, CONV_CH), F32)
        st_ref[...] = jnp.zeros_like(st_ref)

    @pl.when(i > 0)
    def _():
        u_ref[0:halo, :] = u_ref[ts:ts + halo, :]

    xn = _rms(x_ref[...], g_ref[...]).astype(BF16)

    zc = _mm(xn, w_ref[:, PK_CONV:PK_CONV + 2 * CONV_CH])
    u_ref[halo:halo + ts, :] = zc[:, :CONV_CH] * jax.nn.sigmoid(zc[:, CONV_CH:])
    span = us_ref.shape[1]
    for b in range(1, SUBLANES):
        us_ref[b - 1] = u_ref[b:b + span, :]

    zm = _mm(xn, w_ref[:, PK_MOBA:PK_MOBA + 3 * MOBA_WIDTH])
    cos, sa, sb = cos_ref[...], sina_ref[...], sinb_ref[...]
    for c in range(MOBA_WIDTH // LANES):
        cols = slice(c * LANES, (c + 1) * LANES)
        mq_ref[:, cols] = (_rope(zm[:, cols], cos, sa, sb) * MOBA_Q_SCALE).astype(BF16)
        mk_ref[:, cols] = _rope(zm[:, MOBA_WIDTH + c * LANES:MOBA_WIDTH + (c + 1) * LANES], cos, sa, sb).astype(BF16)
    mv_ref[...] = zm[:, 2 * MOBA_WIDTH:].astype(BF16)

    zg = _mm(xn, w_ref[:, PK_GLA:PK_GLA + 2 * GLA_KW + 2 * GLA_VW])
    la = _mm(xn, wla_ref[...])
    pre = _mm(la.astype(BF16), wa2_ref[...]) + ba_ref[...]
    dec = (jnp.minimum(pre, 0.0) - jnp.log1p(jnp.exp(-jnp.abs(pre)))) / GLA_TEMP

    def conv():
        first = halo - (CONV_WIDTH - 1)
        for c in range(CONV_CH // LANES):
            cols = slice(c * LANES, (c + 1) * LANES)
            for r in range(0, ts, CONV_ROW_CHUNK):
                acc = jnp.broadcast_to(bdw_ref[:, cols], (CONV_ROW_CHUNK, LANES))
                for w in range(CONV_WIDTH):
                    a, b = divmod(first + w, SUBLANES)
                    lo = r + a * SUBLANES
                    tap = (u_ref[lo:lo + CONV_ROW_CHUNK, cols] if b == 0
                           else us_ref[b - 1, lo:lo + CONV_ROW_CHUNK, cols])
                    acc = acc + wdw_ref[w:w + 1, cols] * tap
                y_ref[r:r + CONV_ROW_CHUNK, cols] = acc
                yield

    masks = _gla_masks()
    gn = gn_ref[...]
    pair_dv = (LANES // GLA_DK) * GLA_DV
    glas = []
    for p in range(GLA_KW // LANES):
        kl = slice(p * LANES, (p + 1) * LANES)
        vl = slice(2 * GLA_KW + p * pair_dv, 2 * GLA_KW + (p + 1) * pair_dv)
        rl = slice(2 * GLA_KW + GLA_VW + p * pair_dv, 2 * GLA_KW + GLA_VW + (p + 1) * pair_dv)
        glas.append(_gla_tile(zg[:, kl], zg[:, GLA_KW + p * LANES:GLA_KW + (p + 1) * LANES],
                              zg[:, vl].astype(BF16), zg[:, rl], dec[:, kl], gn, st_ref[p], masks))
    res = _drive(*glas, conv())
    for p in range(GLA_KW // LANES):
        outs, st = res[p]
        st_ref[p] = st
        for h, o in enumerate(outs):
            ob_ref[:, p * pair_dv + h * GLA_DV:p * pair_dv + (h + 1) * GLA_DV] = o.astype(BF16)

    y = y_ref[...]
    mu = jnp.mean(y, axis=-1, keepdims=True)
    yc = y - mu
    var = jnp.mean(yc * yc, axis=-1, keepdims=True)
    yn = yc * lax.rsqrt(var + EPS) * gln_ref[...] + bln_ref[...]
    oc_ref[...] = (yn * jax.nn.sigmoid(yn)).astype(BF16)


def _front(layer, x, g, wp, wa2, ba, cos, sina, sinb, wdw, bdw, gln, bln, gn, batch, seq):
    T = x.shape[0]
    nt = seq // FRONT_TILE
    row = lambda n: pl.BlockSpec((FRONT_TILE, n), lambda b, i: (b * nt + i, 0))
    sds = lambda n: jax.ShapeDtypeStruct((T, n), BF16)
    assert PK_MOBA == 0 and PK_GATE % LANES == 0
    return pl.pallas_call(
        _front_kernel,
        grid=(batch, nt),
        in_specs=[row(D_MODEL), _vec_spec(layer, D_MODEL), _layer_spec(layer, D_MODEL, PK_GATE, 0),
                  _layer_spec(layer, D_MODEL, LANES, PK_LA // LANES), _layer_spec(layer, LANES, GLA_KW),
                  _vec_spec(layer, GLA_KW), row(LANES), row(LANES), row(LANES),
                  _layer_spec(layer, CONV_WIDTH, CONV_CH),
                  _vec_spec(layer, CONV_CH), _vec_spec(layer, CONV_CH), _vec_spec(layer, CONV_CH),
                  _vec_spec(layer, GLA_DV)],
        out_specs=[row(MOBA_WIDTH), row(MOBA_WIDTH), row(MOBA_WIDTH), row(GLA_VW), row(CONV_CH)],
        out_shape=[sds(MOBA_WIDTH), sds(MOBA_WIDTH), sds(MOBA_WIDTH), sds(GLA_VW), sds(CONV_CH)],
        scratch_shapes=[pltpu.VMEM((FRONT_TILE + CONV_HALO, CONV_CH), F32),
                        pltpu.VMEM((SUBLANES - 1, FRONT_TILE + CONV_HALO - SUBLANES, CONV_CH), F32),
                        pltpu.VMEM((FRONT_TILE, CONV_CH), F32),
                        pltpu.VMEM((GLA_KW // LANES, GLA_DV, LANES), F32)],
        compiler_params=_params("parallel", "arbitrary"),
        name="front",
    )(x, g, wp, wp, wa2, ba, cos, sina, sinb, wdw, bdw, gln, bln, gn)


def _moba_attn_kernel(q_ref, k_ref, v_ref, o_ref, kmean_ref, vt_ref, s_ref, *, n_blocks):
    blk = MOBA_BLOCK
    pair = LANES // MOBA_HEAD_DIM
    hd = MOBA_HEAD_DIM
    kmean_ref[...] = jnp.zeros_like(kmean_ref)
    ones_row = lax.broadcasted_iota(jnp.int32, (MOBA_VT_ROWS - hd, blk), 0) == 0
    ones_pad = jnp.where(ones_row, 1.0, 0.0).astype(BF16)
    for j in range(n_blocks):
        kj = k_ref[j * blk:(j + 1) * blk, :].astype(F32)
        kmean_ref[j:j + 1, :] = jnp.mean(kj, axis=0, keepdims=True)
        vt = v_ref[j * blk:(j + 1) * blk, :].astype(F32).T.astype(BF16)
        for h in range(pair):
            vt_ref[j, h, 0:hd, :] = vt[h * hd:(h + 1) * hd, :]
            vt_ref[j, h, hd:, :] = ones_pad

    kpos = lax.broadcasted_iota(jnp.int32, (blk, blk), 0)
    qpos = lax.broadcasted_iota(jnp.int32, (blk, blk), 1)
    causal = kpos <= qpos
    qlane = lax.broadcasted_iota(jnp.int32, (blk, LANES), 1) // hd
    sub = lambda t: t.reshape(blk // SUBLANES, SUBLANES, blk)
    col_max = {}

    def scores(ii):
        q = q_ref[ii * blk:(ii + 1) * blk, :]
        need_gate = ii > MOBA_TOPK
        if need_gate:
            km_hi, km_mid, km_lo = _split3(kmean_ref[...])
            jidx = lax.broadcasted_iota(jnp.int32, (kmean_ref.shape[0], blk), 0)
        for h in range(pair):
            qh = jnp.where(qlane == h, q, jnp.zeros_like(q))
            valid = [None] * ii
            if need_gate:
                gate = _nt(km_hi, qh) + _nt(km_mid, qh) + _nt(km_lo, qh)
                for n in range(ii):
                    gn = gate[n:n + 1, :]
                    beats = ((gate > gn) | ((gate == gn) & (jidx < n))) & (jidx < ii)
                    rank = jnp.sum(jnp.where(beats, 1.0, 0.0), axis=0, keepdims=True)
                    valid[n] = rank < MOBA_TOPK
            m8 = None
            for j in range(ii + 1):
                s = _nt(k_ref[j * blk:(j + 1) * blk, :], qh)
                if j == ii:
                    s = jnp.where(causal, s, NEG)
                elif valid[j] is not None:
                    s = jnp.where(valid[j], s, NEG)
                s_ref[ii % 2, h, j] = s
                t = jnp.max(sub(s), axis=0)
                m8 = t if m8 is None else jnp.maximum(m8, t)
                yield
            col_max[ii, h] = jnp.max(m8, axis=0, keepdims=True)

    def values(ii):
        outs = []
        for h in range(pair):
            acc = None
            for j in range(ii + 1):
                p = jnp.exp2(s_ref[ii % 2, h, j] - col_max[ii, h])
                t = _mm(vt_ref[j, h], p.astype(BF16))
                acc = t if acc is None else acc + t
                yield
            outs.append(acc[0:hd, :] / acc[hd:hd + 1, :])
        o_ref[ii * blk:(ii + 1) * blk, :] = jnp.concatenate(outs, axis=0).T.astype(BF16)

    for _ in scores(0):
        pass
    for ii in range(n_blocks):
        live = [values(ii)]
        if ii + 1 < n_blocks:
            live.insert(0, scores(ii + 1))
        while live:
            for gen in list(live):
                if next(gen, StopIteration) is StopIteration:
                    live.remove(gen)


def _moba_attn(q, k, v, batch, seq):
    n_blocks = seq // MOBA_BLOCK
    n_pairs = MOBA_WIDTH // LANES
    pair = LANES // MOBA_HEAD_DIM
    spec = pl.BlockSpec((seq, LANES), lambda b, p: (b, p))
    gate_rows = 2 * SUBLANES
    assert n_blocks <= gate_rows
    return pl.pallas_call(
        functools.partial(_moba_attn_kernel, n_blocks=n_blocks),
        grid=(batch, n_pairs),
        in_specs=[spec, spec, spec],
        out_specs=spec,
        out_shape=jax.ShapeDtypeStruct(q.shape, BF16),
        scratch_shapes=[
            pltpu.VMEM((gate_rows, LANES), F32),
            pltpu.VMEM((n_blocks, pair, MOBA_VT_ROWS, MOBA_BLOCK), BF16),
            pltpu.VMEM((2, pair, n_blocks, MOBA_BLOCK, MOBA_BLOCK), F32),
        ],
        compiler_params=_params("parallel", "parallel"),
        name="moba_attn",
    )(q, k, v)


def _combine_kernel(x_ref, g_ref, wga_ref, wgb_ref, wgc_ref, bg_ref, a_ref, b_ref, c_ref, wa_ref, wb_ref,
                    wc_ref, wo_ref, gp_ref, o_ref):
    x = x_ref[...]
    xn = _rms(x, g_ref[...]).astype(BF16)
    mix = None
    branches = ((wga_ref, a_ref, wa_ref), (wgb_ref, b_ref, wb_ref), (wgc_ref, c_ref, wc_ref))
    for n, (wg_ref, br_ref, wbr_ref) in enumerate(branches):
        cols = slice(n * D_MODEL, (n + 1) * D_MODEL)
        gate = jax.nn.sigmoid(_mm(xn, wg_ref[...]) + bg_ref[:, cols])
        term = gate * _mm(br_ref[...], wbr_ref[...])
        mix = term if mix is None else mix + term
    y = _mm(mix.astype(BF16), wo_ref[...])
    o_ref[...] = x + _rms(y, gp_ref[...])


def _combine(layer, x, g, wp, bg, a, b, c, wa, wb, wc, wo, gp):
    T = x.shape[0]
    row = lambda n: pl.BlockSpec((ROW_TILE, n), lambda i: (i, 0))
    gate_w = lambda n: _layer_spec(layer, D_MODEL, D_MODEL, PK_GATE // D_MODEL + n)
    return pl.pallas_call(
        _combine_kernel,
        grid=(T // ROW_TILE,),
        in_specs=[row(D_MODEL), _vec_spec(layer, D_MODEL), gate_w(0), gate_w(1), gate_w(2),
                  _vec_spec(layer, N_BRANCH * D_MODEL),
                  row(MOBA_WIDTH), row(GLA_VW), row(CONV_CH),
                  _layer_spec(layer, MOBA_WIDTH, D_MODEL), _layer_spec(layer, GLA_VW, D_MODEL),
                  _layer_spec(layer, CONV_CH, D_MODEL), _layer_spec(layer, D_MODEL, D_MODEL),
                  _vec_spec(layer, D_MODEL)],
        out_specs=row(D_MODEL),
        out_shape=jax.ShapeDtypeStruct((T, D_MODEL), F32),
        compiler_params=_params("parallel"),
        name="combine",
    )(x, g, wp, wp, wp, bg, a, b, c, wa, wb, wc, wo, gp)


def _mlp_kernel(x_ref, g_ref, wu_ref, wd_ref, gp_ref, o_ref):
    x = x_ref[...]
    xn = _rms(x, g_ref[...]).astype(BF16)
    acc = None
    for c in range(D_FF // D_MODEL):
        cols = slice(c * D_MODEL, (c + 1) * D_MODEL)
        u = jnp.maximum(_mm(xn, wu_ref[:, cols]), 0.0)
        t = _mm((u * u).astype(BF16), wd_ref[cols, :])
        acc = t if acc is None else acc + t
    o_ref[...] = x + _rms(acc, gp_ref[...])


def _mlp(layer, x, g, wu, wd, gp):
    T = x.shape[0]
    row = pl.BlockSpec((ROW_TILE, D_MODEL), lambda i: (i, 0))
    return pl.pallas_call(
        _mlp_kernel,
        grid=(T // ROW_TILE,),
        in_specs=[row, _vec_spec(layer, D_MODEL), _layer_spec(layer, D_MODEL, D_FF),
                  _layer_spec(layer, D_FF, D_MODEL), _vec_spec(layer, D_MODEL)],
        out_specs=row,
        out_shape=jax.ShapeDtypeStruct((T, D_MODEL), F32),
        compiler_params=_params("parallel"),
        name="mlp",
    )(x, g, wu, wd, gp)


def kernel(x, positions, g_mix_pre, w_in, b_gate, w_gla_a2, b_gla_a, g_gla_norm, w_dw, b_dw, g_conv_ln,
           b_conv_ln, w_moba_o, w_gla_o, w_conv_o, w_mix_out, g_mix_post, g_mlp_pre, w_mlp_up, w_mlp_down,
           g_mlp_post):
    batch, seq, _ = x.shape
    depth = w_in.shape[0]
    assert seq % MOBA_BLOCK == 0 and seq % FRONT_TILE == 0 and (batch * seq) % ROW_TILE == 0
    assert FRONT_TILE % GLA_CHUNK == 0
    h = x.reshape(batch * seq, D_MODEL)
    cos, sina, sinb = _rope_tables(positions)

    o_la = 3 * MOBA_WIDTH + 2 * GLA_KW + GLA_VW
    o_lr = o_la + GLA_RANK
    pad_la = jnp.pad(w_in[:, :, o_la:o_lr], ((0, 0), (0, 0), (0, LANES - GLA_RANK)))
    wp = jnp.concatenate([w_in[:, :, :o_la], w_in[:, :, o_lr:], pad_la], axis=-1).astype(BF16)
    assert wp.shape[-1] == PK_LA + LANES
    wa2 = jnp.pad(w_gla_a2, ((0, 0), (0, LANES - GLA_RANK), (0, 0))).astype(BF16)
    vec = lambda v: v[:, None, :]
    g_pre, g_post, g_mpre, g_mpost = vec(g_mix_pre), vec(g_mix_post), vec(g_mlp_pre), vec(g_mlp_post)
    bg, ba, gn = vec(b_gate), vec(b_gla_a), vec(g_gla_norm)
    bdw, gln, bln = vec(b_dw), vec(g_conv_ln), vec(b_conv_ln)
    wa, wb, wc, wo = (w.astype(BF16) for w in (w_moba_o, w_gla_o, w_conv_o, w_mix_out))
    wu, wd = w_mlp_up.astype(BF16), w_mlp_down.astype(BF16)

    for l in range(depth):
        mq, mk, mv, ob, oc = _front(l, h, g_pre, wp, wa2, ba, cos, sina, sinb, w_dw, bdw, gln, bln, gn, batch, seq)
        oa = _moba_attn(mq, mk, mv, batch, seq)
        h = _combine(l, h, g_pre, wp, bg, oa, ob, oc, wa, wb, wc, wo, g_post)
        h = _mlp(l, h, g_mpre, wu, wd, g_mpost)
    return h.reshape(batch, seq, D_MODEL)
```

```python
import functools

import jax
import jax.numpy as jnp
from jax import lax
from jax.experimental import pallas as pl
from jax.experimental.pallas import tpu as pltpu

D_MODEL = 1024
MOBA_HEADS = 8
MOBA_HEAD_DIM = 64
MOBA_WIDTH = MOBA_HEADS * MOBA_HEAD_DIM
MOBA_BLOCK = 256
MOBA_TOPK = 3
ROPE_THETA = 10000.0
GLA_HEADS = 4
GLA_DK = 64
GLA_DV = 128
GLA_KW = GLA_HEADS * GLA_DK
GLA_VW = GLA_HEADS * GLA_DV
GLA_RANK = 16
GLA_TEMP = 16.0
GLA_CHUNK = 64
CONV_CH = 512
CONV_WIDTH = 31
D_FF = 4 * D_MODEL
N_BRANCH = 3
EPS = 1e-6
NEG = -1e30
LOG2E = 1.4426950408889634
MOBA_Q_SCALE = MOBA_HEAD_DIM ** -0.5 * LOG2E

LANES = 128
SUBLANES = 8
VMEM_LIMIT_BYTES = 56 * 1024 * 1024
MOBA_VT_ROWS = MOBA_HEAD_DIM + 2 * SUBLANES

ROW_TILE = 512
FRONT_TILE = 512
GLA_TILE = 256
CONV_HALO = 32
CONV_ROW_CHUNK = 64

PK_MOBA = 0
PK_GLA = 3 * MOBA_WIDTH
PK_CONV = PK_GLA + 2 * GLA_KW + 2 * GLA_VW
PK_GATE = PK_CONV + 2 * CONV_CH
PK_LA = PK_GATE + N_BRANCH * D_MODEL

F32 = jnp.float32
BF16 = jnp.bfloat16


def _nt(a, b):
    return lax.dot_general(a, b, (((1,), (1,)), ((), ())), preferred_element_type=F32)


def _tn(a, b):
    return lax.dot_general(a, b, (((0,), (0,)), ((), ())), preferred_element_type=F32)


def _mm(a, b):
    return jnp.dot(a, b, preferred_element_type=F32)


def _rms(x, g):
    return x * lax.rsqrt(jnp.mean(x * x, axis=-1, keepdims=True) + EPS) * g


def _split3(x):
    hi = x.astype(BF16)
    r1 = x - hi.astype(F32)
    mid = r1.astype(BF16)
    lo = (r1 - mid.astype(F32)).astype(BF16)
    return hi, mid, lo


def _params(*sem):
    return pltpu.CompilerParams(dimension_semantics=sem, vmem_limit_bytes=VMEM_LIMIT_BYTES)


def _const_spec(shape):
    nd = len(shape)
    return pl.BlockSpec(shape, lambda *_: (0,) * nd)


def _layer_spec(layer, rows, cols, col_block=0):
    return pl.BlockSpec((None, rows, cols), lambda *_: (layer, 0, col_block))


def _vec_spec(layer, n):
    return _layer_spec(layer, 1, n)


def _rope_table_kernel(pos_ref, invf_ref, cos_ref, sina_ref, sinb_ref):
    ang = pos_ref[...].astype(F32) * invf_ref[...]
    lane = lax.broadcasted_iota(jnp.int32, ang.shape, 1)
    first_half = (lane % MOBA_HEAD_DIM) < (MOBA_HEAD_DIM // 2)
    s = jnp.sin(ang)
    cos_ref[...] = jnp.cos(ang)
    sina_ref[...] = jnp.where(first_half, -s, 0.0)
    sinb_ref[...] = jnp.where(first_half, 0.0, s)


def _rope_tables(positions):
    T = positions.size
    half = MOBA_HEAD_DIM // 2
    inv_freq = ROPE_THETA ** (-jnp.arange(half, dtype=F32) / half)
    invf = jnp.tile(inv_freq, LANES // half)[None, :]
    tab = jax.ShapeDtypeStruct((T, LANES), F32)
    row = pl.BlockSpec((ROW_TILE, LANES), lambda i: (i, 0))
    return pl.pallas_call(
        _rope_table_kernel,
        grid=(T // ROW_TILE,),
        in_specs=[pl.BlockSpec((ROW_TILE, 1), lambda i: (i, 0)), _const_spec((1, LANES))],
        out_specs=[row, row, row],
        out_shape=[tab, tab, tab],
        compiler_params=_params("parallel"),
        name="rope_tables",
    )(positions.reshape(T, 1), invf)


def _rope(s, cos, sa, sb):
    return s * cos + pltpu.roll(s, LANES - 32, 1) * sa + pltpu.roll(s, 32, 1) * sb


def _gla_masks():
    ch, sc = GLA_CHUNK, GLA_TILE
    ri = lax.broadcasted_iota(jnp.int32, (sc, sc), 0)
    ci = lax.broadcasted_iota(jnp.int32, (sc, sc), 1)
    lower = (ri >= ci) & (ri // ch == ci // ch)
    tril = jnp.where(lower, 1.0, 0.0).astype(BF16)
    lane_head = lax.broadcasted_iota(jnp.int32, (sc, LANES), 1) // GLA_DK
    st_lane_head = lax.broadcasted_iota(jnp.int32, (GLA_DV, LANES), 1) // GLA_DK
    return lower, tril, lane_head, st_lane_head


def _gla_tile(q, k, v, r, d, gn, st_box, masks):
    ch, sc = GLA_CHUNK, GLA_TILE
    nc = sc // ch
    pair = LANES // GLA_DK
    lower, tril, lane_head, st_lane_head = masks
    d_hi, d_mid, d_lo = _split3(d)
    G = _mm(tril, d_hi) + _mm(tril, d_mid) + _mm(tril, d_lo)
    yield
    g_last = [G[(c + 1) * ch - 1:(c + 1) * ch, :] for c in range(nc)]
    g_last_rows = jnp.concatenate([jnp.broadcast_to(g, (ch, LANES)) for g in g_last], axis=0)
    qt = (q * jnp.exp(G) * (GLA_DK ** -0.5)).astype(BF16)
    kt = (k * jnp.exp(-G)).astype(BF16)
    kd = (k * jnp.exp(g_last_rows - G)).astype(BF16)
    yield
    qh, vh, o_intra, upd = [], [], [], []
    for h in range(pair):
        qh.append(jnp.where(lane_head == h, qt, jnp.zeros_like(qt)))
        a = jnp.where(lower, _nt(qh[h], kt), 0.0).astype(BF16)
        vh.append(v[:, h * GLA_DV:(h + 1) * GLA_DV])
        o_intra.append(_mm(a, vh[h]))
        upd.append([_tn(vh[h][c * ch:(c + 1) * ch], kd[c * ch:(c + 1) * ch]) for c in range(nc)])
        yield
    o_inter = [[] for _ in range(pair)]
    st = st_box[0]
    for c in range(nc):
        st_b = st.astype(BF16)
        u = upd[0][c]
        for h in range(pair):
            o_inter[h].append(_nt(qh[h][c * ch:(c + 1) * ch], st_b))
            if h:
                u = jnp.where(st_lane_head == h, upd[h][c], u)
        st = st * jnp.exp(g_last[c]) + u
    st_box[0] = st
    yield
    outs = []
    for h in range(pair):
        o = _rms(o_intra[h] + jnp.concatenate(o_inter[h], axis=0), gn)
        rh = r[:, h * GLA_DV:(h + 1) * GLA_DV]
        outs.append(o * (rh * jax.nn.sigmoid(rh)))
        yield
    return outs


def _drive(*gens):
    results = [None] * len(gens)
    live = list(enumerate(gens))
    while live:
        for item in list(live):
            n, gen = item
            try:
                next(gen)
            except StopIteration as stop:
                results[n] = stop.value
                live.remove(item)
    return results


def _front_kernel(x_ref, g_ref, w_ref, wla_ref, wa2_ref, ba_ref, cos_ref, sina_ref, sinb_ref,
                  wdw_ref, bdw_ref, gln_ref, bln_ref, gn_ref,
                  mq_ref, mk_ref, mv_ref, ob_ref, oc_ref, u_ref, us_ref, y_ref, st_ref):
    ts, halo = FRONT_TILE, CONV_HALO
    i = pl.program_id(1)

    @pl.when(i == 0)
    def _():
        u_ref[0:halo, :] = jnp.zeros((halo, CONV_CH), F32)
        st_ref[...] = jnp.zeros_like(st_ref)

    @pl.when(i > 0)
    def _():
        u_ref[0:halo, :] = u_ref[ts:ts + halo, :]

    xn = _rms(x_ref[...], g_ref[...]).astype(BF16)

    zc = _mm(xn, w_ref[:, PK_CONV:PK_CONV + 2 * CONV_CH])
    u_ref[halo:halo + ts, :] = zc[:, :CONV_CH] * jax.nn.sigmoid(zc[:, CONV_CH:])
    span = us_ref.shape[1]
    for b in range(1, SUBLANES):
        us_ref[b - 1] = u_ref[b:b + span, :]

    zm = _mm(xn, w_ref[:, PK_MOBA:PK_MOBA + 3 * MOBA_WIDTH])
    cos, sa, sb = cos_ref[...], sina_ref[...], sinb_ref[...]
    for c in range(MOBA_WIDTH // LANES):
        cols = slice(c * LANES, (c + 1) * LANES)
        mq_ref[:, cols] = (_rope(zm[:, cols], cos, sa, sb) * MOBA_Q_SCALE).astype(BF16)
        mk_ref[:, cols] = _rope(zm[:, MOBA_WIDTH + c * LANES:MOBA_WIDTH + (c + 1) * LANES], cos, sa, sb).astype(BF16)
    mv_ref[...] = zm[:, 2 * MOBA_WIDTH:].astype(BF16)

    zg = _mm(xn, w_ref[:, PK_GLA:PK_GLA + 2 * GLA_KW + 2 * GLA_VW])
    la = _mm(xn, wla_ref[...])
    pre = _mm(la.astype(BF16), wa2_ref[...]) + ba_ref[...]
    dec = (jnp.minimum(pre, 0.0) - jnp.log1p(jnp.exp(-jnp.abs(pre)))) / GLA_TEMP

    def conv():
        first = halo - (CONV_WIDTH - 1)
        for c in range(CONV_CH // LANES):
            cols = slice(c * LANES, (c + 1) * LANES)
            for r in range(0, ts, CONV_ROW_CHUNK):
                acc = jnp.broadcast_to(bdw_ref[:, cols], (CONV_ROW_CHUNK, LANES))
                for w in range(CONV_WIDTH):
                    a, b = divmod(first + w, SUBLANES)
                    lo = r + a * SUBLANES
                    tap = (u_ref[lo:lo + CONV_ROW_CHUNK, cols] if b == 0
                           else us_ref[b - 1, lo:lo + CONV_ROW_CHUNK, cols])
                    acc = acc + wdw_ref[w:w + 1, cols] * tap
                y_ref[r:r + CONV_ROW_CHUNK, cols] = acc
                yield

    masks = _gla_masks()
    gn = gn_ref[...]
    pair_dv = (LANES // GLA_DK) * GLA_DV
    n_pairs, n_sub = GLA_KW // LANES, ts // GLA_TILE
    st_boxes = [[st_ref[p]] for p in range(n_pairs)]
    glas = []
    for t in range(n_sub):
        rows = slice(t * GLA_TILE, (t + 1) * GLA_TILE)
        for p in range(n_pairs):
            kl = slice(p * LANES, (p + 1) * LANES)
            vl = slice(2 * GLA_KW + p * pair_dv, 2 * GLA_KW + (p + 1) * pair_dv)
            rl = slice(2 * GLA_KW + GLA_VW + p * pair_dv, 2 * GLA_KW + GLA_VW + (p + 1) * pair_dv)
            glas.append(_gla_tile(zg[rows, kl], zg[rows, GLA_KW + p * LANES:GLA_KW + (p + 1) * LANES],
                                  zg[rows, vl].astype(BF16), zg[rows, rl], dec[rows, kl], gn, st_boxes[p], masks))
    res = _drive(*glas, conv())
    for t in range(n_sub):
        for p in range(n_pairs):
            for h, o in enumerate(res[t * n_pairs + p]):
                ob_ref[t * GLA_TILE:(t + 1) * GLA_TILE, p * pair_dv + h * GLA_DV:p * pair_dv + (h + 1) * GLA_DV] = o.astype(BF16)
    for p in range(n_pairs):
        st_ref[p] = st_boxes[p][0]

    y = y_ref[...]
    mu = jnp.mean(y, axis=-1, keepdims=True)
    yc = y - mu
    var = jnp.mean(yc * yc, axis=-1, keepdims=True)
    yn = yc * lax.rsqrt(var + EPS) * gln_ref[...] + bln_ref[...]
    oc_ref[...] = (yn * jax.nn.sigmoid(yn)).astype(BF16)


def _front(layer, x, g, wp, wa2, ba, cos, sina, sinb, wdw, bdw, gln, bln, gn, batch, seq):
    T = x.shape[0]
    nt = seq // FRONT_TILE
    row = lambda n: pl.BlockSpec((FRONT_TILE, n), lambda b, i: (b * nt + i, 0))
    sds = lambda n: jax.ShapeDtypeStruct((T, n), BF16)
    assert PK_MOBA == 0 and PK_GATE % LANES == 0
    return pl.pallas_call(
        _front_kernel,
        grid=(batch, nt),
        in_specs=[row(D_MODEL), _vec_spec(layer, D_MODEL), _layer_spec(layer, D_MODEL, PK_GATE, 0),
                  _layer_spec(layer, D_MODEL, LANES, PK_LA // LANES), _layer_spec(layer, LANES, GLA_KW),
                  _vec_spec(layer, GLA_KW), row(LANES), row(LANES), row(LANES),
                  _layer_spec(layer, CONV_WIDTH, CONV_CH),
                  _vec_spec(layer, CONV_CH), _vec_spec(layer, CONV_CH), _vec_spec(layer, CONV_CH),
                  _vec_spec(layer, GLA_DV)],
        out_specs=[row(MOBA_WIDTH), row(MOBA_WIDTH), row(MOBA_WIDTH), row(GLA_VW), row(CONV_CH)],
        out_shape=[sds(MOBA_WIDTH), sds(MOBA_WIDTH), sds(MOBA_WIDTH), sds(GLA_VW), sds(CONV_CH)],
        scratch_shapes=[pltpu.VMEM((FRONT_TILE + CONV_HALO, CONV_CH), F32),
                        pltpu.VMEM((SUBLANES - 1, FRONT_TILE + CONV_HALO - SUBLANES, CONV_CH), F32),
                        pltpu.VMEM((FRONT_TILE, CONV_CH), F32),
                        pltpu.VMEM((GLA_KW // LANES, GLA_DV, LANES), F32)],
        compiler_params=_params("parallel", "arbitrary"),
        name="front",
    )(x, g, wp, wp, wa2, ba, cos, sina, sinb, wdw, bdw, gln, bln, gn)


def _moba_attn_kernel(q_ref, k_ref, v_ref, o_ref, kmean_ref, vt_ref, s_ref, *, n_blocks):
    blk = MOBA_BLOCK
    pair = LANES // MOBA_HEAD_DIM
    hd = MOBA_HEAD_DIM
    kmean_ref[...] = jnp.zeros_like(kmean_ref)
    ones_row = lax.broadcasted_iota(jnp.int32, (MOBA_VT_ROWS - hd, blk), 0) == 0
    ones_pad = jnp.where(ones_row, 1.0, 0.0).astype(BF16)
    for j in range(n_blocks):
        kj = k_ref[j * blk:(j + 1) * blk, :].astype(F32)
        kmean_ref[j:j + 1, :] = jnp.mean(kj, axis=0, keepdims=True)
        vt = v_ref[j * blk:(j + 1) * blk, :].astype(F32).T.astype(BF16)
        for h in range(pair):
            vt_ref[j, h, 0:hd, :] = vt[h * hd:(h + 1) * hd, :]
            vt_ref[j, h, hd:, :] = ones_pad

    kpos = lax.broadcasted_iota(jnp.int32, (blk, blk), 0)
    qpos = lax.broadcasted_iota(jnp.int32, (blk, blk), 1)
    causal = kpos <= qpos
    qlane = lax.broadcasted_iota(jnp.int32, (blk, LANES), 1) // hd
    sub = lambda t: t.reshape(blk // SUBLANES, SUBLANES, blk)
    col_max = {}

    def scores(ii):
        q = q_ref[ii * blk:(ii + 1) * blk, :]
        need_gate = ii > MOBA_TOPK
        if need_gate:
            km_hi, km_mid, km_lo = _split3(kmean_ref[...])
            jidx = lax.broadcasted_iota(jnp.int32, (kmean_ref.shape[0], blk), 0)
        for h in range(pair):
            qh = jnp.where(qlane == h, q, jnp.zeros_like(q))
            valid = [None] * ii
            if need_gate:
                gate = _nt(km_hi, qh) + _nt(km_mid, qh) + _nt(km_lo, qh)
                for n in range(ii):
                    gn = gate[n:n + 1, :]
                    beats = ((gate > gn) | ((gate == gn) & (jidx < n))) & (jidx < ii)
                    rank = jnp.sum(jnp.where(beats, 1.0, 0.0), axis=0, keepdims=True)
                    valid[n] = rank < MOBA_TOPK
            m8 = None
            for j in range(ii + 1):
                s = _nt(k_ref[j * blk:(j + 1) * blk, :], qh)
                if j == ii:
                    s = jnp.where(causal, s, NEG)
                elif valid[j] is not None:
                    s = jnp.where(valid[j], s, NEG)
                s_ref[ii % 2, h, j] = s
                t = jnp.max(sub(s), axis=0)
                m8 = t if m8 is None else jnp.maximum(m8, t)
                yield
            col_max[ii, h] = jnp.max(m8, axis=0, keepdims=True)

    def values(ii):
        outs = []
        for h in range(pair):
            acc = None
            for j in range(ii + 1):
                p = jnp.exp2(s_ref[ii % 2, h, j] - col_max[ii, h])
                t = _mm(vt_ref[j, h], p.astype(BF16))
                acc = t if acc is None else acc + t
                yield
            outs.append(acc[0:hd, :] / acc[hd:hd + 1, :])
        o_ref[ii * blk:(ii + 1) * blk, :] = jnp.concatenate(outs, axis=0).T.astype(BF16)

    for _ in scores(0):
        pass
    for ii in range(n_blocks):
        live = [values(ii)]
        if ii + 1 < n_blocks:
            live.insert(0, scores(ii + 1))
        while live:
            for gen in list(live):
                if next(gen, StopIteration) is StopIteration:
                    live.remove(gen)


def _moba_attn(q, k, v, batch, seq):
    n_blocks = seq // MOBA_BLOCK
    n_pairs = MOBA_WIDTH // LANES
    pair = LANES // MOBA_HEAD_DIM
    spec = pl.BlockSpec((seq, LANES), lambda b, p: (b, p))
    gate_rows = 2 * SUBLANES
    assert n_blocks <= gate_rows
    return pl.pallas_call(
        functools.partial(_moba_attn_kernel, n_blocks=n_blocks),
        grid=(batch, n_pairs),
        in_specs=[spec, spec, spec],
        out_specs=spec,
        out_shape=jax.ShapeDtypeStruct(q.shape, BF16),
        scratch_shapes=[
            pltpu.VMEM((gate_rows, LANES), F32),
            pltpu.VMEM((n_blocks, pair, MOBA_VT_ROWS, MOBA_BLOCK), BF16),
            pltpu.VMEM((2, pair, n_blocks, MOBA_BLOCK, MOBA_BLOCK), F32),
        ],
        compiler_params=_params("parallel", "parallel"),
        name="moba_attn",
    )(q, k, v)


def _combine_kernel(x_ref, g_ref, wga_ref, wgb_ref, wgc_ref, bg_ref, a_ref, b_ref, c_ref, wa_ref, wb_ref,
                    wc_ref, wo_ref, gp_ref, wu32_ref, wd32_ref, o_ref, wu_ref, wd_ref):
    wu_ref[...] = wu32_ref[...].astype(BF16)
    wd_ref[...] = wd32_ref[...].astype(BF16)
    x = x_ref[...]
    xn = _rms(x, g_ref[...]).astype(BF16)
    mix = None
    branches = ((wga_ref, a_ref, wa_ref), (wgb_ref, b_ref, wb_ref), (wgc_ref, c_ref, wc_ref))
    for n, (wg_ref, br_ref, wbr_ref) in enumerate(branches):
        cols = slice(n * D_MODEL, (n + 1) * D_MODEL)
        gate = jax.nn.sigmoid(_mm(xn, wg_ref[...]) + bg_ref[:, cols])
        term = gate * _mm(br_ref[...], wbr_ref[...])
        mix = term if mix is None else mix + term
    y = _mm(mix.astype(BF16), wo_ref[...])
    o_ref[...] = x + _rms(y, gp_ref[...])


def _combine(layer, x, g, wp, bg, a, b, c, wa, wb, wc, wo, gp, wu32, wd32):
    T = x.shape[0]
    slab = D_FF // (T // ROW_TILE)
    assert slab % LANES == 0
    row = lambda n: pl.BlockSpec((ROW_TILE, n), lambda i: (i, 0))
    gate_w = lambda n: _layer_spec(layer, D_MODEL, D_MODEL, PK_GATE // D_MODEL + n)
    return pl.pallas_call(
        _combine_kernel,
        grid=(T // ROW_TILE,),
        in_specs=[row(D_MODEL), _vec_spec(layer, D_MODEL), gate_w(0), gate_w(1), gate_w(2),
                  _vec_spec(layer, N_BRANCH * D_MODEL),
                  row(MOBA_WIDTH), row(GLA_VW), row(CONV_CH),
                  _layer_spec(layer, MOBA_WIDTH, D_MODEL), _layer_spec(layer, GLA_VW, D_MODEL),
                  _layer_spec(layer, CONV_CH, D_MODEL), _layer_spec(layer, D_MODEL, D_MODEL),
                  _vec_spec(layer, D_MODEL),
                  pl.BlockSpec((None, D_MODEL, slab), lambda i: (layer, 0, i)),
                  pl.BlockSpec((None, slab, D_MODEL), lambda i: (layer, i, 0))],
        out_specs=[row(D_MODEL), pl.BlockSpec((D_MODEL, slab), lambda i: (0, i)),
                   pl.BlockSpec((slab, D_MODEL), lambda i: (i, 0))],
        out_shape=[jax.ShapeDtypeStruct((T, D_MODEL), F32), jax.ShapeDtypeStruct((D_MODEL, D_FF), BF16),
                   jax.ShapeDtypeStruct((D_FF, D_MODEL), BF16)],
        compiler_params=_params("parallel"),
        name="combine",
    )(x, g, wp, wp, wp, bg, a, b, c, wa, wb, wc, wo, gp, wu32, wd32)


def _mlp_kernel(x_ref, g_ref, wu_ref, wd_ref, gp_ref, o_ref):
    x = x_ref[...]
    xn = _rms(x, g_ref[...]).astype(BF16)
    acc = None
    for c in range(D_FF // D_MODEL):
        cols = slice(c * D_MODEL, (c + 1) * D_MODEL)
        u = jnp.maximum(_mm(xn, wu_ref[:, cols]), 0.0)
        t = _mm((u * u).astype(BF16), wd_ref[cols, :])
        acc = t if acc is None else acc + t
    o_ref[...] = x + _rms(acc, gp_ref[...])


def _mlp(layer, x, g, wu, wd, gp):
    T = x.shape[0]
    row = pl.BlockSpec((ROW_TILE, D_MODEL), lambda i: (i, 0))
    return pl.pallas_call(
        _mlp_kernel,
        grid=(T // ROW_TILE,),
        in_specs=[row, _vec_spec(layer, D_MODEL), _const_spec((D_MODEL, D_FF)), _const_spec((D_FF, D_MODEL)),
                  _vec_spec(layer, D_MODEL)],
        out_specs=row,
        out_shape=jax.ShapeDtypeStruct((T, D_MODEL), F32),
        compiler_params=_params("parallel"),
        name="mlp",
    )(x, g, wu, wd, gp)


def kernel(x, positions, g_mix_pre, w_in, b_gate, w_gla_a2, b_gla_a, g_gla_norm, w_dw, b_dw, g_conv_ln,
           b_conv_ln, w_moba_o, w_gla_o, w_conv_o, w_mix_out, g_mix_post, g_mlp_pre, w_mlp_up, w_mlp_down,
           g_mlp_post):
    batch, seq, _ = x.shape
    depth = w_in.shape[0]
    assert seq % MOBA_BLOCK == 0 and seq % FRONT_TILE == 0 and (batch * seq) % ROW_TILE == 0
    assert FRONT_TILE % GLA_TILE == 0 and GLA_TILE % GLA_CHUNK == 0
    h = x.reshape(batch * seq, D_MODEL)
    cos, sina, sinb = _rope_tables(positions)

    o_la = 3 * MOBA_WIDTH + 2 * GLA_KW + GLA_VW
    o_lr = o_la + GLA_RANK
    pad_la = jnp.pad(w_in[:, :, o_la:o_lr], ((0, 0), (0, 0), (0, LANES - GLA_RANK)))
    wp = jnp.concatenate([w_in[:, :, :o_la], w_in[:, :, o_lr:], pad_la], axis=-1).astype(BF16)
    assert wp.shape[-1] == PK_LA + LANES
    wa2 = jnp.pad(w_gla_a2, ((0, 0), (0, LANES - GLA_RANK), (0, 0))).astype(BF16)
    vec = lambda v: v[:, None, :]
    g_pre, g_post, g_mpre, g_mpost = vec(g_mix_pre), vec(g_mix_post), vec(g_mlp_pre), vec(g_mlp_post)
    bg, ba, gn = vec(b_gate), vec(b_gla_a), vec(g_gla_norm)
    bdw, gln, bln = vec(b_dw), vec(g_conv_ln), vec(b_conv_ln)
    wa, wb, wc, wo = (w.astype(BF16) for w in (w_moba_o, w_gla_o, w_conv_o, w_mix_out))

    for l in range(depth):
        mq, mk, mv, ob, oc = _front(l, h, g_pre, wp, wa2, ba, cos, sina, sinb, w_dw, bdw, gln, bln, gn, batch, seq)
        oa = _moba_attn(mq, mk, mv, batch, seq)
        h, wu, wd = _combine(l, h, g_pre, wp, bg, oa, ob, oc, wa, wb, wc, wo, g_post, w_mlp_up, w_mlp_down)
        h = _mlp(l, h, g_mpre, wu, wd, g_mpost)
    return h.reshape(batch, seq, D_MODEL)
```

```python
import functools

import jax
import jax.numpy as jnp
from jax import lax
from jax.experimental import pallas as pl
from jax.experimental.pallas import tpu as pltpu

D_MODEL = 1024
MOBA_HEADS = 8
MOBA_HEAD_DIM = 64
MOBA_WIDTH = MOBA_HEADS * MOBA_HEAD_DIM
MOBA_BLOCK = 256
MOBA_TOPK = 3
ROPE_THETA = 10000.0
GLA_HEADS = 4
GLA_DK = 64
GLA_DV = 128
GLA_KW = GLA_HEADS * GLA_DK
GLA_VW = GLA_HEADS * GLA_DV
GLA_RANK = 16
GLA_TEMP = 16.0
GLA_CHUNK = 64
CONV_CH = 512
CONV_WIDTH = 31
D_FF = 4 * D_MODEL
N_BRANCH = 3
EPS = 1e-6
NEG = -1e30
LOG2E = 1.4426950408889634
MOBA_Q_SCALE = MOBA_HEAD_DIM ** -0.5 * LOG2E

LANES = 128
SUBLANES = 8
VMEM_LIMIT_BYTES = 56 * 1024 * 1024
MOBA_VT_ROWS = MOBA_HEAD_DIM + 2 * SUBLANES

ROW_TILE = 512
FRONT_TILE = 512
GLA_TILE = 256
CONV_HALO = 32
CONV_ROW_CHUNK = 64
PREP_ROWS = 64

PK_MOBA = 0
PK_GLA = 3 * MOBA_WIDTH
PK_CONV = PK_GLA + 2 * GLA_KW + 2 * GLA_VW
PK_GATE = PK_CONV + 2 * CONV_CH
PK_LA = PK_GATE + N_BRANCH * D_MODEL

F32 = jnp.float32
BF16 = jnp.bfloat16


def _nt(a, b):
    return lax.dot_general(a, b, (((1,), (1,)), ((), ())), preferred_element_type=F32)


def _tn(a, b):
    return lax.dot_general(a, b, (((0,), (0,)), ((), ())), preferred_element_type=F32)


def _mm(a, b):
    return jnp.dot(a, b, preferred_element_type=F32)


def _rms(x, g):
    return x * lax.rsqrt(jnp.mean(x * x, axis=-1, keepdims=True) + EPS) * g


def _split3(x):
    hi = x.astype(BF16)
    r1 = x - hi.astype(F32)
    mid = r1.astype(BF16)
    lo = (r1 - mid.astype(F32)).astype(BF16)
    return hi, mid, lo


def _params(*sem):
    return pltpu.CompilerParams(dimension_semantics=sem, vmem_limit_bytes=VMEM_LIMIT_BYTES)


def _const_spec(shape):
    nd = len(shape)
    return pl.BlockSpec(shape, lambda *_: (0,) * nd)


def _layer_spec(layer, rows, cols, col_block=0):
    return pl.BlockSpec((None, rows, cols), lambda *_: (layer, 0, col_block))


def _vec_spec(layer, n):
    return _layer_spec(layer, 1, n)


def _prep_kernel(w_ref, wa_ref, wb_ref, wc_ref, wo_ref, wp_ref, oa_ref, ob_ref, oc_ref, oo_ref):
    w = w_ref[...]
    o_la = PK_CONV - GLA_VW
    wp_ref[:, :o_la] = w[:, :o_la].astype(BF16)
    wp_ref[:, o_la:PK_LA] = w[:, o_la + GLA_RANK:].astype(BF16)
    pad = jnp.zeros((w.shape[0], LANES - GLA_RANK), F32)
    wp_ref[:, PK_LA:] = jnp.concatenate([w[:, o_la:o_la + GLA_RANK], pad], axis=1).astype(BF16)
    for src, dst in ((wa_ref, oa_ref), (wb_ref, ob_ref), (wc_ref, oc_ref), (wo_ref, oo_ref)):
        dst[...] = src[...].astype(BF16)


def _prep_weights(w_in, w_moba_o, w_gla_o, w_conv_o, w_mix_out):
    depth, rows, in_width = w_in.shape
    assert in_width == PK_LA + GLA_RANK
    steps = rows // PREP_ROWS
    outs = (w_moba_o, w_gla_o, w_conv_o, w_mix_out)
    slab = lambda a, n: pl.BlockSpec((None, a.shape[1] // steps, n), lambda l, i: (l, i, 0))
    return pl.pallas_call(
        _prep_kernel,
        grid=(depth, steps),
        in_specs=[slab(w_in, in_width)] + [slab(a, a.shape[2]) for a in outs],
        out_specs=[slab(w_in, PK_LA + LANES)] + [slab(a, a.shape[2]) for a in outs],
        out_shape=[jax.ShapeDtypeStruct((depth, rows, PK_LA + LANES), BF16)]
        + [jax.ShapeDtypeStruct(a.shape, BF16) for a in outs],
        compiler_params=_params("parallel", "parallel"),
        name="prep_weights",
    )(w_in, *outs)


def _rope_table_kernel(pos_ref, invf_ref, cos_ref, sina_ref, sinb_ref):
    ang = pos_ref[...].astype(F32) * invf_ref[...]
    lane = lax.broadcasted_iota(jnp.int32, ang.shape, 1)
    first_half = (lane % MOBA_HEAD_DIM) < (MOBA_HEAD_DIM // 2)
    s = jnp.sin(ang)
    cos_ref[...] = jnp.cos(ang)
    sina_ref[...] = jnp.where(first_half, -s, 0.0)
    sinb_ref[...] = jnp.where(first_half, 0.0, s)


def _rope_tables(positions):
    T = positions.size
    half = MOBA_HEAD_DIM // 2
    inv_freq = ROPE_THETA ** (-jnp.arange(half, dtype=F32) / half)
    invf = jnp.tile(inv_freq, LANES // half)[None, :]
    tab = jax.ShapeDtypeStruct((T, LANES), F32)
    row = pl.BlockSpec((ROW_TILE, LANES), lambda i: (i, 0))
    return pl.pallas_call(
        _rope_table_kernel,
        grid=(T // ROW_TILE,),
        in_specs=[pl.BlockSpec((ROW_TILE, 1), lambda i: (i, 0)), _const_spec((1, LANES))],
        out_specs=[row, row, row],
        out_shape=[tab, tab, tab],
        compiler_params=_params("parallel"),
        name="rope_tables",
    )(positions.reshape(T, 1), invf)


def _rope(s, cos, sa, sb):
    return s * cos + pltpu.roll(s, LANES - 32, 1) * sa + pltpu.roll(s, 32, 1) * sb


def _gla_masks():
    ch, sc = GLA_CHUNK, GLA_TILE
    ri = lax.broadcasted_iota(jnp.int32, (sc, sc), 0)
    ci = lax.broadcasted_iota(jnp.int32, (sc, sc), 1)
    lower = (ri >= ci) & (ri // ch == ci // ch)
    tril = jnp.where(lower, 1.0, 0.0).astype(BF16)
    lane_head = lax.broadcasted_iota(jnp.int32, (sc, LANES), 1) // GLA_DK
    st_lane_head = lax.broadcasted_iota(jnp.int32, (GLA_DV, LANES), 1) // GLA_DK
    return lower, tril, lane_head, st_lane_head


def _gla_tile(q, k, v, r, d, gn, st_box, masks):
    ch, sc = GLA_CHUNK, GLA_TILE
    nc = sc // ch
    pair = LANES // GLA_DK
    lower, tril, lane_head, st_lane_head = masks
    d_hi, d_mid, d_lo = _split3(d)
    G = _mm(tril, d_hi) + _mm(tril, d_mid) + _mm(tril, d_lo)
    yield
    g_last = [G[(c + 1) * ch - 1:(c + 1) * ch, :] for c in range(nc)]
    g_last_rows = jnp.concatenate([jnp.broadcast_to(g, (ch, LANES)) for g in g_last], axis=0)
    qt = (q * jnp.exp(G) * (GLA_DK ** -0.5)).astype(BF16)
    kt = (k * jnp.exp(-G)).astype(BF16)
    kd = (k * jnp.exp(g_last_rows - G)).astype(BF16)
    yield
    qh, vh, o_intra, upd = [], [], [], []
    for h in range(pair):
        qh.append(jnp.where(lane_head == h, qt, jnp.zeros_like(qt)))
        a = jnp.where(lower, _nt(qh[h], kt), 0.0).astype(BF16)
        vh.append(v[:, h * GLA_DV:(h + 1) * GLA_DV])
        o_intra.append(_mm(a, vh[h]))
        upd.append([_tn(vh[h][c * ch:(c + 1) * ch], kd[c * ch:(c + 1) * ch]) for c in range(nc)])
        yield
    o_inter = [[] for _ in range(pair)]
    st = st_box[0]
    for c in range(nc):
        st_b = st.astype(BF16)
        u = upd[0][c]
        for h in range(pair):
            o_inter[h].append(_nt(qh[h][c * ch:(c + 1) * ch], st_b))
            if h:
                u = jnp.where(st_lane_head == h, upd[h][c], u)
        st = st * jnp.exp(g_last[c]) + u
    st_box[0] = st
    yield
    outs = []
    for h in range(pair):
        o = _rms(o_intra[h] + jnp.concatenate(o_inter[h], axis=0), gn)
        rh = r[:, h * GLA_DV:(h + 1) * GLA_DV]
        outs.append(o * (rh * jax.nn.sigmoid(rh)))
        yield
    return outs


def _drive(*gens):
    results = [None] * len(gens)
    live = list(enumerate(gens))
    while live:
        for item in list(live):
            n, gen = item
            try:
                next(gen)
            except StopIteration as stop:
                results[n] = stop.value
                live.remove(item)
    return results


def _front_kernel(x_ref, g_ref, w_ref, wla_ref, wa2_ref, ba_ref, cos_ref, sina_ref, sinb_ref,
                  wdw_ref, bdw_ref, gln_ref, bln_ref, gn_ref,
                  mq_ref, mk_ref, mv_ref, ob_ref, oc_ref, u_ref, us_ref, y_ref, st_ref):
    ts, halo = FRONT_TILE, CONV_HALO
    i = pl.program_id(1)

    @pl.when(i == 0)
    def _():
        u_ref[0:halo, :] = jnp.zeros((halo, CONV_CH), F32)
        st_ref[...] = jnp.zeros_like(st_ref)

    @pl.when(i > 0)
    def _():
        u_ref[0:halo, :] = u_ref[ts:ts + halo, :]

    xn = _rms(x_ref[...], g_ref[...]).astype(BF16)

    zc = _mm(xn, w_ref[:, PK_CONV:PK_CONV + 2 * CONV_CH])
    u_ref[halo:halo + ts, :] = zc[:, :CONV_CH] * jax.nn.sigmoid(zc[:, CONV_CH:])
    span = us_ref.shape[1]
    for b in range(1, SUBLANES):
        us_ref[b - 1] = u_ref[b:b + span, :]

    zm = _mm(xn, w_ref[:, PK_MOBA:PK_MOBA + 3 * MOBA_WIDTH])
    cos, sa, sb = cos_ref[...], sina_ref[...], sinb_ref[...]
    for c in range(MOBA_WIDTH // LANES):
        cols = slice(c * LANES, (c + 1) * LANES)
        mq_ref[:, cols] = (_rope(zm[:, cols], cos, sa, sb) * MOBA_Q_SCALE).astype(BF16)
        mk_ref[:, cols] = _rope(zm[:, MOBA_WIDTH + c * LANES:MOBA_WIDTH + (c + 1) * LANES], cos, sa, sb).astype(BF16)
    mv_ref[...] = zm[:, 2 * MOBA_WIDTH:].astype(BF16)

    zg = _mm(xn, w_ref[:, PK_GLA:PK_GLA + 2 * GLA_KW + 2 * GLA_VW])
    la = _mm(xn, wla_ref[...])
    pre = _mm(la.astype(BF16), wa2_ref[...]) + ba_ref[...]
    dec = (jnp.minimum(pre, 0.0) - jnp.log1p(jnp.exp(-jnp.abs(pre)))) / GLA_TEMP

    def conv():
        first = halo - (CONV_WIDTH - 1)
        for c in range(CONV_CH // LANES):
            cols = slice(c * LANES, (c + 1) * LANES)
            for r in range(0, ts, CONV_ROW_CHUNK):
                acc = jnp.broadcast_to(bdw_ref[:, cols], (CONV_ROW_CHUNK, LANES))
                for w in range(CONV_WIDTH):
                    a, b = divmod(first + w, SUBLANES)
                    lo = r + a * SUBLANES
                    tap = (u_ref[lo:lo + CONV_ROW_CHUNK, cols] if b == 0
                           else us_ref[b - 1, lo:lo + CONV_ROW_CHUNK, cols])
                    acc = acc + wdw_ref[w:w + 1, cols] * tap
                y_ref[r:r + CONV_ROW_CHUNK, cols] = acc
                yield

    masks = _gla_masks()
    gn = gn_ref[...]
    pair_dv = (LANES // GLA_DK) * GLA_DV
    n_pairs, n_sub = GLA_KW // LANES, ts // GLA_TILE
    st_boxes = [[st_ref[p]] for p in range(n_pairs)]
    glas = []
    for t in range(n_sub):
        rows = slice(t * GLA_TILE, (t + 1) * GLA_TILE)
        for p in range(n_pairs):
            kl = slice(p * LANES, (p + 1) * LANES)
            vl = slice(2 * GLA_KW + p * pair_dv, 2 * GLA_KW + (p + 1) * pair_dv)
            rl = slice(2 * GLA_KW + GLA_VW + p * pair_dv, 2 * GLA_KW + GLA_VW + (p + 1) * pair_dv)
            glas.append(_gla_tile(zg[rows, kl], zg[rows, GLA_KW + p * LANES:GLA_KW + (p + 1) * LANES],
                                  zg[rows, vl].astype(BF16), zg[rows, rl], dec[rows, kl], gn, st_boxes[p], masks))
    res = _drive(*glas, conv())
    for t in range(n_sub):
        for p in range(n_pairs):
            for h, o in enumerate(res[t * n_pairs + p]):
                ob_ref[t * GLA_TILE:(t + 1) * GLA_TILE, p * pair_dv + h * GLA_DV:p * pair_dv + (h + 1) * GLA_DV] = o.astype(BF16)
    for p in range(n_pairs):
        st_ref[p] = st_boxes[p][0]

    y = y_ref[...]
    mu = jnp.mean(y, axis=-1, keepdims=True)
    yc = y - mu
    var = jnp.mean(yc * yc, axis=-1, keepdims=True)
    yn = yc * lax.rsqrt(var + EPS) * gln_ref[...] + bln_ref[...]
    oc_ref[...] = (yn * jax.nn.sigmoid(yn)).astype(BF16)


def _front(layer, x, g, wp, wa2, ba, cos, sina, sinb, wdw, bdw, gln, bln, gn, batch, seq):
    T = x.shape[0]
    nt = seq // FRONT_TILE
    row = lambda n: pl.BlockSpec((FRONT_TILE, n), lambda b, i: (b * nt + i, 0))
    sds = lambda n: jax.ShapeDtypeStruct((T, n), BF16)
    assert PK_MOBA == 0 and PK_GATE % LANES == 0
    return pl.pallas_call(
        _front_kernel,
        grid=(batch, nt),
        in_specs=[row(D_MODEL), _vec_spec(layer, D_MODEL), _layer_spec(layer, D_MODEL, PK_GATE, 0),
                  _layer_spec(layer, D_MODEL, LANES, PK_LA // LANES), _layer_spec(layer, LANES, GLA_KW),
                  _vec_spec(layer, GLA_KW), row(LANES), row(LANES), row(LANES),
                  _layer_spec(layer, CONV_WIDTH, CONV_CH),
                  _vec_spec(layer, CONV_CH), _vec_spec(layer, CONV_CH), _vec_spec(layer, CONV_CH),
                  _vec_spec(layer, GLA_DV)],
        out_specs=[row(MOBA_WIDTH), row(MOBA_WIDTH), row(MOBA_WIDTH), row(GLA_VW), row(CONV_CH)],
        out_shape=[sds(MOBA_WIDTH), sds(MOBA_WIDTH), sds(MOBA_WIDTH), sds(GLA_VW), sds(CONV_CH)],
        scratch_shapes=[pltpu.VMEM((FRONT_TILE + CONV_HALO, CONV_CH), F32),
                        pltpu.VMEM((SUBLANES - 1, FRONT_TILE + CONV_HALO - SUBLANES, CONV_CH), F32),
                        pltpu.VMEM((FRONT_TILE, CONV_CH), F32),
                        pltpu.VMEM((GLA_KW // LANES, GLA_DV, LANES), F32)],
        compiler_params=_params("parallel", "arbitrary"),
        name="front",
    )(x, g, wp, wp, wa2, ba, cos, sina, sinb, wdw, bdw, gln, bln, gn)


def _moba_attn_kernel(q_ref, k_ref, v_ref, o_ref, kmean_ref, vt_ref, s_ref, *, n_blocks):
    blk = MOBA_BLOCK
    pair = LANES // MOBA_HEAD_DIM
    hd = MOBA_HEAD_DIM
    kmean_ref[...] = jnp.zeros_like(kmean_ref)
    ones_row = lax.broadcasted_iota(jnp.int32, (MOBA_VT_ROWS - hd, blk), 0) == 0
    ones_pad = jnp.where(ones_row, 1.0, 0.0).astype(BF16)
    for j in range(n_blocks):
        kj = k_ref[j * blk:(j + 1) * blk, :].astype(F32)
        kmean_ref[j:j + 1, :] = jnp.mean(kj, axis=0, keepdims=True)
        vt = v_ref[j * blk:(j + 1) * blk, :].astype(F32).T.astype(BF16)
        for h in range(pair):
            vt_ref[j, h, 0:hd, :] = vt[h * hd:(h + 1) * hd, :]
            vt_ref[j, h, hd:, :] = ones_pad

    kpos = lax.broadcasted_iota(jnp.int32, (blk, blk), 0)
    qpos = lax.broadcasted_iota(jnp.int32, (blk, blk), 1)
    causal = kpos <= qpos
    qlane = lax.broadcasted_iota(jnp.int32, (blk, LANES), 1) // hd
    sub = lambda t: t.reshape(blk // SUBLANES, SUBLANES, blk)
    col_max = {}

    def scores(ii):
        q = q_ref[ii * blk:(ii + 1) * blk, :]
        need_gate = ii > MOBA_TOPK
        if need_gate:
            km_hi, km_mid, km_lo = _split3(kmean_ref[...])
            jidx = lax.broadcasted_iota(jnp.int32, (kmean_ref.shape[0], blk), 0)
        for h in range(pair):
            qh = jnp.where(qlane == h, q, jnp.zeros_like(q))
            valid = [None] * ii
            if need_gate:
                gate = _nt(km_hi, qh) + _nt(km_mid, qh) + _nt(km_lo, qh)
                for n in range(ii):
                    gn = gate[n:n + 1, :]
                    beats = ((gate > gn) | ((gate == gn) & (jidx < n))) & (jidx < ii)
                    rank = jnp.sum(jnp.where(beats, 1.0, 0.0), axis=0, keepdims=True)
                    valid[n] = rank < MOBA_TOPK
            m8 = None
            for j in range(ii + 1):
                s = _nt(k_ref[j * blk:(j + 1) * blk, :], qh)
                if j == ii:
                    s = jnp.where(causal, s, NEG)
                elif valid[j] is not None:
                    s = jnp.where(valid[j], s, NEG)
                s_ref[ii % 2, h, j] = s
                t = jnp.max(sub(s), axis=0)
                m8 = t if m8 is None else jnp.maximum(m8, t)
                yield
            col_max[ii, h] = jnp.max(m8, axis=0, keepdims=True)

    def values(ii):
        outs = []
        for h in range(pair):
            acc = None
            for j in range(ii + 1):
                p = jnp.exp2(s_ref[ii % 2, h, j] - col_max[ii, h])
                t = _mm(vt_ref[j, h], p.astype(BF16))
                acc = t if acc is None else acc + t
                yield
            outs.append(acc[0:hd, :] / acc[hd:hd + 1, :])
        o_ref[ii * blk:(ii + 1) * blk, :] = jnp.concatenate(outs, axis=0).T.astype(BF16)

    for _ in scores(0):
        pass
    for ii in range(n_blocks):
        live = [values(ii)]
        if ii + 1 < n_blocks:
            live.insert(0, scores(ii + 1))
        while live:
            for gen in list(live):
                if next(gen, StopIteration) is StopIteration:
                    live.remove(gen)


def _moba_attn(q, k, v, batch, seq):
    n_blocks = seq // MOBA_BLOCK
    n_pairs = MOBA_WIDTH // LANES
    pair = LANES // MOBA_HEAD_DIM
    spec = pl.BlockSpec((seq, LANES), lambda b, p: (b, p))
    gate_rows = 2 * SUBLANES
    assert n_blocks <= gate_rows
    return pl.pallas_call(
        functools.partial(_moba_attn_kernel, n_blocks=n_blocks),
        grid=(batch, n_pairs),
        in_specs=[spec, spec, spec],
        out_specs=spec,
        out_shape=jax.ShapeDtypeStruct(q.shape, BF16),
        scratch_shapes=[
            pltpu.VMEM((gate_rows, LANES), F32),
            pltpu.VMEM((n_blocks, pair, MOBA_VT_ROWS, MOBA_BLOCK), BF16),
            pltpu.VMEM((2, pair, n_blocks, MOBA_BLOCK, MOBA_BLOCK), F32),
        ],
        compiler_params=_params("parallel", "parallel"),
        name="moba_attn",
    )(q, k, v)


def _combine_kernel(x_ref, g_ref, wga_ref, wgb_ref, wgc_ref, bg_ref, a_ref, b_ref, c_ref, wa_ref, wb_ref,
                    wc_ref, wo_ref, gp_ref, wu32_ref, wd32_ref, o_ref, wu_ref, wd_ref):
    wu_ref[...] = wu32_ref[...].astype(BF16)
    wd_ref[...] = wd32_ref[...].astype(BF16)
    x = x_ref[...]
    xn = _rms(x, g_ref[...]).astype(BF16)
    mix = None
    branches = ((wga_ref, a_ref, wa_ref), (wgb_ref, b_ref, wb_ref), (wgc_ref, c_ref, wc_ref))
    for n, (wg_ref, br_ref, wbr_ref) in enumerate(branches):
        cols = slice(n * D_MODEL, (n + 1) * D_MODEL)
        gate = jax.nn.sigmoid(_mm(xn, wg_ref[...]) + bg_ref[:, cols])
        term = gate * _mm(br_ref[...], wbr_ref[...])
        mix = term if mix is None else mix + term
    y = _mm(mix.astype(BF16), wo_ref[...])
    o_ref[...] = x + _rms(y, gp_ref[...])


def _combine(layer, x, g, wp, bg, a, b, c, wa, wb, wc, wo, gp, wu32, wd32):
    T = x.shape[0]
    slab = D_FF // (T // ROW_TILE)
    assert slab % LANES == 0
    row = lambda n: pl.BlockSpec((ROW_TILE, n), lambda i: (i, 0))
    gate_w = lambda n: _layer_spec(layer, D_MODEL, D_MODEL, PK_GATE // D_MODEL + n)
    return pl.pallas_call(
        _combine_kernel,
        grid=(T // ROW_TILE,),
        in_specs=[row(D_MODEL), _vec_spec(layer, D_MODEL), gate_w(0), gate_w(1), gate_w(2),
                  _vec_spec(layer, N_BRANCH * D_MODEL),
                  row(MOBA_WIDTH), row(GLA_VW), row(CONV_CH),
                  _layer_spec(layer, MOBA_WIDTH, D_MODEL), _layer_spec(layer, GLA_VW, D_MODEL),
                  _layer_spec(layer, CONV_CH, D_MODEL), _layer_spec(layer, D_MODEL, D_MODEL),
                  _vec_spec(layer, D_MODEL),
                  pl.BlockSpec((None, D_MODEL, slab), lambda i: (layer, 0, i)),
                  pl.BlockSpec((None, slab, D_MODEL), lambda i: (layer, i, 0))],
        out_specs=[row(D_MODEL), pl.BlockSpec((D_MODEL, slab), lambda i: (0, i)),
                   pl.BlockSpec((slab, D_MODEL), lambda i: (i, 0))],
        out_shape=[jax.ShapeDtypeStruct((T, D_MODEL), F32), jax.ShapeDtypeStruct((D_MODEL, D_FF), BF16),
                   jax.ShapeDtypeStruct((D_FF, D_MODEL), BF16)],
        compiler_params=_params("parallel"),
        name="combine",
    )(x, g, wp, wp, wp, bg, a, b, c, wa, wb, wc, wo, gp, wu32, wd32)


def _mlp_kernel(x_ref, g_ref, wu_ref, wd_ref, gp_ref, o_ref):
    x = x_ref[...]
    xn = _rms(x, g_ref[...]).astype(BF16)
    acc = None
    for c in range(D_FF // D_MODEL):
        cols = slice(c * D_MODEL, (c + 1) * D_MODEL)
        u = jnp.maximum(_mm(xn, wu_ref[:, cols]), 0.0)
        t = _mm((u * u).astype(BF16), wd_ref[cols, :])
        acc = t if acc is None else acc + t
    o_ref[...] = x + _rms(acc, gp_ref[...])


def _mlp(layer, x, g, wu, wd, gp):
    T = x.shape[0]
    row = pl.BlockSpec((ROW_TILE, D_MODEL), lambda i: (i, 0))
    return pl.pallas_call(
        _mlp_kernel,
        grid=(T // ROW_TILE,),
        in_specs=[row, _vec_spec(layer, D_MODEL), _const_spec((D_MODEL, D_FF)), _const_spec((D_FF, D_MODEL)),
                  _vec_spec(layer, D_MODEL)],
        out_specs=row,
        out_shape=jax.ShapeDtypeStruct((T, D_MODEL), F32),
        compiler_params=_params("parallel"),
        name="mlp",
    )(x, g, wu, wd, gp)


def kernel(x, positions, g_mix_pre, w_in, b_gate, w_gla_a2, b_gla_a, g_gla_norm, w_dw, b_dw, g_conv_ln,
           b_conv_ln, w_moba_o, w_gla_o, w_conv_o, w_mix_out, g_mix_post, g_mlp_pre, w_mlp_up, w_mlp_down,
           g_mlp_post):
    batch, seq, _ = x.shape
    depth = w_in.shape[0]
    assert seq % MOBA_BLOCK == 0 and seq % FRONT_TILE == 0 and (batch * seq) % ROW_TILE == 0
    assert FRONT_TILE % GLA_TILE == 0 and GLA_TILE % GLA_CHUNK == 0
    h = x.reshape(batch * seq, D_MODEL)
    cos, sina, sinb = _rope_tables(positions)

    wp, wa, wb, wc, wo = _prep_weights(w_in, w_moba_o, w_gla_o, w_conv_o, w_mix_out)
    wa2 = jnp.pad(w_gla_a2, ((0, 0), (0, LANES - GLA_RANK), (0, 0))).astype(BF16)
    vec = lambda v: v[:, None, :]
    g_pre, g_post, g_mpre, g_mpost = vec(g_mix_pre), vec(g_mix_post), vec(g_mlp_pre), vec(g_mlp_post)
    bg, ba, gn = vec(b_gate), vec(b_gla_a), vec(g_gla_norm)
    bdw, gln, bln = vec(b_dw), vec(g_conv_ln), vec(b_conv_ln)

    for l in range(depth):
        mq, mk, mv, ob, oc = _front(l, h, g_pre, wp, wa2, ba, cos, sina, sinb, w_dw, bdw, gln, bln, gn, batch, seq)
        oa = _moba_attn(mq, mk, mv, batch, seq)
        h, wu, wd = _combine(l, h, g_pre, wp, bg, oa, ob, oc, wa, wb, wc, wo, g_post, w_mlp_up, w_mlp_down)
        h = _mlp(l, h, g_mpre, wu, wd, g_mpost)
    return h.reshape(batch, seq, D_MODEL)
```

```python
import functools

import jax
import jax.numpy as jnp
from jax import lax
from jax.experimental import pallas as pl
from jax.experimental.pallas import tpu as pltpu

D_MODEL = 1024
MOBA_HEADS = 8
MOBA_HEAD_DIM = 64
MOBA_WIDTH = MOBA_HEADS * MOBA_HEAD_DIM
MOBA_BLOCK = 256
MOBA_TOPK = 3
ROPE_THETA = 10000.0
GLA_HEADS = 4
GLA_DK = 64
GLA_DV = 128
GLA_KW = GLA_HEADS * GLA_DK
GLA_VW = GLA_HEADS * GLA_DV
GLA_RANK = 16
GLA_TEMP = 16.0
GLA_CHUNK = 64
CONV_CH = 512
CONV_WIDTH = 31
D_FF = 4 * D_MODEL
N_BRANCH = 3
EPS = 1e-6
NEG = -1e30
LOG2E = 1.4426950408889634
MOBA_Q_SCALE = MOBA_HEAD_DIM ** -0.5 * LOG2E

LANES = 128
SUBLANES = 8
VMEM_LIMIT_BYTES = 56 * 1024 * 1024
MOBA_VT_ROWS = MOBA_HEAD_DIM + 2 * SUBLANES

ROW_TILE = 512
FRONT_TILE = 512
GLA_TILE = 256
CONV_HALO = 32
CONV_ROW_CHUNK = 64

PK_MOBA = 0
PK_GLA = 3 * MOBA_WIDTH
PK_CONV = PK_GLA + 2 * GLA_KW + 2 * GLA_VW
PK_GATE = PK_CONV + 2 * CONV_CH
PK_LA = PK_GATE + N_BRANCH * D_MODEL

F32 = jnp.float32
BF16 = jnp.bfloat16


def _nt(a, b):
    return lax.dot_general(a, b, (((1,), (1,)), ((), ())), preferred_element_type=F32)


def _tn(a, b):
    return lax.dot_general(a, b, (((0,), (0,)), ((), ())), preferred_element_type=F32)


def _mm(a, b):
    return jnp.dot(a, b, preferred_element_type=F32)


def _rms(x, g):
    return x * lax.rsqrt(jnp.mean(x * x, axis=-1, keepdims=True) + EPS) * g


def _split3(x):
    hi = x.astype(BF16)
    r1 = x - hi.astype(F32)
    mid = r1.astype(BF16)
    lo = (r1 - mid.astype(F32)).astype(BF16)
    return hi, mid, lo


def _params(*sem):
    return pltpu.CompilerParams(dimension_semantics=sem, vmem_limit_bytes=VMEM_LIMIT_BYTES)


def _const_spec(shape):
    nd = len(shape)
    return pl.BlockSpec(shape, lambda *_: (0,) * nd)


def _layer_spec(layer, rows, cols, col_block=0):
    return pl.BlockSpec((None, rows, cols), lambda *_: (layer, 0, col_block))


def _vec_spec(layer, n):
    return _layer_spec(layer, 1, n)


def _rope_table_kernel(pos_ref, invf_ref, cos_ref, sin_ref):
    ang = pos_ref[...].astype(F32) * invf_ref[...]
    cos_ref[...] = jnp.cos(ang)
    sin_ref[...] = jnp.sin(ang)


def _rope_tables(positions):
    T = positions.size
    half = MOBA_HEAD_DIM // 2
    per_row = LANES // half
    inv_freq = ROPE_THETA ** (-jnp.arange(half, dtype=F32) / half)
    invf = jnp.tile(inv_freq, per_row)[None, :]
    pos = jnp.repeat(positions.reshape(T // per_row, per_row), half, axis=1)
    rows = T // per_row
    tile = min(ROW_TILE, rows)
    assert rows % tile == 0
    tab = jax.ShapeDtypeStruct((rows, LANES), F32)
    row = pl.BlockSpec((tile, LANES), lambda i: (i, 0))
    cos, sin = pl.pallas_call(
        _rope_table_kernel,
        grid=(rows // tile,),
        in_specs=[row, _const_spec((1, LANES))],
        out_specs=[row, row],
        out_shape=[tab, tab],
        compiler_params=_params("parallel"),
        name="rope_tables",
    )(pos, invf)
    return cos.reshape(T, half), sin.reshape(T, half)


def _rope(s, cos, sa, sb):
    return s * cos + pltpu.roll(s, LANES - 32, 1) * sa + pltpu.roll(s, 32, 1) * sb


def _gla_masks():
    ch, sc = GLA_CHUNK, GLA_TILE
    ri = lax.broadcasted_iota(jnp.int32, (sc, sc), 0)
    ci = lax.broadcasted_iota(jnp.int32, (sc, sc), 1)
    lower = (ri >= ci) & (ri // ch == ci // ch)
    tril = jnp.where(lower, 1.0, 0.0).astype(BF16)
    lane_head = lax.broadcasted_iota(jnp.int32, (sc, LANES), 1) // GLA_DK
    st_lane_head = lax.broadcasted_iota(jnp.int32, (GLA_DV, LANES), 1) // GLA_DK
    return lower, tril, lane_head, st_lane_head


def _gla_tile(q, k, v, r, d, gn, st_box, masks):
    ch, sc = GLA_CHUNK, GLA_TILE
    nc = sc // ch
    pair = LANES // GLA_DK
    lower, tril, lane_head, st_lane_head = masks
    d_hi, d_mid, d_lo = _split3(d)
    G = _mm(tril, d_hi) + _mm(tril, d_mid) + _mm(tril, d_lo)
    yield
    per_chunk = lambda row: jnp.concatenate(
        [jnp.broadcast_to(G[c * ch + row:c * ch + row + 1, :], (ch, LANES)) for c in range(nc)], axis=0)
    g_last = [G[(c + 1) * ch - 1:(c + 1) * ch, :] for c in range(nc)]
    g_last_rows = per_chunk(ch - 1)
    g_mid_rows = per_chunk(ch // 2 - 1)
    scale = GLA_DK ** -0.5
    qa = (q * jnp.exp(G - g_mid_rows) * scale).astype(BF16)
    kt = (k * jnp.exp(g_mid_rows - G)).astype(BF16)
    qt = (q * jnp.exp(G) * scale).astype(BF16)
    kd = (k * jnp.exp(g_last_rows - G)).astype(BF16)
    yield
    qh, vh, o_intra, upd = [], [], [], []
    for h in range(pair):
        qh.append(jnp.where(lane_head == h, qt, jnp.zeros_like(qt)))
        qah = jnp.where(lane_head == h, qa, jnp.zeros_like(qa))
        a = jnp.where(lower, _nt(qah, kt), 0.0).astype(BF16)
        vh.append(v[:, h * GLA_DV:(h + 1) * GLA_DV])
        o_intra.append(_mm(a, vh[h]))
        upd.append([_tn(vh[h][c * ch:(c + 1) * ch], kd[c * ch:(c + 1) * ch]) for c in range(nc)])
        yield
    o_inter = [[] for _ in range(pair)]
    st = st_box[0]
    for c in range(nc):
        st_b = st.astype(BF16)
        u = upd[0][c]
        for h in range(pair):
            o_inter[h].append(_nt(qh[h][c * ch:(c + 1) * ch], st_b))
            if h:
                u = jnp.where(st_lane_head == h, upd[h][c], u)
        st = st * jnp.exp(g_last[c]) + u
    st_box[0] = st
    yield
    outs = []
    for h in range(pair):
        o = _rms(o_intra[h] + jnp.concatenate(o_inter[h], axis=0), gn)
        rh = r[:, h * GLA_DV:(h + 1) * GLA_DV]
        outs.append(o * (rh * jax.nn.sigmoid(rh)))
        yield
    return outs


def _drive(*gens):
    results = [None] * len(gens)
    live = list(enumerate(gens))
    while live:
        for item in list(live):
            n, gen = item
            try:
                next(gen)
            except StopIteration as stop:
                results[n] = stop.value
                live.remove(item)
    return results


def _front_kernel(x_ref, g_ref, w_ref, wla_ref, wa2_ref, ba_ref, cos_ref, sin_ref,
                  wdw_ref, bdw_ref, gln_ref, bln_ref, gn_ref,
                  mq_ref, mk_ref, mv_ref, ob_ref, oc_ref, u_ref, us_ref, y_ref, st_ref):
    ts, halo = FRONT_TILE, CONV_HALO
    i = pl.program_id(1)

    @pl.when(i == 0)
    def _():
        u_ref[0:halo, :] = jnp.zeros((halo, CONV_CH), F32)
        st_ref[...] = jnp.zeros_like(st_ref)

    @pl.when(i > 0)
    def _():
        u_ref[0:halo, :] = u_ref[ts:ts + halo, :]

    xn = _rms(x_ref[...], g_ref[...]).astype(BF16)

    zc = _mm(xn, w_ref[:, PK_CONV:PK_CONV + 2 * CONV_CH])
    u_ref[halo:halo + ts, :] = zc[:, :CONV_CH] * jax.nn.sigmoid(zc[:, CONV_CH:])
    span = us_ref.shape[1]
    for b in range(1, SUBLANES):
        us_ref[b - 1] = u_ref[b:b + span, :]

    zm = _mm(xn, w_ref[:, PK_MOBA:PK_MOBA + 3 * MOBA_WIDTH])
    reps = LANES // cos_ref.shape[1]
    cos = jnp.tile(cos_ref[...], (1, reps))
    sin = jnp.tile(sin_ref[...], (1, reps))
    first_half = (lax.broadcasted_iota(jnp.int32, sin.shape, 1) % MOBA_HEAD_DIM) < (MOBA_HEAD_DIM // 2)
    sa = jnp.where(first_half, -sin, 0.0)
    sb = jnp.where(first_half, 0.0, sin)
    for c in range(MOBA_WIDTH // LANES):
        cols = slice(c * LANES, (c + 1) * LANES)
        mq_ref[:, cols] = (_rope(zm[:, cols], cos, sa, sb) * MOBA_Q_SCALE).astype(BF16)
        mk_ref[:, cols] = _rope(zm[:, MOBA_WIDTH + c * LANES:MOBA_WIDTH + (c + 1) * LANES], cos, sa, sb).astype(BF16)
    mv_ref[...] = zm[:, 2 * MOBA_WIDTH:].astype(BF16)

    zg = _mm(xn, w_ref[:, PK_GLA:PK_GLA + 2 * GLA_KW + 2 * GLA_VW])
    la = _mm(xn, wla_ref[...])
    pre = _mm(la.astype(BF16), wa2_ref[...]) + ba_ref[...]
    dec = (jnp.minimum(pre, 0.0) - jnp.log1p(jnp.exp(-jnp.abs(pre)))) / GLA_TEMP

    def conv():
        first = halo - (CONV_WIDTH - 1)
        for c in range(CONV_CH // LANES):
            cols = slice(c * LANES, (c + 1) * LANES)
            for r in range(0, ts, CONV_ROW_CHUNK):
                acc = jnp.broadcast_to(bdw_ref[:, cols], (CONV_ROW_CHUNK, LANES))
                for w in range(CONV_WIDTH):
                    a, b = divmod(first + w, SUBLANES)
                    lo = r + a * SUBLANES
                    tap = (u_ref[lo:lo + CONV_ROW_CHUNK, cols] if b == 0
                           else us_ref[b - 1, lo:lo + CONV_ROW_CHUNK, cols])
                    acc = acc + wdw_ref[w:w + 1, cols] * tap
                y_ref[r:r + CONV_ROW_CHUNK, cols] = acc
                yield

    masks = _gla_masks()
    gn = gn_ref[...]
    pair_dv = (LANES // GLA_DK) * GLA_DV
    n_pairs, n_sub = GLA_KW // LANES, ts // GLA_TILE
    st_boxes = [[st_ref[p]] for p in range(n_pairs)]
    glas = []
    for t in range(n_sub):
        rows = slice(t * GLA_TILE, (t + 1) * GLA_TILE)
        for p in range(n_pairs):
            kl = slice(p * LANES, (p + 1) * LANES)
            vl = slice(2 * GLA_KW + p * pair_dv, 2 * GLA_KW + (p + 1) * pair_dv)
            rl = slice(2 * GLA_KW + GLA_VW + p * pair_dv, 2 * GLA_KW + GLA_VW + (p + 1) * pair_dv)
            glas.append(_gla_tile(zg[rows, kl], zg[rows, GLA_KW + p * LANES:GLA_KW + (p + 1) * LANES],
                                  zg[rows, vl].astype(BF16), zg[rows, rl], dec[rows, kl], gn, st_boxes[p], masks))
    res = _drive(*glas, conv())
    for t in range(n_sub):
        for p in range(n_pairs):
            for h, o in enumerate(res[t * n_pairs + p]):
                ob_ref[t * GLA_TILE:(t + 1) * GLA_TILE, p * pair_dv + h * GLA_DV:p * pair_dv + (h + 1) * GLA_DV] = o.astype(BF16)
    for p in range(n_pairs):
        st_ref[p] = st_boxes[p][0]

    y = y_ref[...]
    mu = jnp.mean(y, axis=-1, keepdims=True)
    yc = y - mu
    var = jnp.mean(yc * yc, axis=-1, keepdims=True)
    yn = yc * lax.rsqrt(var + EPS) * gln_ref[...] + bln_ref[...]
    oc_ref[...] = (yn * jax.nn.sigmoid(yn)).astype(BF16)


def _front(layer, x, g, wp, wa2, ba, cos, sin, wdw, bdw, gln, bln, gn, batch, seq):
    T = x.shape[0]
    nt = seq // FRONT_TILE
    row = lambda n: pl.BlockSpec((FRONT_TILE, n), lambda b, i: (b * nt + i, 0))
    sds = lambda n: jax.ShapeDtypeStruct((T, n), BF16)
    assert PK_MOBA == 0 and PK_GATE % LANES == 0
    return pl.pallas_call(
        _front_kernel,
        grid=(batch, nt),
        in_specs=[row(D_MODEL), _vec_spec(layer, D_MODEL), _layer_spec(layer, D_MODEL, PK_GATE, 0),
                  _layer_spec(layer, D_MODEL, LANES, PK_LA // LANES), _layer_spec(layer, LANES, GLA_KW),
                  _vec_spec(layer, GLA_KW), row(MOBA_HEAD_DIM // 2), row(MOBA_HEAD_DIM // 2),
                  _layer_spec(layer, CONV_WIDTH, CONV_CH),
                  _vec_spec(layer, CONV_CH), _vec_spec(layer, CONV_CH), _vec_spec(layer, CONV_CH),
                  _vec_spec(layer, GLA_DV)],
        out_specs=[row(MOBA_WIDTH), row(MOBA_WIDTH), row(MOBA_WIDTH), row(GLA_VW), row(CONV_CH)],
        out_shape=[sds(MOBA_WIDTH), sds(MOBA_WIDTH), sds(MOBA_WIDTH), sds(GLA_VW), sds(CONV_CH)],
        scratch_shapes=[pltpu.VMEM((FRONT_TILE + CONV_HALO, CONV_CH), F32),
                        pltpu.VMEM((SUBLANES - 1, FRONT_TILE + CONV_HALO - SUBLANES, CONV_CH), F32),
                        pltpu.VMEM((FRONT_TILE, CONV_CH), F32),
                        pltpu.VMEM((GLA_KW // LANES, GLA_DV, LANES), F32)],
        compiler_params=_params("parallel", "arbitrary"),
        name="front",
    )(x, g, wp, wp, wa2, ba, cos, sin, wdw, bdw, gln, bln, gn)


def _moba_attn_kernel(q_ref, k_ref, v_ref, o_ref, kmean_ref, vt_ref, s_ref, *, n_blocks):
    blk = MOBA_BLOCK
    pair = LANES // MOBA_HEAD_DIM
    hd = MOBA_HEAD_DIM
    kmean_ref[...] = jnp.zeros_like(kmean_ref)
    ones_row = lax.broadcasted_iota(jnp.int32, (MOBA_VT_ROWS - hd, blk), 0) == 0
    ones_pad = jnp.where(ones_row, 1.0, 0.0).astype(BF16)
    for j in range(n_blocks):
        kj = k_ref[j * blk:(j + 1) * blk, :].astype(F32)
        kmean_ref[j:j + 1, :] = jnp.mean(kj, axis=0, keepdims=True)
        vt = v_ref[j * blk:(j + 1) * blk, :].astype(F32).T.astype(BF16)
        for h in range(pair):
            vt_ref[j, h, 0:hd, :] = vt[h * hd:(h + 1) * hd, :]
            vt_ref[j, h, hd:, :] = ones_pad

    kpos = lax.broadcasted_iota(jnp.int32, (blk, blk), 0)
    qpos = lax.broadcasted_iota(jnp.int32, (blk, blk), 1)
    causal = kpos <= qpos
    qlane = lax.broadcasted_iota(jnp.int32, (blk, LANES), 1) // hd
    sub = lambda t: t.reshape(blk // SUBLANES, SUBLANES, blk)
    col_max = {}

    def scores(ii):
        q = q_ref[ii * blk:(ii + 1) * blk, :]
        need_gate = ii > MOBA_TOPK
        if need_gate:
            km_hi, km_mid, km_lo = _split3(kmean_ref[...])
            jidx = lax.broadcasted_iota(jnp.int32, (kmean_ref.shape[0], blk), 0)
        for h in range(pair):
            qh = jnp.where(qlane == h, q, jnp.zeros_like(q))
            valid = [None] * ii
            if need_gate:
                gate = _nt(km_hi, qh) + _nt(km_mid, qh) + _nt(km_lo, qh)
                for n in range(ii):
                    gn = gate[n:n + 1, :]
                    beats = ((gate > gn) | ((gate == gn) & (jidx < n))) & (jidx < ii)
                    rank = jnp.sum(jnp.where(beats, 1.0, 0.0), axis=0, keepdims=True)
                    valid[n] = rank < MOBA_TOPK
            m8 = None
            for j in range(ii + 1):
                s = _nt(k_ref[j * blk:(j + 1) * blk, :], qh)
                if j == ii:
                    s = jnp.where(causal, s, NEG)
                elif valid[j] is not None:
                    s = jnp.where(valid[j], s, NEG)
                s_ref[ii % 2, h, j] = s
                t = jnp.max(sub(s), axis=0)
                m8 = t if m8 is None else jnp.maximum(m8, t)
                yield
            col_max[ii, h] = jnp.max(m8, axis=0, keepdims=True)

    def values(ii):
        outs = []
        for h in range(pair):
            acc = None
            for j in range(ii + 1):
                p = jnp.exp2(s_ref[ii % 2, h, j] - col_max[ii, h])
                t = _mm(vt_ref[j, h], p.astype(BF16))
                acc = t if acc is None else acc + t
                yield
            outs.append(acc[0:hd, :] / acc[hd:hd + 1, :])
        o_ref[ii * blk:(ii + 1) * blk, :] = jnp.concatenate(outs, axis=0).T.astype(BF16)

    for _ in scores(0):
        pass
    for ii in range(n_blocks):
        live = [values(ii)]
        if ii + 1 < n_blocks:
            live.insert(0, scores(ii + 1))
        while live:
            for gen in list(live):
                if next(gen, StopIteration) is StopIteration:
                    live.remove(gen)


def _moba_attn(q, k, v, batch, seq):
    n_blocks = seq // MOBA_BLOCK
    n_pairs = MOBA_WIDTH // LANES
    pair = LANES // MOBA_HEAD_DIM
    spec = pl.BlockSpec((seq, LANES), lambda b, p: (b, p))
    gate_rows = 2 * SUBLANES
    assert n_blocks <= gate_rows
    return pl.pallas_call(
        functools.partial(_moba_attn_kernel, n_blocks=n_blocks),
        grid=(batch, n_pairs),
        in_specs=[spec, spec, spec],
        out_specs=spec,
        out_shape=jax.ShapeDtypeStruct(q.shape, BF16),
        scratch_shapes=[
            pltpu.VMEM((gate_rows, LANES), F32),
            pltpu.VMEM((n_blocks, pair, MOBA_VT_ROWS, MOBA_BLOCK), BF16),
            pltpu.VMEM((2, pair, n_blocks, MOBA_BLOCK, MOBA_BLOCK), F32),
        ],
        compiler_params=_params("parallel", "parallel"),
        name="moba_attn",
    )(q, k, v)


def _combine_kernel(x_ref, g_ref, wga_ref, wgb_ref, wgc_ref, bg_ref, a_ref, b_ref, c_ref, wa_ref, wb_ref,
                    wc_ref, wo_ref, gp_ref, wu32_ref, wd32_ref, o_ref, wu_ref, wd_ref):
    wu_ref[...] = wu32_ref[...].astype(BF16)
    wd_ref[...] = wd32_ref[...].astype(BF16)
    x = x_ref[...]
    xn = _rms(x, g_ref[...]).astype(BF16)
    mix = None
    branches = ((wga_ref, a_ref, wa_ref), (wgb_ref, b_ref, wb_ref), (wgc_ref, c_ref, wc_ref))
    for n, (wg_ref, br_ref, wbr_ref) in enumerate(branches):
        cols = slice(n * D_MODEL, (n + 1) * D_MODEL)
        gate = jax.nn.sigmoid(_mm(xn, wg_ref[...]) + bg_ref[:, cols])
        term = gate * _mm(br_ref[...], wbr_ref[...])
        mix = term if mix is None else mix + term
    y = _mm(mix.astype(BF16), wo_ref[...])
    o_ref[...] = x + _rms(y, gp_ref[...])


def _combine(layer, x, g, wp, bg, a, b, c, wa, wb, wc, wo, gp, wu32, wd32):
    T = x.shape[0]
    slab = D_FF // (T // ROW_TILE)
    assert slab % LANES == 0
    row = lambda n: pl.BlockSpec((ROW_TILE, n), lambda i: (i, 0))
    gate_w = lambda n: _layer_spec(layer, D_MODEL, D_MODEL, PK_GATE // D_MODEL + n)
    return pl.pallas_call(
        _combine_kernel,
        grid=(T // ROW_TILE,),
        in_specs=[row(D_MODEL), _vec_spec(layer, D_MODEL), gate_w(0), gate_w(1), gate_w(2),
                  _vec_spec(layer, N_BRANCH * D_MODEL),
                  row(MOBA_WIDTH), row(GLA_VW), row(CONV_CH),
                  _layer_spec(layer, MOBA_WIDTH, D_MODEL), _layer_spec(layer, GLA_VW, D_MODEL),
                  _layer_spec(layer, CONV_CH, D_MODEL), _layer_spec(layer, D_MODEL, D_MODEL),
                  _vec_spec(layer, D_MODEL),
                  pl.BlockSpec((None, D_MODEL, slab), lambda i: (layer, 0, i)),
                  pl.BlockSpec((None, slab, D_MODEL), lambda i: (layer, i, 0))],
        out_specs=[row(D_MODEL), pl.BlockSpec((D_MODEL, slab), lambda i: (0, i)),
                   pl.BlockSpec((slab, D_MODEL), lambda i: (i, 0))],
        out_shape=[jax.ShapeDtypeStruct((T, D_MODEL), F32), jax.ShapeDtypeStruct((D_MODEL, D_FF), BF16),
                   jax.ShapeDtypeStruct((D_FF, D_MODEL), BF16)],
        compiler_params=_params("parallel"),
        name="combine",
    )(x, g, wp, wp, wp, bg, a, b, c, wa, wb, wc, wo, gp, wu32, wd32)


def _mlp_kernel(x_ref, g_ref, wu_ref, wd_ref, gp_ref, o_ref):
    x = x_ref[...]
    xn = _rms(x, g_ref[...]).astype(BF16)
    acc = None
    for c in range(D_FF // D_MODEL):
        cols = slice(c * D_MODEL, (c + 1) * D_MODEL)
        u = jnp.maximum(_mm(xn, wu_ref[:, cols]), 0.0)
        t = _mm((u * u).astype(BF16), wd_ref[cols, :])
        acc = t if acc is None else acc + t
    o_ref[...] = x + _rms(acc, gp_ref[...])


def _mlp(layer, x, g, wu, wd, gp):
    T = x.shape[0]
    row = pl.BlockSpec((ROW_TILE, D_MODEL), lambda i: (i, 0))
    return pl.pallas_call(
        _mlp_kernel,
        grid=(T // ROW_TILE,),
        in_specs=[row, _vec_spec(layer, D_MODEL), _const_spec((D_MODEL, D_FF)), _const_spec((D_FF, D_MODEL)),
                  _vec_spec(layer, D_MODEL)],
        out_specs=row,
        out_shape=jax.ShapeDtypeStruct((T, D_MODEL), F32),
        compiler_params=_params("parallel"),
        name="mlp",
    )(x, g, wu, wd, gp)


def kernel(x, positions, g_mix_pre, w_in, b_gate, w_gla_a2, b_gla_a, g_gla_norm, w_dw, b_dw, g_conv_ln,
           b_conv_ln, w_moba_o, w_gla_o, w_conv_o, w_mix_out, g_mix_post, g_mlp_pre, w_mlp_up, w_mlp_down,
           g_mlp_post):
    batch, seq, _ = x.shape
    depth = w_in.shape[0]
    assert seq % MOBA_BLOCK == 0 and seq % FRONT_TILE == 0 and (batch * seq) % ROW_TILE == 0
    assert FRONT_TILE % GLA_TILE == 0 and GLA_TILE % GLA_CHUNK == 0
    h = x.reshape(batch * seq, D_MODEL)
    cos, sin = _rope_tables(positions)

    o_la = 3 * MOBA_WIDTH + 2 * GLA_KW + GLA_VW
    o_lr = o_la + GLA_RANK
    pad_la = jnp.pad(w_in[:, :, o_la:o_lr], ((0, 0), (0, 0), (0, LANES - GLA_RANK)))
    wp = jnp.concatenate([w_in[:, :, :o_la], w_in[:, :, o_lr:], pad_la], axis=-1).astype(BF16)
    assert wp.shape[-1] == PK_LA + LANES
    wa2 = jnp.pad(w_gla_a2, ((0, 0), (0, LANES - GLA_RANK), (0, 0))).astype(BF16)
    vec = lambda v: v[:, None, :]
    g_pre, g_post, g_mpre, g_mpost = vec(g_mix_pre), vec(g_mix_post), vec(g_mlp_pre), vec(g_mlp_post)
    bg, ba, gn = vec(b_gate), vec(b_gla_a), vec(g_gla_norm)
    bdw, gln, bln = vec(b_dw), vec(g_conv_ln), vec(b_conv_ln)
    wa, wb, wc, wo = (w.astype(BF16) for w in (w_moba_o, w_gla_o, w_conv_o, w_mix_out))

    for l in range(depth):
        mq, mk, mv, ob, oc = _front(l, h, g_pre, wp, wa2, ba, cos, sin, w_dw, bdw, gln, bln, gn, batch, seq)
        oa = _moba_attn(mq, mk, mv, batch, seq)
        h, wu, wd = _combine(l, h, g_pre, wp, bg, oa, ob, oc, wa, wb, wc, wo, g_post, w_mlp_up, w_mlp_down)
        h = _mlp(l, h, g_mpre, wu, wd, g_mpost)
    return h.reshape(batch, seq, D_MODEL)
```

```python
import functools

import jax
import jax.numpy as jnp
from jax import lax
from jax.experimental import pallas as pl
from jax.experimental.pallas import tpu as pltpu

D_MODEL = 1024
MOBA_HEADS = 8
MOBA_HEAD_DIM = 64
MOBA_WIDTH = MOBA_HEADS * MOBA_HEAD_DIM
MOBA_BLOCK = 256
MOBA_TOPK = 3
ROPE_THETA = 10000.0
GLA_HEADS = 4
GLA_DK = 64
GLA_DV = 128
GLA_KW = GLA_HEADS * GLA_DK
GLA_VW = GLA_HEADS * GLA_DV
GLA_RANK = 16
GLA_TEMP = 16.0
GLA_CHUNK = 64
CONV_CH = 512
CONV_WIDTH = 31
D_FF = 4 * D_MODEL
N_BRANCH = 3
EPS = 1e-6
NEG = -1e30
LOG2E = 1.4426950408889634
MOBA_Q_SCALE = MOBA_HEAD_DIM ** -0.5 * LOG2E

LANES = 128
SUBLANES = 8
VMEM_LIMIT_BYTES = 56 * 1024 * 1024
MOBA_VT_ROWS = MOBA_HEAD_DIM + 2 * SUBLANES

ROW_TILE = 512
FRONT_TILE = 1024
GLA_TILE = 256
CONV_HALO = 32
CONV_ROW_CHUNK = 64

PK_MOBA = 0
PK_GLA = 3 * MOBA_WIDTH
PK_CONV = PK_GLA + 2 * GLA_KW + 2 * GLA_VW
PK_GATE = PK_CONV + 2 * CONV_CH
PK_LA = PK_GATE + N_BRANCH * D_MODEL

F32 = jnp.float32
BF16 = jnp.bfloat16


def _nt(a, b):
    return lax.dot_general(a, b, (((1,), (1,)), ((), ())), preferred_element_type=F32)


def _tn(a, b):
    return lax.dot_general(a, b, (((0,), (0,)), ((), ())), preferred_element_type=F32)


def _mm(a, b):
    return jnp.dot(a, b, preferred_element_type=F32)


def _rms(x, g):
    return x * lax.rsqrt(jnp.mean(x * x, axis=-1, keepdims=True) + EPS) * g


def _split3(x):
    hi = x.astype(BF16)
    r1 = x - hi.astype(F32)
    mid = r1.astype(BF16)
    lo = (r1 - mid.astype(F32)).astype(BF16)
    return hi, mid, lo


def _params(*sem):
    return pltpu.CompilerParams(dimension_semantics=sem, vmem_limit_bytes=VMEM_LIMIT_BYTES)


def _const_spec(shape):
    nd = len(shape)
    return pl.BlockSpec(shape, lambda *_: (0,) * nd)


def _layer_spec(layer, rows, cols, col_block=0):
    return pl.BlockSpec((None, rows, cols), lambda *_: (layer, 0, col_block))


def _vec_spec(layer, n):
    return _layer_spec(layer, 1, n)


def _rope_table_kernel(pos_ref, invf_ref, cos_ref, sin_ref):
    ang = pos_ref[...].astype(F32) * invf_ref[...]
    cos_ref[...] = jnp.cos(ang)
    sin_ref[...] = jnp.sin(ang)


def _rope_tables(positions):
    T = positions.size
    half = MOBA_HEAD_DIM // 2
    per_row = LANES // half
    inv_freq = ROPE_THETA ** (-jnp.arange(half, dtype=F32) / half)
    invf = jnp.tile(inv_freq, per_row)[None, :]
    pos = jnp.repeat(positions.reshape(T // per_row, per_row), half, axis=1)
    rows = T // per_row
    tile = min(ROW_TILE, rows)
    assert rows % tile == 0
    tab = jax.ShapeDtypeStruct((rows, LANES), F32)
    row = pl.BlockSpec((tile, LANES), lambda i: (i, 0))
    cos, sin = pl.pallas_call(
        _rope_table_kernel,
        grid=(rows // tile,),
        in_specs=[row, _const_spec((1, LANES))],
        out_specs=[row, row],
        out_shape=[tab, tab],
        compiler_params=_params("parallel"),
        name="rope_tables",
    )(pos, invf)
    return cos.reshape(T, half), sin.reshape(T, half)


def _rope(s, cos, sa, sb):
    return s * cos + pltpu.roll(s, LANES - 32, 1) * sa + pltpu.roll(s, 32, 1) * sb


def _gla_masks():
    ch, sc = GLA_CHUNK, GLA_TILE
    ri = lax.broadcasted_iota(jnp.int32, (sc, sc), 0)
    ci = lax.broadcasted_iota(jnp.int32, (sc, sc), 1)
    lower = (ri >= ci) & (ri // ch == ci // ch)
    tril = jnp.where(lower, 1.0, 0.0).astype(BF16)
    lane_head = lax.broadcasted_iota(jnp.int32, (sc, LANES), 1) // GLA_DK
    st_lane_head = lax.broadcasted_iota(jnp.int32, (GLA_DV, LANES), 1) // GLA_DK
    return lower, tril, lane_head, st_lane_head


def _gla_tile(q, k, v, r, d, gn, st_box, masks):
    ch, sc = GLA_CHUNK, GLA_TILE
    nc = sc // ch
    pair = LANES // GLA_DK
    lower, tril, lane_head, st_lane_head = masks
    d_hi, d_mid, d_lo = _split3(d)
    G = _mm(tril, d_hi) + _mm(tril, d_mid) + _mm(tril, d_lo)
    yield
    per_chunk = lambda row: jnp.concatenate(
        [jnp.broadcast_to(G[c * ch + row:c * ch + row + 1, :], (ch, LANES)) for c in range(nc)], axis=0)
    g_last = [G[(c + 1) * ch - 1:(c + 1) * ch, :] for c in range(nc)]
    g_last_rows = per_chunk(ch - 1)
    g_mid_rows = per_chunk(ch // 2 - 1)
    scale = GLA_DK ** -0.5
    qa = (q * jnp.exp(G - g_mid_rows) * scale).astype(BF16)
    kt = (k * jnp.exp(g_mid_rows - G)).astype(BF16)
    qt = (q * jnp.exp(G) * scale).astype(BF16)
    kd = (k * jnp.exp(g_last_rows - G)).astype(BF16)
    yield
    qh, vh, o_intra, upd = [], [], [], []
    for h in range(pair):
        qh.append(jnp.where(lane_head == h, qt, jnp.zeros_like(qt)))
        qah = jnp.where(lane_head == h, qa, jnp.zeros_like(qa))
        a = jnp.where(lower, _nt(qah, kt), 0.0).astype(BF16)
        vh.append(v[:, h * GLA_DV:(h + 1) * GLA_DV])
        o_intra.append(_mm(a, vh[h]))
        upd.append([_tn(vh[h][c * ch:(c + 1) * ch], kd[c * ch:(c + 1) * ch]) for c in range(nc)])
        yield
    o_inter = [[] for _ in range(pair)]
    st = st_box[0]
    for c in range(nc):
        st_b = st.astype(BF16)
        u = upd[0][c]
        for h in range(pair):
            o_inter[h].append(_nt(qh[h][c * ch:(c + 1) * ch], st_b))
            if h:
                u = jnp.where(st_lane_head == h, upd[h][c], u)
        st = st * jnp.exp(g_last[c]) + u
    st_box[0] = st
    yield
    outs = []
    for h in range(pair):
        o = _rms(o_intra[h] + jnp.concatenate(o_inter[h], axis=0), gn)
        rh = r[:, h * GLA_DV:(h + 1) * GLA_DV]
        outs.append(o * (rh * jax.nn.sigmoid(rh)))
        yield
    return outs


def _drive(*gens):
    results = [None] * len(gens)
    live = list(enumerate(gens))
    while live:
        for item in list(live):
            n, gen = item
            try:
                next(gen)
            except StopIteration as stop:
                results[n] = stop.value
                live.remove(item)
    return results


def _front_kernel(x_ref, g_ref, w_ref, wla_ref, wa2_ref, ba_ref, cos_ref, sin_ref,
                  wdw_ref, bdw_ref, gln_ref, bln_ref, gn_ref,
                  mq_ref, mk_ref, mv_ref, ob_ref, oc_ref, u_ref, us_ref, y_ref, st_ref):
    ts, halo = FRONT_TILE, CONV_HALO
    i = pl.program_id(1)

    @pl.when(i == 0)
    def _():
        u_ref[0:halo, :] = jnp.zeros((halo, CONV_CH), F32)
        st_ref[...] = jnp.zeros_like(st_ref)

    @pl.when(i > 0)
    def _():
        u_ref[0:halo, :] = u_ref[ts:ts + halo, :]

    xn = _rms(x_ref[...], g_ref[...]).astype(BF16)

    zc = _mm(xn, w_ref[:, PK_CONV:PK_CONV + 2 * CONV_CH])
    u_ref[halo:halo + ts, :] = zc[:, :CONV_CH] * jax.nn.sigmoid(zc[:, CONV_CH:])
    span = us_ref.shape[1]
    for b in range(1, SUBLANES):
        us_ref[b - 1] = u_ref[b:b + span, :]

    zm = _mm(xn, w_ref[:, PK_MOBA:PK_MOBA + 3 * MOBA_WIDTH])
    reps = LANES // cos_ref.shape[1]
    cos = jnp.tile(cos_ref[...], (1, reps))
    sin = jnp.tile(sin_ref[...], (1, reps))
    first_half = (lax.broadcasted_iota(jnp.int32, sin.shape, 1) % MOBA_HEAD_DIM) < (MOBA_HEAD_DIM // 2)
    sa = jnp.where(first_half, -sin, 0.0)
    sb = jnp.where(first_half, 0.0, sin)
    for c in range(MOBA_WIDTH // LANES):
        cols = slice(c * LANES, (c + 1) * LANES)
        mq_ref[:, cols] = (_rope(zm[:, cols], cos, sa, sb) * MOBA_Q_SCALE).astype(BF16)
        mk_ref[:, cols] = _rope(zm[:, MOBA_WIDTH + c * LANES:MOBA_WIDTH + (c + 1) * LANES], cos, sa, sb).astype(BF16)
    mv_ref[...] = zm[:, 2 * MOBA_WIDTH:].astype(BF16)

    zg = _mm(xn, w_ref[:, PK_GLA:PK_GLA + 2 * GLA_KW + 2 * GLA_VW])
    la = _mm(xn, wla_ref[...])
    pre = _mm(la.astype(BF16), wa2_ref[...]) + ba_ref[...]
    dec = (jnp.minimum(pre, 0.0) - jnp.log1p(jnp.exp(-jnp.abs(pre)))) / GLA_TEMP

    def conv():
        first = halo - (CONV_WIDTH - 1)
        for c in range(CONV_CH // LANES):
            cols = slice(c * LANES, (c + 1) * LANES)
            for r in range(0, ts, CONV_ROW_CHUNK):
                acc = jnp.broadcast_to(bdw_ref[:, cols], (CONV_ROW_CHUNK, LANES))
                for w in range(CONV_WIDTH):
                    a, b = divmod(first + w, SUBLANES)
                    lo = r + a * SUBLANES
                    tap = (u_ref[lo:lo + CONV_ROW_CHUNK, cols] if b == 0
                           else us_ref[b - 1, lo:lo + CONV_ROW_CHUNK, cols])
                    acc = acc + wdw_ref[w:w + 1, cols] * tap
                y_ref[r:r + CONV_ROW_CHUNK, cols] = acc
                yield

    masks = _gla_masks()
    gn = gn_ref[...]
    pair_dv = (LANES // GLA_DK) * GLA_DV
    n_pairs, n_sub = GLA_KW // LANES, ts // GLA_TILE
    st_boxes = [[st_ref[p]] for p in range(n_pairs)]
    glas = []
    for t in range(n_sub):
        rows = slice(t * GLA_TILE, (t + 1) * GLA_TILE)
        for p in range(n_pairs):
            kl = slice(p * LANES, (p + 1) * LANES)
            vl = slice(2 * GLA_KW + p * pair_dv, 2 * GLA_KW + (p + 1) * pair_dv)
            rl = slice(2 * GLA_KW + GLA_VW + p * pair_dv, 2 * GLA_KW + GLA_VW + (p + 1) * pair_dv)
            glas.append(_gla_tile(zg[rows, kl], zg[rows, GLA_KW + p * LANES:GLA_KW + (p + 1) * LANES],
                                  zg[rows, vl].astype(BF16), zg[rows, rl], dec[rows, kl], gn, st_boxes[p], masks))
    res = _drive(*glas, conv())
    for t in range(n_sub):
        for p in range(n_pairs):
            for h, o in enumerate(res[t * n_pairs + p]):
                ob_ref[t * GLA_TILE:(t + 1) * GLA_TILE, p * pair_dv + h * GLA_DV:p * pair_dv + (h + 1) * GLA_DV] = o.astype(BF16)
    for p in range(n_pairs):
        st_ref[p] = st_boxes[p][0]

    y = y_ref[...]
    mu = jnp.mean(y, axis=-1, keepdims=True)
    yc = y - mu
    var = jnp.mean(yc * yc, axis=-1, keepdims=True)
    yn = yc * lax.rsqrt(var + EPS) * gln_ref[...] + bln_ref[...]
    oc_ref[...] = (yn * jax.nn.sigmoid(yn)).astype(BF16)


def _front(layer, x, g, wp, wa2, ba, cos, sin, wdw, bdw, gln, bln, gn, batch, seq):
    T = x.shape[0]
    nt = seq // FRONT_TILE
    row = lambda n: pl.BlockSpec((FRONT_TILE, n), lambda b, i: (b * nt + i, 0))
    sds = lambda n: jax.ShapeDtypeStruct((T, n), BF16)
    assert PK_MOBA == 0 and PK_GATE % LANES == 0
    return pl.pallas_call(
        _front_kernel,
        grid=(batch, nt),
        in_specs=[row(D_MODEL), _vec_spec(layer, D_MODEL), _layer_spec(layer, D_MODEL, PK_GATE, 0),
                  _layer_spec(layer, D_MODEL, LANES, PK_LA // LANES), _layer_spec(layer, LANES, GLA_KW),
                  _vec_spec(layer, GLA_KW), row(MOBA_HEAD_DIM // 2), row(MOBA_HEAD_DIM // 2),
                  _layer_spec(layer, CONV_WIDTH, CONV_CH),
                  _vec_spec(layer, CONV_CH), _vec_spec(layer, CONV_CH), _vec_spec(layer, CONV_CH),
                  _vec_spec(layer, GLA_DV)],
        out_specs=[row(MOBA_WIDTH), row(MOBA_WIDTH), row(MOBA_WIDTH), row(GLA_VW), row(CONV_CH)],
        out_shape=[sds(MOBA_WIDTH), sds(MOBA_WIDTH), sds(MOBA_WIDTH), sds(GLA_VW), sds(CONV_CH)],
        scratch_shapes=[pltpu.VMEM((FRONT_TILE + CONV_HALO, CONV_CH), F32),
                        pltpu.VMEM((SUBLANES - 1, FRONT_TILE + CONV_HALO - SUBLANES, CONV_CH), F32),
                        pltpu.VMEM((FRONT_TILE, CONV_CH), F32),
                        pltpu.VMEM((GLA_KW // LANES, GLA_DV, LANES), F32)],
        compiler_params=_params("parallel", "arbitrary"),
        name="front",
    )(x, g, wp, wp, wa2, ba, cos, sin, wdw, bdw, gln, bln, gn)


def _moba_attn_kernel(q_ref, k_ref, v_ref, o_ref, kmean_ref, vt_ref, s_ref, *, n_blocks):
    blk = MOBA_BLOCK
    pair = LANES // MOBA_HEAD_DIM
    hd = MOBA_HEAD_DIM
    kmean_ref[...] = jnp.zeros_like(kmean_ref)
    ones_row = lax.broadcasted_iota(jnp.int32, (MOBA_VT_ROWS - hd, blk), 0) == 0
    ones_pad = jnp.where(ones_row, 1.0, 0.0).astype(BF16)
    for j in range(n_blocks):
        kj = k_ref[j * blk:(j + 1) * blk, :].astype(F32)
        kmean_ref[j:j + 1, :] = jnp.mean(kj, axis=0, keepdims=True)
        vt = v_ref[j * blk:(j + 1) * blk, :].astype(F32).T.astype(BF16)
        for h in range(pair):
            vt_ref[j, h, 0:hd, :] = vt[h * hd:(h + 1) * hd, :]
            vt_ref[j, h, hd:, :] = ones_pad

    kpos = lax.broadcasted_iota(jnp.int32, (blk, blk), 0)
    qpos = lax.broadcasted_iota(jnp.int32, (blk, blk), 1)
    causal = kpos <= qpos
    qlane = lax.broadcasted_iota(jnp.int32, (blk, LANES), 1) // hd
    sub = lambda t: t.reshape(blk // SUBLANES, SUBLANES, blk)
    col_max = {}

    def scores(ii):
        q = q_ref[ii * blk:(ii + 1) * blk, :]
        need_gate = ii > MOBA_TOPK
        if need_gate:
            km_hi, km_mid, km_lo = _split3(kmean_ref[...])
            jidx = lax.broadcasted_iota(jnp.int32, (kmean_ref.shape[0], blk), 0)
        for h in range(pair):
            qh = jnp.where(qlane == h, q, jnp.zeros_like(q))
            valid = [None] * ii
            if need_gate:
                gate = _nt(km_hi, qh) + _nt(km_mid, qh) + _nt(km_lo, qh)
                for n in range(ii):
                    gn = gate[n:n + 1, :]
                    beats = ((gate > gn) | ((gate == gn) & (jidx < n))) & (jidx < ii)
                    rank = jnp.sum(jnp.where(beats, 1.0, 0.0), axis=0, keepdims=True)
                    valid[n] = rank < MOBA_TOPK
            m8 = None
            for j in range(ii + 1):
                s = _nt(k_ref[j * blk:(j + 1) * blk, :], qh)
                if j == ii:
                    s = jnp.where(causal, s, NEG)
                elif valid[j] is not None:
                    s = jnp.where(valid[j], s, NEG)
                s_ref[ii % 2, h, j] = s
                t = jnp.max(sub(s), axis=0)
                m8 = t if m8 is None else jnp.maximum(m8, t)
                yield
            col_max[ii, h] = jnp.max(m8, axis=0, keepdims=True)

    def values(ii):
        outs = []
        for h in range(pair):
            acc = None
            for j in range(ii + 1):
                p = jnp.exp2(s_ref[ii % 2, h, j] - col_max[ii, h])
                t = _mm(vt_ref[j, h], p.astype(BF16))
                acc = t if acc is None else acc + t
                yield
            outs.append(acc[0:hd, :] / acc[hd:hd + 1, :])
        o_ref[ii * blk:(ii + 1) * blk, :] = jnp.concatenate(outs, axis=0).T.astype(BF16)

    for _ in scores(0):
        pass
    for ii in range(n_blocks):
        live = [values(ii)]
        if ii + 1 < n_blocks:
            live.insert(0, scores(ii + 1))
        while live:
            for gen in list(live):
                if next(gen, StopIteration) is StopIteration:
                    live.remove(gen)


def _moba_attn(q, k, v, batch, seq):
    n_blocks = seq // MOBA_BLOCK
    n_pairs = MOBA_WIDTH // LANES
    pair = LANES // MOBA_HEAD_DIM
    spec = pl.BlockSpec((seq, LANES), lambda b, p: (b, p))
    gate_rows = 2 * SUBLANES
    assert n_blocks <= gate_rows
    return pl.pallas_call(
        functools.partial(_moba_attn_kernel, n_blocks=n_blocks),
        grid=(batch, n_pairs),
        in_specs=[spec, spec, spec],
        out_specs=spec,
        out_shape=jax.ShapeDtypeStruct(q.shape, BF16),
        scratch_shapes=[
            pltpu.VMEM((gate_rows, LANES), F32),
            pltpu.VMEM((n_blocks, pair, MOBA_VT_ROWS, MOBA_BLOCK), BF16),
            pltpu.VMEM((2, pair, n_blocks, MOBA_BLOCK, MOBA_BLOCK), F32),
        ],
        compiler_params=_params("parallel", "parallel"),
        name="moba_attn",
    )(q, k, v)


def _combine_kernel(x_ref, g_ref, wga_ref, wgb_ref, wgc_ref, bg_ref, a_ref, b_ref, c_ref, wa_ref, wb_ref,
                    wc_ref, wo_ref, gp_ref, wu32_ref, wd32_ref, o_ref, wu_ref, wd_ref):
    wu_ref[...] = wu32_ref[...].astype(BF16)
    wd_ref[...] = wd32_ref[...].astype(BF16)
    x = x_ref[...]
    xn = _rms(x, g_ref[...]).astype(BF16)
    mix = None
    branches = ((wga_ref, a_ref, wa_ref), (wgb_ref, b_ref, wb_ref), (wgc_ref, c_ref, wc_ref))
    for n, (wg_ref, br_ref, wbr_ref) in enumerate(branches):
        cols = slice(n * D_MODEL, (n + 1) * D_MODEL)
        gate = jax.nn.sigmoid(_mm(xn, wg_ref[...]) + bg_ref[:, cols])
        term = gate * _mm(br_ref[...], wbr_ref[...])
        mix = term if mix is None else mix + term
    y = _mm(mix.astype(BF16), wo_ref[...])
    o_ref[...] = x + _rms(y, gp_ref[...])


def _combine(layer, x, g, wp, bg, a, b, c, wa, wb, wc, wo, gp, wu32, wd32):
    T = x.shape[0]
    slab = D_FF // (T // ROW_TILE)
    assert slab % LANES == 0
    row = lambda n: pl.BlockSpec((ROW_TILE, n), lambda i: (i, 0))
    gate_w = lambda n: _layer_spec(layer, D_MODEL, D_MODEL, PK_GATE // D_MODEL + n)
    return pl.pallas_call(
        _combine_kernel,
        grid=(T // ROW_TILE,),
        in_specs=[row(D_MODEL), _vec_spec(layer, D_MODEL), gate_w(0), gate_w(1), gate_w(2),
                  _vec_spec(layer, N_BRANCH * D_MODEL),
                  row(MOBA_WIDTH), row(GLA_VW), row(CONV_CH),
                  _layer_spec(layer, MOBA_WIDTH, D_MODEL), _layer_spec(layer, GLA_VW, D_MODEL),
                  _layer_spec(layer, CONV_CH, D_MODEL), _layer_spec(layer, D_MODEL, D_MODEL),
                  _vec_spec(layer, D_MODEL),
                  pl.BlockSpec((None, D_MODEL, slab), lambda i: (layer, 0, i)),
                  pl.BlockSpec((None, slab, D_MODEL), lambda i: (layer, i, 0))],
        out_specs=[row(D_MODEL), pl.BlockSpec((D_MODEL, slab), lambda i: (0, i)),
                   pl.BlockSpec((slab, D_MODEL), lambda i: (i, 0))],
        out_shape=[jax.ShapeDtypeStruct((T, D_MODEL), F32), jax.ShapeDtypeStruct((D_MODEL, D_FF), BF16),
                   jax.ShapeDtypeStruct((D_FF, D_MODEL), BF16)],
        compiler_params=_params("parallel"),
        name="combine",
    )(x, g, wp, wp, wp, bg, a, b, c, wa, wb, wc, wo, gp, wu32, wd32)


def _mlp_kernel(x_ref, g_ref, wu_ref, wd_ref, gp_ref, o_ref):
    x = x_ref[...]
    xn = _rms(x, g_ref[...]).astype(BF16)
    acc = None
    for c in range(D_FF // D_MODEL):
        cols = slice(c * D_MODEL, (c + 1) * D_MODEL)
        u = jnp.maximum(_mm(xn, wu_ref[:, cols]), 0.0)
        t = _mm((u * u).astype(BF16), wd_ref[cols, :])
        acc = t if acc is None else acc + t
    o_ref[...] = x + _rms(acc, gp_ref[...])


def _mlp(layer, x, g, wu, wd, gp):
    T = x.shape[0]
    row = pl.BlockSpec((ROW_TILE, D_MODEL), lambda i: (i, 0))
    return pl.pallas_call(
        _mlp_kernel,
        grid=(T // ROW_TILE,),
        in_specs=[row, _vec_spec(layer, D_MODEL), _const_spec((D_MODEL, D_FF)), _const_spec((D_FF, D_MODEL)),
                  _vec_spec(layer, D_MODEL)],
        out_specs=row,
        out_shape=jax.ShapeDtypeStruct((T, D_MODEL), F32),
        compiler_params=_params("parallel"),
        name="mlp",
    )(x, g, wu, wd, gp)


def kernel(x, positions, g_mix_pre, w_in, b_gate, w_gla_a2, b_gla_a, g_gla_norm, w_dw, b_dw, g_conv_ln,
           b_conv_ln, w_moba_o, w_gla_o, w_conv_o, w_mix_out, g_mix_post, g_mlp_pre, w_mlp_up, w_mlp_down,
           g_mlp_post):
    batch, seq, _ = x.shape
    depth = w_in.shape[0]
    assert seq % MOBA_BLOCK == 0 and seq % FRONT_TILE == 0 and (batch * seq) % ROW_TILE == 0
    assert FRONT_TILE % GLA_TILE == 0 and GLA_TILE % GLA_CHUNK == 0
    h = x.reshape(batch * seq, D_MODEL)
    cos, sin = _rope_tables(positions)

    o_la = 3 * MOBA_WIDTH + 2 * GLA_KW + GLA_VW
    o_lr = o_la + GLA_RANK
    width = PK_LA + LANES
    tail = lambda a, left: jnp.pad(a, ((0, 0), (0, 0), (left, width - left - a.shape[-1])))
    col = lax.broadcasted_iota(jnp.int32, (1, 1, width), 2)
    wp = jnp.where(col < o_la, tail(w_in, 0),
                   jnp.where(col < PK_LA, tail(w_in[:, :, GLA_RANK:], 0), tail(w_in[:, :, o_la:o_lr], PK_LA)))
    wp = wp.astype(BF16)
    wa2 = jnp.pad(w_gla_a2, ((0, 0), (0, LANES - GLA_RANK), (0, 0))).astype(BF16)
    vec = lambda v: v[:, None, :]
    g_pre, g_post, g_mpre, g_mpost = vec(g_mix_pre), vec(g_mix_post), vec(g_mlp_pre), vec(g_mlp_post)
    bg, ba, gn = vec(b_gate), vec(b_gla_a), vec(g_gla_norm)
    bdw, gln, bln = vec(b_dw), vec(g_conv_ln), vec(b_conv_ln)
    wa, wb, wc, wo = (w.astype(BF16) for w in (w_moba_o, w_gla_o, w_conv_o, w_mix_out))

    for l in range(depth):
        mq, mk, mv, ob, oc = _front(l, h, g_pre, wp, wa2, ba, cos, sin, w_dw, bdw, gln, bln, gn, batch, seq)
        oa = _moba_attn(mq, mk, mv, batch, seq)
        h, wu, wd = _combine(l, h, g_pre, wp, bg, oa, ob, oc, wa, wb, wc, wo, g_post, w_mlp_up, w_mlp_down)
        h = _mlp(l, h, g_mpre, wu, wd, g_mpost)
    return h.reshape(batch, seq, D_MODEL)
```

```python
import functools

import jax
import jax.numpy as jnp
from jax import lax
from jax.experimental import pallas as pl
from jax.experimental.pallas import tpu as pltpu

D_MODEL = 1024
MOBA_HEADS = 8
MOBA_HEAD_DIM = 64
MOBA_WIDTH = MOBA_HEADS * MOBA_HEAD_DIM
MOBA_BLOCK = 256
MOBA_TOPK = 3
ROPE_THETA = 10000.0
GLA_HEADS = 4
GLA_DK = 64
GLA_DV = 128
GLA_KW = GLA_HEADS * GLA_DK
GLA_VW = GLA_HEADS * GLA_DV
GLA_RANK = 16
GLA_TEMP = 16.0
GLA_CHUNK = 64
CONV_CH = 512
CONV_WIDTH = 31
D_FF = 4 * D_MODEL
N_BRANCH = 3
EPS = 1e-6
NEG = -1e30
LOG2E = 1.4426950408889634
MOBA_Q_SCALE = MOBA_HEAD_DIM ** -0.5 * LOG2E

LANES = 128
SUBLANES = 8
VMEM_LIMIT_BYTES = 56 * 1024 * 1024
MOBA_VT_ROWS = MOBA_HEAD_DIM + 2 * SUBLANES

ROW_TILE = 512
FRONT_TILE = 1024
GLA_TILE = 256
CONV_HALO = 32
CONV_ROW_CHUNK = 64

PK_MOBA = 0
PK_GLA = 3 * MOBA_WIDTH
PK_CONV = PK_GLA + 2 * GLA_KW + 2 * GLA_VW
PK_GATE = PK_CONV + 2 * CONV_CH
PK_LA = PK_GATE + N_BRANCH * D_MODEL

F32 = jnp.float32
BF16 = jnp.bfloat16


def _nt(a, b):
    return lax.dot_general(a, b, (((1,), (1,)), ((), ())), preferred_element_type=F32)


def _tn(a, b):
    return lax.dot_general(a, b, (((0,), (0,)), ((), ())), preferred_element_type=F32)


def _mm(a, b):
    return jnp.dot(a, b, preferred_element_type=F32)


def _rms(x, g):
    return x * lax.rsqrt(jnp.mean(x * x, axis=-1, keepdims=True) + EPS) * g


def _split3(x):
    hi = x.astype(BF16)
    r1 = x - hi.astype(F32)
    mid = r1.astype(BF16)
    lo = (r1 - mid.astype(F32)).astype(BF16)
    return hi, mid, lo


def _params(*sem):
    return pltpu.CompilerParams(dimension_semantics=sem, vmem_limit_bytes=VMEM_LIMIT_BYTES)


def _const_spec(shape):
    nd = len(shape)
    return pl.BlockSpec(shape, lambda *_: (0,) * nd)


def _layer_spec(layer, rows, cols, col_block=0):
    return pl.BlockSpec((None, rows, cols), lambda *_: (layer, 0, col_block))


def _vec_spec(layer, n):
    return _layer_spec(layer, 1, n)


def _rope_table_kernel(pos_ref, invf_ref, cos_ref, sin_ref):
    ang = pos_ref[...].astype(F32) * invf_ref[...]
    cos_ref[...] = jnp.cos(ang)
    sin_ref[...] = jnp.sin(ang)


def _rope_tables(positions):
    T = positions.size
    half = MOBA_HEAD_DIM // 2
    per_row = LANES // half
    inv_freq = ROPE_THETA ** (-jnp.arange(half, dtype=F32) / half)
    invf = jnp.tile(inv_freq, per_row)[None, :]
    pos = jnp.repeat(positions.reshape(T // per_row, per_row), half, axis=1)
    rows = T // per_row
    tile = min(ROW_TILE, rows)
    assert rows % tile == 0
    tab = jax.ShapeDtypeStruct((rows, LANES), F32)
    row = pl.BlockSpec((tile, LANES), lambda i: (i, 0))
    cos, sin = pl.pallas_call(
        _rope_table_kernel,
        grid=(rows // tile,),
        in_specs=[row, _const_spec((1, LANES))],
        out_specs=[row, row],
        out_shape=[tab, tab],
        compiler_params=_params("parallel"),
        name="rope_tables",
    )(pos, invf)
    return cos.reshape(T, half), sin.reshape(T, half)


def _rope(s, cos, sa, sb):
    return s * cos + pltpu.roll(s, LANES - 32, 1) * sa + pltpu.roll(s, 32, 1) * sb


def _gla_masks():
    ch, sc = GLA_CHUNK, GLA_TILE
    ri = lax.broadcasted_iota(jnp.int32, (sc, sc), 0)
    ci = lax.broadcasted_iota(jnp.int32, (sc, sc), 1)
    lower = (ri >= ci) & (ri // ch == ci // ch)
    tril = jnp.where(lower, 1.0, 0.0).astype(BF16)
    lane_head = lax.broadcasted_iota(jnp.int32, (sc, LANES), 1) // GLA_DK
    st_lane_head = lax.broadcasted_iota(jnp.int32, (GLA_DV, LANES), 1) // GLA_DK
    return lower, tril, lane_head, st_lane_head


def _gla_tile(q, k, v, r, d, gn, st_box, masks):
    ch, sc = GLA_CHUNK, GLA_TILE
    nc = sc // ch
    pair = LANES // GLA_DK
    lower, tril, lane_head, st_lane_head = masks
    d_hi, d_mid, d_lo = _split3(d)
    G = _mm(tril, d_hi) + _mm(tril, d_mid) + _mm(tril, d_lo)
    yield
    per_chunk = lambda row: jnp.concatenate(
        [jnp.broadcast_to(G[c * ch + row:c * ch + row + 1, :], (ch, LANES)) for c in range(nc)], axis=0)
    g_last = [G[(c + 1) * ch - 1:(c + 1) * ch, :] for c in range(nc)]
    g_last_rows = per_chunk(ch - 1)
    g_mid_rows = per_chunk(ch // 2 - 1)
    scale = GLA_DK ** -0.5
    qa = (q * jnp.exp(G - g_mid_rows) * scale).astype(BF16)
    kt = (k * jnp.exp(g_mid_rows - G)).astype(BF16)
    qt = (q * jnp.exp(G) * scale).astype(BF16)
    kd = (k * jnp.exp(g_last_rows - G)).astype(BF16)
    yield
    qh, vh, o_intra, upd = [], [], [], []
    for h in range(pair):
        qh.append(jnp.where(lane_head == h, qt, jnp.zeros_like(qt)))
        qah = jnp.where(lane_head == h, qa, jnp.zeros_like(qa))
        a = jnp.where(lower, _nt(qah, kt), 0.0).astype(BF16)
        vh.append(v[:, h * GLA_DV:(h + 1) * GLA_DV])
        o_intra.append(_mm(a, vh[h]))
        upd.append([_tn(vh[h][c * ch:(c + 1) * ch], kd[c * ch:(c + 1) * ch]) for c in range(nc)])
        yield
    o_inter = [[] for _ in range(pair)]
    st = st_box[0]
    for c in range(nc):
        st_b = st.astype(BF16)
        u = upd[0][c]
        for h in range(pair):
            o_inter[h].append(_nt(qh[h][c * ch:(c + 1) * ch], st_b))
            if h:
                u = jnp.where(st_lane_head == h, upd[h][c], u)
        st = st * jnp.exp(g_last[c]) + u
    st_box[0] = st
    yield
    outs = []
    for h in range(pair):
        o = _rms(o_intra[h] + jnp.concatenate(o_inter[h], axis=0), gn)
        rh = r[:, h * GLA_DV:(h + 1) * GLA_DV]
        outs.append(o * (rh * jax.nn.sigmoid(rh)))
        yield
    return outs


def _drive(*gens):
    results = [None] * len(gens)
    live = list(enumerate(gens))
    while live:
        for item in list(live):
            n, gen = item
            try:
                next(gen)
            except StopIteration as stop:
                results[n] = stop.value
                live.remove(item)
    return results


def _front_kernel(x_ref, g_ref, w_ref, wla_ref, wa2_ref, ba_ref, cos_ref, sin_ref,
                  wdw_ref, bdw_ref, gln_ref, bln_ref, gn_ref,
                  mq_ref, mk_ref, mv_ref, ob_ref, oc_ref, u_ref, us_ref, y_ref, st_ref):
    ts, halo = FRONT_TILE, CONV_HALO
    i = pl.program_id(1)

    @pl.when(i == 0)
    def _():
        u_ref[0:halo, :] = jnp.zeros((halo, CONV_CH), F32)
        st_ref[...] = jnp.zeros_like(st_ref)

    @pl.when(i > 0)
    def _():
        u_ref[0:halo, :] = u_ref[ts:ts + halo, :]

    xn = _rms(x_ref[...], g_ref[...]).astype(BF16)

    zc = _mm(xn, w_ref[:, PK_CONV:PK_CONV + 2 * CONV_CH])
    u_ref[halo:halo + ts, :] = zc[:, :CONV_CH] * jax.nn.sigmoid(zc[:, CONV_CH:])
    span = us_ref.shape[1]
    for b in range(1, SUBLANES):
        us_ref[b - 1] = u_ref[b:b + span, :]

    zm = _mm(xn, w_ref[:, PK_MOBA:PK_MOBA + 3 * MOBA_WIDTH])
    reps = LANES // cos_ref.shape[1]
    cos = jnp.tile(cos_ref[...], (1, reps))
    sin = jnp.tile(sin_ref[...], (1, reps))
    first_half = (lax.broadcasted_iota(jnp.int32, sin.shape, 1) % MOBA_HEAD_DIM) < (MOBA_HEAD_DIM // 2)
    sa = jnp.where(first_half, -sin, 0.0)
    sb = jnp.where(first_half, 0.0, sin)
    for c in range(MOBA_WIDTH // LANES):
        cols = slice(c * LANES, (c + 1) * LANES)
        mq_ref[:, cols] = (_rope(zm[:, cols], cos, sa, sb) * MOBA_Q_SCALE).astype(BF16)
        mk_ref[:, cols] = _rope(zm[:, MOBA_WIDTH + c * LANES:MOBA_WIDTH + (c + 1) * LANES], cos, sa, sb).astype(BF16)
    mv_ref[...] = zm[:, 2 * MOBA_WIDTH:].astype(BF16)

    zg = _mm(xn, w_ref[:, PK_GLA:PK_GLA + 2 * GLA_KW + 2 * GLA_VW])
    la = _mm(xn, wla_ref[...])
    pre = _mm(la.astype(BF16), wa2_ref[...]) + ba_ref[...]
    dec = (jnp.minimum(pre, 0.0) - jnp.log1p(jnp.exp(-jnp.abs(pre)))) / GLA_TEMP

    def conv():
        first = halo - (CONV_WIDTH - 1)
        for c in range(CONV_CH // LANES):
            cols = slice(c * LANES, (c + 1) * LANES)
            for r in range(0, ts, CONV_ROW_CHUNK):
                acc = jnp.broadcast_to(bdw_ref[:, cols], (CONV_ROW_CHUNK, LANES))
                for w in range(CONV_WIDTH):
                    a, b = divmod(first + w, SUBLANES)
                    lo = r + a * SUBLANES
                    tap = (u_ref[lo:lo + CONV_ROW_CHUNK, cols] if b == 0
                           else us_ref[b - 1, lo:lo + CONV_ROW_CHUNK, cols])
                    acc = acc + wdw_ref[w:w + 1, cols] * tap
                y_ref[r:r + CONV_ROW_CHUNK, cols] = acc
                yield

    masks = _gla_masks()
    gn = gn_ref[...]
    pair_dv = (LANES // GLA_DK) * GLA_DV
    n_pairs, n_sub = GLA_KW // LANES, ts // GLA_TILE
    st_boxes = [[st_ref[p]] for p in range(n_pairs)]
    glas = []
    for t in range(n_sub):
        rows = slice(t * GLA_TILE, (t + 1) * GLA_TILE)
        for p in range(n_pairs):
            kl = slice(p * LANES, (p + 1) * LANES)
            vl = slice(2 * GLA_KW + p * pair_dv, 2 * GLA_KW + (p + 1) * pair_dv)
            rl = slice(2 * GLA_KW + GLA_VW + p * pair_dv, 2 * GLA_KW + GLA_VW + (p + 1) * pair_dv)
            glas.append(_gla_tile(zg[rows, kl], zg[rows, GLA_KW + p * LANES:GLA_KW + (p + 1) * LANES],
                                  zg[rows, vl].astype(BF16), zg[rows, rl], dec[rows, kl], gn, st_boxes[p], masks))
    res = _drive(*glas, conv())
    for t in range(n_sub):
        for p in range(n_pairs):
            for h, o in enumerate(res[t * n_pairs + p]):
                ob_ref[t * GLA_TILE:(t + 1) * GLA_TILE, p * pair_dv + h * GLA_DV:p * pair_dv + (h + 1) * GLA_DV] = o.astype(BF16)
    for p in range(n_pairs):
        st_ref[p] = st_boxes[p][0]

    y = y_ref[...]
    mu = jnp.mean(y, axis=-1, keepdims=True)
    yc = y - mu
    var = jnp.mean(yc * yc, axis=-1, keepdims=True)
    yn = yc * lax.rsqrt(var + EPS) * gln_ref[...] + bln_ref[...]
    oc_ref[...] = (yn * jax.nn.sigmoid(yn)).astype(BF16)


def _front(layer, x, g, wp, wa2, ba, cos, sin, wdw, bdw, gln, bln, gn, batch, seq):
    T = x.shape[0]
    nt = seq // FRONT_TILE
    row = lambda n: pl.BlockSpec((FRONT_TILE, n), lambda b, i: (b * nt + i, 0))
    sds = lambda n: jax.ShapeDtypeStruct((T, n), BF16)
    assert PK_MOBA == 0 and PK_GATE % LANES == 0
    return pl.pallas_call(
        _front_kernel,
        grid=(batch, nt),
        in_specs=[row(D_MODEL), _vec_spec(layer, D_MODEL), _layer_spec(layer, D_MODEL, PK_GATE, 0),
                  _layer_spec(layer, D_MODEL, LANES, PK_LA // LANES), _layer_spec(layer, LANES, GLA_KW),
                  _vec_spec(layer, GLA_KW), row(MOBA_HEAD_DIM // 2), row(MOBA_HEAD_DIM // 2),
                  _layer_spec(layer, CONV_WIDTH, CONV_CH),
                  _vec_spec(layer, CONV_CH), _vec_spec(layer, CONV_CH), _vec_spec(layer, CONV_CH),
                  _vec_spec(layer, GLA_DV)],
        out_specs=[row(MOBA_WIDTH), row(MOBA_WIDTH), row(MOBA_WIDTH), row(GLA_VW), row(CONV_CH)],
        out_shape=[sds(MOBA_WIDTH), sds(MOBA_WIDTH), sds(MOBA_WIDTH), sds(GLA_VW), sds(CONV_CH)],
        scratch_shapes=[pltpu.VMEM((FRONT_TILE + CONV_HALO, CONV_CH), F32),
                        pltpu.VMEM((SUBLANES - 1, FRONT_TILE + CONV_HALO - SUBLANES, CONV_CH), F32),
                        pltpu.VMEM((FRONT_TILE, CONV_CH), F32),
                        pltpu.VMEM((GLA_KW // LANES, GLA_DV, LANES), F32)],
        compiler_params=_params("parallel", "arbitrary"),
        name="front",
    )(x, g, wp, wp, wa2, ba, cos, sin, wdw, bdw, gln, bln, gn)


def _moba_attn_kernel(q_ref, k_ref, v_ref, o_ref, kmean_ref, vt_ref, s_ref, *, n_blocks):
    blk = MOBA_BLOCK
    pair = LANES // MOBA_HEAD_DIM
    hd = MOBA_HEAD_DIM
    kmean_ref[...] = jnp.zeros_like(kmean_ref)
    ones_row = lax.broadcasted_iota(jnp.int32, (MOBA_VT_ROWS - hd, blk), 0) == 0
    ones_pad = jnp.where(ones_row, 1.0, 0.0).astype(BF16)
    for j in range(n_blocks):
        kj = k_ref[j * blk:(j + 1) * blk, :].astype(F32)
        kmean_ref[j:j + 1, :] = jnp.mean(kj, axis=0, keepdims=True)
        vt = v_ref[j * blk:(j + 1) * blk, :].astype(F32).T.astype(BF16)
        for h in range(pair):
            vt_ref[j, h, 0:hd, :] = vt[h * hd:(h + 1) * hd, :]
            vt_ref[j, h, hd:, :] = ones_pad

    kpos = lax.broadcasted_iota(jnp.int32, (blk, blk), 0)
    qpos = lax.broadcasted_iota(jnp.int32, (blk, blk), 1)
    causal = kpos <= qpos
    qlane = lax.broadcasted_iota(jnp.int32, (blk, LANES), 1) // hd
    sub = lambda t: t.reshape(blk // SUBLANES, SUBLANES, blk)
    col_max = {}

    def scores(ii):
        q = q_ref[ii * blk:(ii + 1) * blk, :]
        need_gate = ii > MOBA_TOPK
        if need_gate:
            km_hi, km_mid, km_lo = _split3(kmean_ref[...])
            jidx = lax.broadcasted_iota(jnp.int32, (kmean_ref.shape[0], blk), 0)
        for h in range(pair):
            qh = jnp.where(qlane == h, q, jnp.zeros_like(q))
            valid = [None] * ii
            if need_gate:
                gate = _nt(km_hi, qh) + _nt(km_mid, qh) + _nt(km_lo, qh)
                for n in range(ii):
                    gn = gate[n:n + 1, :]
                    beats = ((gate > gn) | ((gate == gn) & (jidx < n))) & (jidx < ii)
                    rank = jnp.sum(jnp.where(beats, 1.0, 0.0), axis=0, keepdims=True)
                    valid[n] = rank < MOBA_TOPK
            m8 = None
            for j in range(ii + 1):
                s = _nt(k_ref[j * blk:(j + 1) * blk, :], qh)
                if j == ii:
                    s = jnp.where(causal, s, NEG)
                elif valid[j] is not None:
                    s = jnp.where(valid[j], s, NEG)
                s_ref[ii % 2, h, j] = s
                t = jnp.max(sub(s), axis=0)
                m8 = t if m8 is None else jnp.maximum(m8, t)
                yield
            col_max[ii, h] = jnp.max(m8, axis=0, keepdims=True)

    def values(ii):
        outs = []
        for h in range(pair):
            acc = None
            for j in range(ii + 1):
                p = jnp.exp2(s_ref[ii % 2, h, j] - col_max[ii, h])
                t = _mm(vt_ref[j, h], p.astype(BF16))
                acc = t if acc is None else acc + t
                yield
            outs.append(acc[0:hd, :] / acc[hd:hd + 1, :])
        o_ref[ii * blk:(ii + 1) * blk, :] = jnp.concatenate(outs, axis=0).T.astype(BF16)

    for _ in scores(0):
        pass
    for ii in range(n_blocks):
        live = [values(ii)]
        if ii + 1 < n_blocks:
            live.insert(0, scores(ii + 1))
        while live:
            for gen in list(live):
                if next(gen, StopIteration) is StopIteration:
                    live.remove(gen)


def _moba_attn(q, k, v, batch, seq):
    n_blocks = seq // MOBA_BLOCK
    n_pairs = MOBA_WIDTH // LANES
    pair = LANES // MOBA_HEAD_DIM
    spec = pl.BlockSpec((seq, LANES), lambda b, p: (b, p))
    gate_rows = 2 * SUBLANES
    assert n_blocks <= gate_rows
    return pl.pallas_call(
        functools.partial(_moba_attn_kernel, n_blocks=n_blocks),
        grid=(batch, n_pairs),
        in_specs=[spec, spec, spec],
        out_specs=spec,
        out_shape=jax.ShapeDtypeStruct(q.shape, BF16),
        scratch_shapes=[
            pltpu.VMEM((gate_rows, LANES), F32),
            pltpu.VMEM((n_blocks, pair, MOBA_VT_ROWS, MOBA_BLOCK), BF16),
            pltpu.VMEM((2, pair, n_blocks, MOBA_BLOCK, MOBA_BLOCK), F32),
        ],
        compiler_params=_params("parallel", "parallel"),
        name="moba_attn",
    )(q, k, v)


def _combine_kernel(x_ref, g_ref, wga_ref, wgb_ref, wgc_ref, bg_ref, a_ref, b_ref, c_ref, wa_ref, wb_ref,
                    wc_ref, wo_ref, gp_ref, wu32_ref, wd32_ref, o_ref, wu_ref, wd_ref):
    wu_ref[...] = wu32_ref[...].astype(BF16)
    wd_ref[...] = wd32_ref[...].astype(BF16)
    x = x_ref[...]
    xn = _rms(x, g_ref[...]).astype(BF16)
    mix = None
    branches = ((wga_ref, a_ref, wa_ref), (wgb_ref, b_ref, wb_ref), (wgc_ref, c_ref, wc_ref))
    for n, (wg_ref, br_ref, wbr_ref) in enumerate(branches):
        cols = slice(n * D_MODEL, (n + 1) * D_MODEL)
        gate = jax.nn.sigmoid(_mm(xn, wg_ref[...]) + bg_ref[:, cols])
        term = gate * _mm(br_ref[...], wbr_ref[...])
        mix = term if mix is None else mix + term
    y = _mm(mix.astype(BF16), wo_ref[...])
    o_ref[...] = x + _rms(y, gp_ref[...])


def _combine(layer, x, g, wp, bg, a, b, c, wa, wb, wc, wo, gp, wu32, wd32):
    T = x.shape[0]
    slab = D_FF // (T // ROW_TILE)
    assert slab % LANES == 0
    row = lambda n: pl.BlockSpec((ROW_TILE, n), lambda i: (i, 0))
    gate_w = lambda n: _layer_spec(layer, D_MODEL, D_MODEL, PK_GATE // D_MODEL + n)
    return pl.pallas_call(
        _combine_kernel,
        grid=(T // ROW_TILE,),
        in_specs=[row(D_MODEL), _vec_spec(layer, D_MODEL), gate_w(0), gate_w(1), gate_w(2),
                  _vec_spec(layer, N_BRANCH * D_MODEL),
                  row(MOBA_WIDTH), row(GLA_VW), row(CONV_CH),
                  _layer_spec(layer, MOBA_WIDTH, D_MODEL), _layer_spec(layer, GLA_VW, D_MODEL),
                  _layer_spec(layer, CONV_CH, D_MODEL), _layer_spec(layer, D_MODEL, D_MODEL),
                  _vec_spec(layer, D_MODEL),
                  pl.BlockSpec((None, D_MODEL, slab), lambda i: (layer, 0, i)),
                  pl.BlockSpec((None, slab, D_MODEL), lambda i: (layer, i, 0))],
        out_specs=[row(D_MODEL), pl.BlockSpec((D_MODEL, slab), lambda i: (0, i)),
                   pl.BlockSpec((slab, D_MODEL), lambda i: (i, 0))],
        out_shape=[jax.ShapeDtypeStruct((T, D_MODEL), F32), jax.ShapeDtypeStruct((D_MODEL, D_FF), BF16),
                   jax.ShapeDtypeStruct((D_FF, D_MODEL), BF16)],
        compiler_params=_params("parallel"),
        name="combine",
    )(x, g, wp, wp, wp, bg, a, b, c, wa, wb, wc, wo, gp, wu32, wd32)


def _mlp_kernel(x_ref, g_ref, wu_ref, wd_ref, gp_ref, o_ref):
    x = x_ref[...]
    xn = _rms(x, g_ref[...]).astype(BF16)
    acc = None
    for c in range(D_FF // D_MODEL):
        cols = slice(c * D_MODEL, (c + 1) * D_MODEL)
        u = jnp.maximum(_mm(xn, wu_ref[:, cols]), 0.0)
        t = _mm((u * u).astype(BF16), wd_ref[cols, :])
        acc = t if acc is None else acc + t
    o_ref[...] = x + _rms(acc, gp_ref[...])


def _mlp(layer, x, g, wu, wd, gp):
    T = x.shape[0]
    row = pl.BlockSpec((ROW_TILE, D_MODEL), lambda i: (i, 0))
    return pl.pallas_call(
        _mlp_kernel,
        grid=(T // ROW_TILE,),
        in_specs=[row, _vec_spec(layer, D_MODEL), _const_spec((D_MODEL, D_FF)), _const_spec((D_FF, D_MODEL)),
                  _vec_spec(layer, D_MODEL)],
        out_specs=row,
        out_shape=jax.ShapeDtypeStruct((T, D_MODEL), F32),
        compiler_params=_params("parallel"),
        name="mlp",
    )(x, g, wu, wd, gp)


def kernel(x, positions, g_mix_pre, w_in, b_gate, w_gla_a2, b_gla_a, g_gla_norm, w_dw, b_dw, g_conv_ln,
           b_conv_ln, w_moba_o, w_gla_o, w_conv_o, w_mix_out, g_mix_post, g_mlp_pre, w_mlp_up, w_mlp_down,
           g_mlp_post):
    batch, seq, _ = x.shape
    depth = w_in.shape[0]
    assert seq % MOBA_BLOCK == 0 and seq % FRONT_TILE == 0 and (batch * seq) % ROW_TILE == 0
    assert FRONT_TILE % GLA_TILE == 0 and GLA_TILE % GLA_CHUNK == 0
    h = x.reshape(batch * seq, D_MODEL)
    cos, sin = _rope_tables(positions)

    o_la = 3 * MOBA_WIDTH + 2 * GLA_KW + GLA_VW
    o_lr = o_la + GLA_RANK
    pad_la = jnp.pad(w_in[:, :, o_la:o_lr], ((0, 0), (0, 0), (0, LANES - GLA_RANK)))
    wp = jnp.concatenate([w_in[:, :, :o_la], w_in[:, :, o_lr:], pad_la], axis=-1).astype(BF16)
    assert wp.shape[-1] == PK_LA + LANES
    wa2 = jnp.pad(w_gla_a2, ((0, 0), (0, LANES - GLA_RANK), (0, 0))).astype(BF16)
    vec = lambda v: v[:, None, :]
    g_pre, g_post, g_mpre, g_mpost = vec(g_mix_pre), vec(g_mix_post), vec(g_mlp_pre), vec(g_mlp_post)
    bg, ba, gn = vec(b_gate), vec(b_gla_a), vec(g_gla_norm)
    bdw, gln, bln = vec(b_dw), vec(g_conv_ln), vec(b_conv_ln)
    wa, wb, wc, wo = (w.astype(BF16) for w in (w_moba_o, w_gla_o, w_conv_o, w_mix_out))

    for l in range(depth):
        mq, mk, mv, ob, oc = _front(l, h, g_pre, wp, wa2, ba, cos, sin, w_dw, bdw, gln, bln, gn, batch, seq)
        oa = _moba_attn(mq, mk, mv, batch, seq)
        h, wu, wd = _combine(l, h, g_pre, wp, bg, oa, ob, oc, wa, wb, wc, wo, g_post, w_mlp_up, w_mlp_down)
        h = _mlp(l, h, g_mpre, wu, wd, g_mpost)
    return h.reshape(batch, seq, D_MODEL)
```

```python
import functools

import jax
import jax.numpy as jnp
from jax import lax
from jax.experimental import pallas as pl
from jax.experimental.pallas import tpu as pltpu

D_MODEL = 1024
MOBA_HEADS = 8
MOBA_HEAD_DIM = 64
MOBA_WIDTH = MOBA_HEADS * MOBA_HEAD_DIM
MOBA_BLOCK = 256
MOBA_TOPK = 3
ROPE_THETA = 10000.0
GLA_HEADS = 4
GLA_DK = 64
GLA_DV = 128
GLA_KW = GLA_HEADS * GLA_DK
GLA_VW = GLA_HEADS * GLA_DV
GLA_RANK = 16
GLA_TEMP = 16.0
GLA_CHUNK = 64
CONV_CH = 512
CONV_WIDTH = 31
D_FF = 4 * D_MODEL
N_BRANCH = 3
EPS = 1e-6
NEG = -1e30
LOG2E = 1.4426950408889634
MOBA_Q_SCALE = MOBA_HEAD_DIM ** -0.5 * LOG2E

LANES = 128
SUBLANES = 8
VMEM_LIMIT_BYTES = 56 * 1024 * 1024
MOBA_VT_ROWS = MOBA_HEAD_DIM + 2 * SUBLANES

ROW_TILE = 512
FRONT_TILE = 1024
GLA_TILE = 256
CONV_HALO = 32
CONV_ROW_CHUNK = 64

PK_MOBA = 0
PK_GLA = 3 * MOBA_WIDTH
PK_CONV = PK_GLA + 2 * GLA_KW + 2 * GLA_VW
PK_GATE = PK_CONV + 2 * CONV_CH
PK_LA = PK_GATE + N_BRANCH * D_MODEL

F32 = jnp.float32
BF16 = jnp.bfloat16


def _nt(a, b):
    return lax.dot_general(a, b, (((1,), (1,)), ((), ())), preferred_element_type=F32)


def _tn(a, b):
    return lax.dot_general(a, b, (((0,), (0,)), ((), ())), preferred_element_type=F32)


def _mm(a, b):
    return jnp.dot(a, b, preferred_element_type=F32)


def _rms(x, g):
    return x * lax.rsqrt(jnp.mean(x * x, axis=-1, keepdims=True) + EPS) * g


def _split3(x):
    hi = x.astype(BF16)
    r1 = x - hi.astype(F32)
    mid = r1.astype(BF16)
    lo = (r1 - mid.astype(F32)).astype(BF16)
    return hi, mid, lo


def _params(*sem):
    return pltpu.CompilerParams(dimension_semantics=sem, vmem_limit_bytes=VMEM_LIMIT_BYTES)


def _const_spec(shape):
    nd = len(shape)
    return pl.BlockSpec(shape, lambda *_: (0,) * nd)


def _layer_spec(layer, rows, cols, col_block=0):
    return pl.BlockSpec((None, rows, cols), lambda *_: (layer, 0, col_block))


def _vec_spec(layer, n):
    return _layer_spec(layer, 1, n)


def _rope_table_kernel(pos_ref, invf_ref, cos_ref, sin_ref):
    ang = pos_ref[...].astype(F32) * invf_ref[...]
    cos_ref[...] = jnp.cos(ang)
    sin_ref[...] = jnp.sin(ang)


def _rope_tables(positions):
    T = positions.size
    half = MOBA_HEAD_DIM // 2
    per_row = LANES // half
    inv_freq = ROPE_THETA ** (-jnp.arange(half, dtype=F32) / half)
    invf = jnp.tile(inv_freq, per_row)[None, :]
    pos = jnp.repeat(positions.reshape(T // per_row, per_row), half, axis=1)
    rows = T // per_row
    tile = min(ROW_TILE, rows)
    assert rows % tile == 0
    tab = jax.ShapeDtypeStruct((rows, LANES), F32)
    row = pl.BlockSpec((tile, LANES), lambda i: (i, 0))
    cos, sin = pl.pallas_call(
        _rope_table_kernel,
        grid=(rows // tile,),
        in_specs=[row, _const_spec((1, LANES))],
        out_specs=[row, row],
        out_shape=[tab, tab],
        compiler_params=_params("parallel"),
        name="rope_tables",
    )(pos, invf)
    return cos.reshape(T, half), sin.reshape(T, half)


def _rope(s, cos, sa, sb):
    return s * cos + pltpu.roll(s, LANES - 32, 1) * sa + pltpu.roll(s, 32, 1) * sb


def _gla_masks():
    ch, sc = GLA_CHUNK, GLA_TILE
    ri = lax.broadcasted_iota(jnp.int32, (sc, sc), 0)
    ci = lax.broadcasted_iota(jnp.int32, (sc, sc), 1)
    lower = (ri >= ci) & (ri // ch == ci // ch)
    tril = jnp.where(lower, 1.0, 0.0).astype(BF16)
    lane_head = lax.broadcasted_iota(jnp.int32, (sc, LANES), 1) // GLA_DK
    st_lane_head = lax.broadcasted_iota(jnp.int32, (GLA_DV, LANES), 1) // GLA_DK
    return lower, tril, lane_head, st_lane_head


def _gla_tile(q, k, v, r, d, gn, st_box, masks):
    ch, sc = GLA_CHUNK, GLA_TILE
    nc = sc // ch
    pair = LANES // GLA_DK
    lower, tril, lane_head, st_lane_head = masks
    d_hi, d_mid, d_lo = _split3(d)
    G = _mm(tril, d_hi) + _mm(tril, d_mid) + _mm(tril, d_lo)
    yield
    per_chunk = lambda row: jnp.concatenate(
        [jnp.broadcast_to(G[c * ch + row:c * ch + row + 1, :], (ch, LANES)) for c in range(nc)], axis=0)
    g_last = [G[(c + 1) * ch - 1:(c + 1) * ch, :] for c in range(nc)]
    g_last_rows = per_chunk(ch - 1)
    g_mid_rows = per_chunk(ch // 2 - 1)
    scale = GLA_DK ** -0.5
    qa = (q * jnp.exp(G - g_mid_rows) * scale).astype(BF16)
    kt = (k * jnp.exp(g_mid_rows - G)).astype(BF16)
    qt = (q * jnp.exp(G) * scale).astype(BF16)
    kd = (k * jnp.exp(g_last_rows - G)).astype(BF16)
    yield
    qh, vh, o_intra, upd = [], [], [], []
    for h in range(pair):
        qh.append(jnp.where(lane_head == h, qt, jnp.zeros_like(qt)))
        qah = jnp.where(lane_head == h, qa, jnp.zeros_like(qa))
        a = jnp.where(lower, _nt(qah, kt), 0.0).astype(BF16)
        vh.append(v[:, h * GLA_DV:(h + 1) * GLA_DV])
        o_intra.append(_mm(a, vh[h]))
        upd.append([_tn(vh[h][c * ch:(c + 1) * ch], kd[c * ch:(c + 1) * ch]) for c in range(nc)])
        yield
    o_inter = [[] for _ in range(pair)]
    st = st_box[0]
    for c in range(nc):
        st_b = st.astype(BF16)
        u = upd[0][c]
        for h in range(pair):
            o_inter[h].append(_nt(qh[h][c * ch:(c + 1) * ch], st_b))
            if h:
                u = jnp.where(st_lane_head == h, upd[h][c], u)
        st = st * jnp.exp(g_last[c]) + u
    st_box[0] = st
    yield
    outs = []
    for h in range(pair):
        o = _rms(o_intra[h] + jnp.concatenate(o_inter[h], axis=0), gn)
        rh = r[:, h * GLA_DV:(h + 1) * GLA_DV]
        outs.append(o * (rh * jax.nn.sigmoid(rh)))
        yield
    return outs


def _drive(*gens):
    results = [None] * len(gens)
    live = list(enumerate(gens))
    while live:
        for item in list(live):
            n, gen = item
            try:
                next(gen)
            except StopIteration as stop:
                results[n] = stop.value
                live.remove(item)
    return results


def _front_kernel(x_ref, g_ref, w_ref, wla_ref, wa2_ref, ba_ref, cos_ref, sin_ref,
                  wdw_ref, bdw_ref, gln_ref, bln_ref, gn_ref, wa32_ref, wb32_ref, wc32_ref, wo32_ref,
                  mq_ref, mk_ref, mv_ref, ob_ref, oc_ref, wa_ref, wb_ref, wc_ref, wo_ref,
                  u_ref, us_ref, y_ref, st_ref):
    ts, halo = FRONT_TILE, CONV_HALO
    i = pl.program_id(1)
    for src, dst in ((wa32_ref, wa_ref), (wb32_ref, wb_ref), (wc32_ref, wc_ref), (wo32_ref, wo_ref)):
        dst[...] = src[...].astype(BF16)

    @pl.when(i == 0)
    def _():
        u_ref[0:halo, :] = jnp.zeros((halo, CONV_CH), F32)
        st_ref[...] = jnp.zeros_like(st_ref)

    @pl.when(i > 0)
    def _():
        u_ref[0:halo, :] = u_ref[ts:ts + halo, :]

    xn = _rms(x_ref[...], g_ref[...]).astype(BF16)

    zc = _mm(xn, w_ref[:, PK_CONV:PK_CONV + 2 * CONV_CH])
    u_ref[halo:halo + ts, :] = zc[:, :CONV_CH] * jax.nn.sigmoid(zc[:, CONV_CH:])
    span = us_ref.shape[1]
    for b in range(1, SUBLANES):
        us_ref[b - 1] = u_ref[b:b + span, :]

    zm = _mm(xn, w_ref[:, PK_MOBA:PK_MOBA + 3 * MOBA_WIDTH])
    reps = LANES // cos_ref.shape[1]
    cos = jnp.tile(cos_ref[...], (1, reps))
    sin = jnp.tile(sin_ref[...], (1, reps))
    first_half = (lax.broadcasted_iota(jnp.int32, sin.shape, 1) % MOBA_HEAD_DIM) < (MOBA_HEAD_DIM // 2)
    sa = jnp.where(first_half, -sin, 0.0)
    sb = jnp.where(first_half, 0.0, sin)
    for c in range(MOBA_WIDTH // LANES):
        cols = slice(c * LANES, (c + 1) * LANES)
        mq_ref[:, cols] = (_rope(zm[:, cols], cos, sa, sb) * MOBA_Q_SCALE).astype(BF16)
        mk_ref[:, cols] = _rope(zm[:, MOBA_WIDTH + c * LANES:MOBA_WIDTH + (c + 1) * LANES], cos, sa, sb).astype(BF16)
    mv_ref[...] = zm[:, 2 * MOBA_WIDTH:].astype(BF16)

    zg = _mm(xn, w_ref[:, PK_GLA:PK_GLA + 2 * GLA_KW + 2 * GLA_VW])
    la = _mm(xn, wla_ref[...])
    pre = _mm(la.astype(BF16), wa2_ref[...]) + ba_ref[...]
    dec = (jnp.minimum(pre, 0.0) - jnp.log1p(jnp.exp(-jnp.abs(pre)))) / GLA_TEMP

    def conv():
        first = halo - (CONV_WIDTH - 1)
        for c in range(CONV_CH // LANES):
            cols = slice(c * LANES, (c + 1) * LANES)
            for r in range(0, ts, CONV_ROW_CHUNK):
                acc = jnp.broadcast_to(bdw_ref[:, cols], (CONV_ROW_CHUNK, LANES))
                for w in range(CONV_WIDTH):
                    a, b = divmod(first + w, SUBLANES)
                    lo = r + a * SUBLANES
                    tap = (u_ref[lo:lo + CONV_ROW_CHUNK, cols] if b == 0
                           else us_ref[b - 1, lo:lo + CONV_ROW_CHUNK, cols])
                    acc = acc + wdw_ref[w:w + 1, cols] * tap
                y_ref[r:r + CONV_ROW_CHUNK, cols] = acc
                yield

    masks = _gla_masks()
    gn = gn_ref[...]
    pair_dv = (LANES // GLA_DK) * GLA_DV
    n_pairs, n_sub = GLA_KW // LANES, ts // GLA_TILE
    st_boxes = [[st_ref[p]] for p in range(n_pairs)]
    glas = []
    for t in range(n_sub):
        rows = slice(t * GLA_TILE, (t + 1) * GLA_TILE)
        for p in range(n_pairs):
            kl = slice(p * LANES, (p + 1) * LANES)
            vl = slice(2 * GLA_KW + p * pair_dv, 2 * GLA_KW + (p + 1) * pair_dv)
            rl = slice(2 * GLA_KW + GLA_VW + p * pair_dv, 2 * GLA_KW + GLA_VW + (p + 1) * pair_dv)
            glas.append(_gla_tile(zg[rows, kl], zg[rows, GLA_KW + p * LANES:GLA_KW + (p + 1) * LANES],
                                  zg[rows, vl].astype(BF16), zg[rows, rl], dec[rows, kl], gn, st_boxes[p], masks))
    res = _drive(*glas, conv())
    for t in range(n_sub):
        for p in range(n_pairs):
            for h, o in enumerate(res[t * n_pairs + p]):
                ob_ref[t * GLA_TILE:(t + 1) * GLA_TILE, p * pair_dv + h * GLA_DV:p * pair_dv + (h + 1) * GLA_DV] = o.astype(BF16)
    for p in range(n_pairs):
        st_ref[p] = st_boxes[p][0]

    y = y_ref[...]
    mu = jnp.mean(y, axis=-1, keepdims=True)
    yc = y - mu
    var = jnp.mean(yc * yc, axis=-1, keepdims=True)
    yn = yc * lax.rsqrt(var + EPS) * gln_ref[...] + bln_ref[...]
    oc_ref[...] = (yn * jax.nn.sigmoid(yn)).astype(BF16)


def _front(layer, x, g, wp, wa2, ba, cos, sin, wdw, bdw, gln, bln, gn, out_ws, batch, seq):
    T = x.shape[0]
    nt = seq // FRONT_TILE
    row = lambda n: pl.BlockSpec((FRONT_TILE, n), lambda b, i: (b * nt + i, 0))
    sds = lambda n: jax.ShapeDtypeStruct((T, n), BF16)
    assert PK_MOBA == 0 and PK_GATE % LANES == 0
    steps = batch * nt
    slab_rows = [w.shape[1] // steps for w in out_ws]
    assert all(r % (2 * SUBLANES) == 0 for r in slab_rows)
    slab_in = [pl.BlockSpec((None, r, D_MODEL), lambda b, i: (layer, b * nt + i, 0)) for r in slab_rows]
    slab_out = [pl.BlockSpec((r, D_MODEL), lambda b, i: (b * nt + i, 0)) for r in slab_rows]
    return pl.pallas_call(
        _front_kernel,
        grid=(batch, nt),
        in_specs=[row(D_MODEL), _vec_spec(layer, D_MODEL), _layer_spec(layer, D_MODEL, PK_GATE, 0),
                  _layer_spec(layer, D_MODEL, LANES, PK_LA // LANES), _layer_spec(layer, LANES, GLA_KW),
                  _vec_spec(layer, GLA_KW), row(MOBA_HEAD_DIM // 2), row(MOBA_HEAD_DIM // 2),
                  _layer_spec(layer, CONV_WIDTH, CONV_CH),
                  _vec_spec(layer, CONV_CH), _vec_spec(layer, CONV_CH), _vec_spec(layer, CONV_CH),
                  _vec_spec(layer, GLA_DV)] + slab_in,
        out_specs=[row(MOBA_WIDTH), row(MOBA_WIDTH), row(MOBA_WIDTH), row(GLA_VW), row(CONV_CH)] + slab_out,
        out_shape=[sds(MOBA_WIDTH), sds(MOBA_WIDTH), sds(MOBA_WIDTH), sds(GLA_VW), sds(CONV_CH)]
        + [jax.ShapeDtypeStruct(w.shape[1:], BF16) for w in out_ws],
        scratch_shapes=[pltpu.VMEM((FRONT_TILE + CONV_HALO, CONV_CH), F32),
                        pltpu.VMEM((SUBLANES - 1, FRONT_TILE + CONV_HALO - SUBLANES, CONV_CH), F32),
                        pltpu.VMEM((FRONT_TILE, CONV_CH), F32),
                        pltpu.VMEM((GLA_KW // LANES, GLA_DV, LANES), F32)],
        compiler_params=_params("parallel", "arbitrary"),
        name="front",
    )(x, g, wp, wp, wa2, ba, cos, sin, wdw, bdw, gln, bln, gn, *out_ws)


def _moba_attn_kernel(q_ref, k_ref, v_ref, o_ref, kmean_ref, vt_ref, s_ref, *, n_blocks):
    blk = MOBA_BLOCK
    pair = LANES // MOBA_HEAD_DIM
    hd = MOBA_HEAD_DIM
    kmean_ref[...] = jnp.zeros_like(kmean_ref)
    ones_row = lax.broadcasted_iota(jnp.int32, (MOBA_VT_ROWS - hd, blk), 0) == 0
    ones_pad = jnp.where(ones_row, 1.0, 0.0).astype(BF16)
    for j in range(n_blocks):
        kj = k_ref[j * blk:(j + 1) * blk, :].astype(F32)
        kmean_ref[j:j + 1, :] = jnp.mean(kj, axis=0, keepdims=True)
        vt = v_ref[j * blk:(j + 1) * blk, :].astype(F32).T.astype(BF16)
        for h in range(pair):
            vt_ref[j, h, 0:hd, :] = vt[h * hd:(h + 1) * hd, :]
            vt_ref[j, h, hd:, :] = ones_pad

    kpos = lax.broadcasted_iota(jnp.int32, (blk, blk), 0)
    qpos = lax.broadcasted_iota(jnp.int32, (blk, blk), 1)
    causal = kpos <= qpos
    qlane = lax.broadcasted_iota(jnp.int32, (blk, LANES), 1) // hd
    sub = lambda t: t.reshape(blk // SUBLANES, SUBLANES, blk)
    col_max = {}

    def scores(ii):
        q = q_ref[ii * blk:(ii + 1) * blk, :]
        need_gate = ii > MOBA_TOPK
        if need_gate:
            km_hi, km_mid, km_lo = _split3(kmean_ref[...])
            jidx = lax.broadcasted_iota(jnp.int32, (kmean_ref.shape[0], blk), 0)
        for h in range(pair):
            qh = jnp.where(qlane == h, q, jnp.zeros_like(q))
            valid = [None] * ii
            if need_gate:
                gate = _nt(km_hi, qh) + _nt(km_mid, qh) + _nt(km_lo, qh)
                for n in range(ii):
                    gn = gate[n:n + 1, :]
                    beats = ((gate > gn) | ((gate == gn) & (jidx < n))) & (jidx < ii)
                    rank = jnp.sum(jnp.where(beats, 1.0, 0.0), axis=0, keepdims=True)
                    valid[n] = rank < MOBA_TOPK
            m8 = None
            for j in range(ii + 1):
                s = _nt(k_ref[j * blk:(j + 1) * blk, :], qh)
                if j == ii:
                    s = jnp.where(causal, s, NEG)
                elif valid[j] is not None:
                    s = jnp.where(valid[j], s, NEG)
                s_ref[ii % 2, h, j] = s
                t = jnp.max(sub(s), axis=0)
                m8 = t if m8 is None else jnp.maximum(m8, t)
                yield
            col_max[ii, h] = jnp.max(m8, axis=0, keepdims=True)

    def values(ii):
        outs = []
        for h in range(pair):
            acc = None
            for j in range(ii + 1):
                p = jnp.exp2(s_ref[ii % 2, h, j] - col_max[ii, h])
                t = _mm(vt_ref[j, h], p.astype(BF16))
                acc = t if acc is None else acc + t
                yield
            outs.append(acc[0:hd, :] / acc[hd:hd + 1, :])
        o_ref[ii * blk:(ii + 1) * blk, :] = jnp.concatenate(outs, axis=0).T.astype(BF16)

    for _ in scores(0):
        pass
    for ii in range(n_blocks):
        live = [values(ii)]
        if ii + 1 < n_blocks:
            live.insert(0, scores(ii + 1))
        while live:
            for gen in list(live):
                if next(gen, StopIteration) is StopIteration:
                    live.remove(gen)


def _moba_attn(q, k, v, batch, seq):
    n_blocks = seq // MOBA_BLOCK
    n_pairs = MOBA_WIDTH // LANES
    pair = LANES // MOBA_HEAD_DIM
    spec = pl.BlockSpec((seq, LANES), lambda b, p: (b, p))
    gate_rows = 2 * SUBLANES
    assert n_blocks <= gate_rows
    return pl.pallas_call(
        functools.partial(_moba_attn_kernel, n_blocks=n_blocks),
        grid=(batch, n_pairs),
        in_specs=[spec, spec, spec],
        out_specs=spec,
        out_shape=jax.ShapeDtypeStruct(q.shape, BF16),
        scratch_shapes=[
            pltpu.VMEM((gate_rows, LANES), F32),
            pltpu.VMEM((n_blocks, pair, MOBA_VT_ROWS, MOBA_BLOCK), BF16),
            pltpu.VMEM((2, pair, n_blocks, MOBA_BLOCK, MOBA_BLOCK), F32),
        ],
        compiler_params=_params("parallel", "parallel"),
        name="moba_attn",
    )(q, k, v)


def _combine_kernel(x_ref, g_ref, wga_ref, wgb_ref, wgc_ref, bg_ref, a_ref, b_ref, c_ref, wa_ref, wb_ref,
                    wc_ref, wo_ref, gp_ref, wu32_ref, wd32_ref, o_ref, wu_ref, wd_ref):
    wu_ref[...] = wu32_ref[...].astype(BF16)
    wd_ref[...] = wd32_ref[...].astype(BF16)
    x = x_ref[...]
    xn = _rms(x, g_ref[...]).astype(BF16)
    mix = None
    branches = ((wga_ref, a_ref, wa_ref), (wgb_ref, b_ref, wb_ref), (wgc_ref, c_ref, wc_ref))
    for n, (wg_ref, br_ref, wbr_ref) in enumerate(branches):
        cols = slice(n * D_MODEL, (n + 1) * D_MODEL)
        gate = jax.nn.sigmoid(_mm(xn, wg_ref[...]) + bg_ref[:, cols])
        term = gate * _mm(br_ref[...], wbr_ref[...])
        mix = term if mix is None else mix + term
    y = _mm(mix.astype(BF16), wo_ref[...])
    o_ref[...] = x + _rms(y, gp_ref[...])


def _combine(layer, x, g, wp, bg, a, b, c, wa, wb, wc, wo, gp, wu32, wd32):
    T = x.shape[0]
    slab = D_FF // (T // ROW_TILE)
    assert slab % LANES == 0
    row = lambda n: pl.BlockSpec((ROW_TILE, n), lambda i: (i, 0))
    gate_w = lambda n: _layer_spec(layer, D_MODEL, D_MODEL, PK_GATE // D_MODEL + n)
    return pl.pallas_call(
        _combine_kernel,
        grid=(T // ROW_TILE,),
        in_specs=[row(D_MODEL), _vec_spec(layer, D_MODEL), gate_w(0), gate_w(1), gate_w(2),
                  _vec_spec(layer, N_BRANCH * D_MODEL),
                  row(MOBA_WIDTH), row(GLA_VW), row(CONV_CH),
                  _const_spec(wa.shape), _const_spec(wb.shape), _const_spec(wc.shape), _const_spec(wo.shape),
                  _vec_spec(layer, D_MODEL),
                  pl.BlockSpec((None, D_MODEL, slab), lambda i: (layer, 0, i)),
                  pl.BlockSpec((None, slab, D_MODEL), lambda i: (layer, i, 0))],
        out_specs=[row(D_MODEL), pl.BlockSpec((D_MODEL, slab), lambda i: (0, i)),
                   pl.BlockSpec((slab, D_MODEL), lambda i: (i, 0))],
        out_shape=[jax.ShapeDtypeStruct((T, D_MODEL), F32), jax.ShapeDtypeStruct((D_MODEL, D_FF), BF16),
                   jax.ShapeDtypeStruct((D_FF, D_MODEL), BF16)],
        compiler_params=_params("parallel"),
        name="combine",
    )(x, g, wp, wp, wp, bg, a, b, c, wa, wb, wc, wo, gp, wu32, wd32)


def _mlp_kernel(x_ref, g_ref, wu_ref, wd_ref, gp_ref, o_ref):
    x = x_ref[...]
    xn = _rms(x, g_ref[...]).astype(BF16)
    acc = None
    for c in range(D_FF // D_MODEL):
        cols = slice(c * D_MODEL, (c + 1) * D_MODEL)
        u = jnp.maximum(_mm(xn, wu_ref[:, cols]), 0.0)
        t = _mm((u * u).astype(BF16), wd_ref[cols, :])
        acc = t if acc is None else acc + t
    o_ref[...] = x + _rms(acc, gp_ref[...])


def _mlp(layer, x, g, wu, wd, gp):
    T = x.shape[0]
    row = pl.BlockSpec((ROW_TILE, D_MODEL), lambda i: (i, 0))
    return pl.pallas_call(
        _mlp_kernel,
        grid=(T // ROW_TILE,),
        in_specs=[row, _vec_spec(layer, D_MODEL), _const_spec((D_MODEL, D_FF)), _const_spec((D_FF, D_MODEL)),
                  _vec_spec(layer, D_MODEL)],
        out_specs=row,
        out_shape=jax.ShapeDtypeStruct((T, D_MODEL), F32),
        compiler_params=_params("parallel"),
        name="mlp",
    )(x, g, wu, wd, gp)


def kernel(x, positions, g_mix_pre, w_in, b_gate, w_gla_a2, b_gla_a, g_gla_norm, w_dw, b_dw, g_conv_ln,
           b_conv_ln, w_moba_o, w_gla_o, w_conv_o, w_mix_out, g_mix_post, g_mlp_pre, w_mlp_up, w_mlp_down,
           g_mlp_post):
    batch, seq, _ = x.shape
    depth = w_in.shape[0]
    assert seq % MOBA_BLOCK == 0 and seq % FRONT_TILE == 0 and (batch * seq) % ROW_TILE == 0
    assert FRONT_TILE % GLA_TILE == 0 and GLA_TILE % GLA_CHUNK == 0
    h = x.reshape(batch * seq, D_MODEL)
    cos, sin = _rope_tables(positions)

    o_la = 3 * MOBA_WIDTH + 2 * GLA_KW + GLA_VW
    o_lr = o_la + GLA_RANK
    pad_la = jnp.pad(w_in[:, :, o_la:o_lr], ((0, 0), (0, 0), (0, LANES - GLA_RANK)))
    wp = jnp.concatenate([w_in[:, :, :o_la], w_in[:, :, o_lr:], pad_la], axis=-1).astype(BF16)
    assert wp.shape[-1] == PK_LA + LANES
    wa2 = jnp.pad(w_gla_a2, ((0, 0), (0, LANES - GLA_RANK), (0, 0))).astype(BF16)
    vec = lambda v: v[:, None, :]
    g_pre, g_post, g_mpre, g_mpost = vec(g_mix_pre), vec(g_mix_post), vec(g_mlp_pre), vec(g_mlp_post)
    bg, ba, gn = vec(b_gate), vec(b_gla_a), vec(g_gla_norm)
    bdw, gln, bln = vec(b_dw), vec(g_conv_ln), vec(b_conv_ln)
    out_ws = (w_moba_o, w_gla_o, w_conv_o, w_mix_out)

    for l in range(depth):
        mq, mk, mv, ob, oc, wa, wb, wc, wo = _front(l, h, g_pre, wp, wa2, ba, cos, sin, w_dw, bdw, gln, bln, gn,
                                                    out_ws, batch, seq)
        oa = _moba_attn(mq, mk, mv, batch, seq)
        h, wu, wd = _combine(l, h, g_pre, wp, bg, oa, ob, oc, wa, wb, wc, wo, g_post, w_mlp_up, w_mlp_down)
        h = _mlp(l, h, g_mpre, wu, wd, g_mpost)
    return h.reshape(batch, seq, D_MODEL)
```

```python
import functools

import jax
import jax.numpy as jnp
from jax import lax
from jax.experimental import pallas as pl
from jax.experimental.pallas import tpu as pltpu

D_MODEL = 1024
MOBA_HEADS = 8
MOBA_HEAD_DIM = 64
MOBA_WIDTH = MOBA_HEADS * MOBA_HEAD_DIM
MOBA_BLOCK = 256
MOBA_TOPK = 3
ROPE_THETA = 10000.0
GLA_HEADS = 4
GLA_DK = 64
GLA_DV = 128
GLA_KW = GLA_HEADS * GLA_DK
GLA_VW = GLA_HEADS * GLA_DV
GLA_RANK = 16
GLA_TEMP = 16.0
GLA_CHUNK = 64
CONV_CH = 512
CONV_WIDTH = 31
D_FF = 4 * D_MODEL
N_BRANCH = 3
EPS = 1e-6
NEG = -1e30
LOG2E = 1.4426950408889634
MOBA_Q_SCALE = MOBA_HEAD_DIM ** -0.5 * LOG2E

LANES = 128
SUBLANES = 8
VMEM_LIMIT_BYTES = 56 * 1024 * 1024
MOBA_VT_ROWS = MOBA_HEAD_DIM + 2 * SUBLANES

ROW_TILE = 512
COMBINE_SPLIT = 2
FRONT_TILE = 1024
GLA_TILE = 256
CONV_HALO = 32
CONV_ROW_CHUNK = 64

PK_MOBA = 0
PK_GLA = 3 * MOBA_WIDTH
PK_CONV = PK_GLA + 2 * GLA_KW + 2 * GLA_VW
PK_GATE = PK_CONV + 2 * CONV_CH
PK_LA = PK_GATE + N_BRANCH * D_MODEL

F32 = jnp.float32
BF16 = jnp.bfloat16


def _nt(a, b):
    return lax.dot_general(a, b, (((1,), (1,)), ((), ())), preferred_element_type=F32)


def _tn(a, b):
    return lax.dot_general(a, b, (((0,), (0,)), ((), ())), preferred_element_type=F32)


def _mm(a, b):
    return jnp.dot(a, b, preferred_element_type=F32)


def _rms(x, g):
    return x * lax.rsqrt(jnp.mean(x * x, axis=-1, keepdims=True) + EPS) * g


def _split3(x):
    hi = x.astype(BF16)
    r1 = x - hi.astype(F32)
    mid = r1.astype(BF16)
    lo = (r1 - mid.astype(F32)).astype(BF16)
    return hi, mid, lo


def _params(*sem):
    return pltpu.CompilerParams(dimension_semantics=sem, vmem_limit_bytes=VMEM_LIMIT_BYTES)


def _const_spec(shape):
    nd = len(shape)
    return pl.BlockSpec(shape, lambda *_: (0,) * nd)


def _layer_spec(layer, rows, cols, col_block=0):
    return pl.BlockSpec((None, rows, cols), lambda *_: (layer, 0, col_block))


def _vec_spec(layer, n):
    return _layer_spec(layer, 1, n)


def _rope_table_kernel(pos_ref, invf_ref, cos_ref, sin_ref):
    ang = pos_ref[...].astype(F32) * invf_ref[...]
    cos_ref[...] = jnp.cos(ang)
    sin_ref[...] = jnp.sin(ang)


def _rope_tables(positions):
    T = positions.size
    half = MOBA_HEAD_DIM // 2
    per_row = LANES // half
    inv_freq = ROPE_THETA ** (-jnp.arange(half, dtype=F32) / half)
    invf = jnp.tile(inv_freq, per_row)[None, :]
    pos = jnp.repeat(positions.reshape(T // per_row, per_row), half, axis=1)
    rows = T // per_row
    tile = min(ROW_TILE, rows)
    assert rows % tile == 0
    tab = jax.ShapeDtypeStruct((rows, LANES), F32)
    row = pl.BlockSpec((tile, LANES), lambda i: (i, 0))
    cos, sin = pl.pallas_call(
        _rope_table_kernel,
        grid=(rows // tile,),
        in_specs=[row, _const_spec((1, LANES))],
        out_specs=[row, row],
        out_shape=[tab, tab],
        compiler_params=_params("parallel"),
        name="rope_tables",
    )(pos, invf)
    return cos.reshape(T, half), sin.reshape(T, half)


def _rope(s, cos, sa, sb):
    return s * cos + pltpu.roll(s, LANES - 32, 1) * sa + pltpu.roll(s, 32, 1) * sb


def _gla_masks():
    ch, sc = GLA_CHUNK, GLA_TILE
    ri = lax.broadcasted_iota(jnp.int32, (sc, sc), 0)
    ci = lax.broadcasted_iota(jnp.int32, (sc, sc), 1)
    lower = (ri >= ci) & (ri // ch == ci // ch)
    tril = jnp.where(lower, 1.0, 0.0).astype(BF16)
    lane_head = lax.broadcasted_iota(jnp.int32, (sc, LANES), 1) // GLA_DK
    st_lane_head = lax.broadcasted_iota(jnp.int32, (GLA_DV, LANES), 1) // GLA_DK
    return lower, tril, lane_head, st_lane_head


def _gla_tile(q, k, v, r, d, gn, st_box, masks):
    ch, sc = GLA_CHUNK, GLA_TILE
    nc = sc // ch
    pair = LANES // GLA_DK
    lower, tril, lane_head, st_lane_head = masks
    d_hi, d_mid, d_lo = _split3(d)
    g3 = _mm(tril, jnp.concatenate([d_hi, d_mid, d_lo], axis=1))
    G = g3[:, :LANES] + g3[:, LANES:2 * LANES] + g3[:, 2 * LANES:]
    yield
    per_chunk = lambda row: jnp.concatenate(
        [jnp.broadcast_to(G[c * ch + row:c * ch + row + 1, :], (ch, LANES)) for c in range(nc)], axis=0)
    g_last = [G[(c + 1) * ch - 1:(c + 1) * ch, :] for c in range(nc)]
    g_last_rows = per_chunk(ch - 1)
    g_mid_rows = per_chunk(ch // 2 - 1)
    scale = GLA_DK ** -0.5
    qa = (q * jnp.exp(G - g_mid_rows) * scale).astype(BF16)
    kt = (k * jnp.exp(g_mid_rows - G)).astype(BF16)
    qt = (q * jnp.exp(G) * scale).astype(BF16)
    kd = (k * jnp.exp(g_last_rows - G)).astype(BF16)
    yield
    qh, vh, o_intra, upd = [], [], [], []
    for h in range(pair):
        qh.append(jnp.where(lane_head == h, qt, jnp.zeros_like(qt)))
        qah = jnp.where(lane_head == h, qa, jnp.zeros_like(qa))
        a = jnp.where(lower, _nt(qah, kt), 0.0).astype(BF16)
        vh.append(v[:, h * GLA_DV:(h + 1) * GLA_DV])
        o_intra.append(_mm(a, vh[h]))
        upd.append([_tn(vh[h][c * ch:(c + 1) * ch], kd[c * ch:(c + 1) * ch]) for c in range(nc)])
        yield
    o_inter = [[] for _ in range(pair)]
    st = st_box[0]
    for c in range(nc):
        st_b = st.astype(BF16)
        u = upd[0][c]
        for h in range(pair):
            o_inter[h].append(_nt(qh[h][c * ch:(c + 1) * ch], st_b))
            if h:
                u = jnp.where(st_lane_head == h, upd[h][c], u)
        st = st * jnp.exp(g_last[c]) + u
    st_box[0] = st
    yield
    outs = []
    for h in range(pair):
        o = _rms(o_intra[h] + jnp.concatenate(o_inter[h], axis=0), gn)
        rh = r[:, h * GLA_DV:(h + 1) * GLA_DV]
        outs.append(o * (rh * jax.nn.sigmoid(rh)))
        yield
    return outs


def _drive(*gens):
    results = [None] * len(gens)
    live = list(enumerate(gens))
    while live:
        for item in list(live):
            n, gen = item
            try:
                next(gen)
            except StopIteration as stop:
                results[n] = stop.value
                live.remove(item)
    return results


def _front_kernel(x_ref, g_ref, w_ref, wla_ref, wa2_ref, ba_ref, cos_ref, sin_ref,
                  wdw_ref, bdw_ref, gln_ref, bln_ref, gn_ref, wa32_ref, wb32_ref, wc32_ref, wo32_ref,
                  mq_ref, mk_ref, mv_ref, ob_ref, oc_ref, wa_ref, wb_ref, wc_ref, wo_ref,
                  u_ref, us_ref, y_ref, st_ref):
    ts, halo = FRONT_TILE, CONV_HALO
    i = pl.program_id(1)
    for src, dst in ((wa32_ref, wa_ref), (wb32_ref, wb_ref), (wc32_ref, wc_ref), (wo32_ref, wo_ref)):
        dst[...] = src[...].astype(BF16)

    @pl.when(i == 0)
    def _():
        u_ref[0:halo, :] = jnp.zeros((halo, CONV_CH), F32)
        st_ref[...] = jnp.zeros_like(st_ref)

    @pl.when(i > 0)
    def _():
        u_ref[0:halo, :] = u_ref[ts:ts + halo, :]

    xn = _rms(x_ref[...], g_ref[...]).astype(BF16)

    zc = _mm(xn, w_ref[:, PK_CONV:PK_CONV + 2 * CONV_CH])
    u_ref[halo:halo + ts, :] = zc[:, :CONV_CH] * jax.nn.sigmoid(zc[:, CONV_CH:])
    span = us_ref.shape[1]
    for b in range(1, SUBLANES):
        us_ref[b - 1] = u_ref[b:b + span, :]

    zm = _mm(xn, w_ref[:, PK_MOBA:PK_MOBA + 3 * MOBA_WIDTH])
    reps = LANES // cos_ref.shape[1]
    cos = jnp.tile(cos_ref[...], (1, reps))
    sin = jnp.tile(sin_ref[...], (1, reps))
    first_half = (lax.broadcasted_iota(jnp.int32, sin.shape, 1) % MOBA_HEAD_DIM) < (MOBA_HEAD_DIM // 2)
    sa = jnp.where(first_half, -sin, 0.0)
    sb = jnp.where(first_half, 0.0, sin)
    for c in range(MOBA_WIDTH // LANES):
        cols = slice(c * LANES, (c + 1) * LANES)
        mq_ref[:, cols] = (_rope(zm[:, cols], cos, sa, sb) * MOBA_Q_SCALE).astype(BF16)
        mk_ref[:, cols] = _rope(zm[:, MOBA_WIDTH + c * LANES:MOBA_WIDTH + (c + 1) * LANES], cos, sa, sb).astype(BF16)
    mv_ref[...] = zm[:, 2 * MOBA_WIDTH:].astype(BF16)

    zg = _mm(xn, w_ref[:, PK_GLA:PK_GLA + 2 * GLA_KW + 2 * GLA_VW])
    la = _mm(xn, wla_ref[...])
    pre = _mm(la.astype(BF16), wa2_ref[...]) + ba_ref[...]
    dec = (jnp.minimum(pre, 0.0) - jnp.log1p(jnp.exp(-jnp.abs(pre)))) / GLA_TEMP

    def conv():
        first = halo - (CONV_WIDTH - 1)
        for c in range(CONV_CH // LANES):
            cols = slice(c * LANES, (c + 1) * LANES)
            for r in range(0, ts, CONV_ROW_CHUNK):
                acc = jnp.broadcast_to(bdw_ref[:, cols], (CONV_ROW_CHUNK, LANES))
                for w in range(CONV_WIDTH):
                    a, b = divmod(first + w, SUBLANES)
                    lo = r + a * SUBLANES
                    tap = (u_ref[lo:lo + CONV_ROW_CHUNK, cols] if b == 0
                           else us_ref[b - 1, lo:lo + CONV_ROW_CHUNK, cols])
                    acc = acc + wdw_ref[w:w + 1, cols] * tap
                y_ref[r:r + CONV_ROW_CHUNK, cols] = acc
                yield

    masks = _gla_masks()
    gn = gn_ref[...]
    pair_dv = (LANES // GLA_DK) * GLA_DV
    n_pairs, n_sub = GLA_KW // LANES, ts // GLA_TILE
    st_boxes = [[st_ref[p]] for p in range(n_pairs)]
    glas = []
    for t in range(n_sub):
        rows = slice(t * GLA_TILE, (t + 1) * GLA_TILE)
        for p in range(n_pairs):
            kl = slice(p * LANES, (p + 1) * LANES)
            vl = slice(2 * GLA_KW + p * pair_dv, 2 * GLA_KW + (p + 1) * pair_dv)
            rl = slice(2 * GLA_KW + GLA_VW + p * pair_dv, 2 * GLA_KW + GLA_VW + (p + 1) * pair_dv)
            glas.append(_gla_tile(zg[rows, kl], zg[rows, GLA_KW + p * LANES:GLA_KW + (p + 1) * LANES],
                                  zg[rows, vl].astype(BF16), zg[rows, rl], dec[rows, kl], gn, st_boxes[p], masks))
    res = _drive(*glas, conv())
    for t in range(n_sub):
        for p in range(n_pairs):
            for h, o in enumerate(res[t * n_pairs + p]):
                ob_ref[t * GLA_TILE:(t + 1) * GLA_TILE, p * pair_dv + h * GLA_DV:p * pair_dv + (h + 1) * GLA_DV] = o.astype(BF16)
    for p in range(n_pairs):
        st_ref[p] = st_boxes[p][0]

    y = y_ref[...]
    mu = jnp.mean(y, axis=-1, keepdims=True)
    yc = y - mu
    var = jnp.mean(yc * yc, axis=-1, keepdims=True)
    yn = yc * lax.rsqrt(var + EPS) * gln_ref[...] + bln_ref[...]
    oc_ref[...] = (yn * jax.nn.sigmoid(yn)).astype(BF16)


def _front(layer, x, g, wp, wa2, ba, cos, sin, wdw, bdw, gln, bln, gn, out_ws, batch, seq):
    T = x.shape[0]
    nt = seq // FRONT_TILE
    row = lambda n: pl.BlockSpec((FRONT_TILE, n), lambda b, i: (b * nt + i, 0))
    sds = lambda n: jax.ShapeDtypeStruct((T, n), BF16)
    assert PK_MOBA == 0 and PK_GATE % LANES == 0
    steps = batch * nt
    slab_rows = [w.shape[1] // steps for w in out_ws]
    assert all(r % (2 * SUBLANES) == 0 for r in slab_rows)
    slab_in = [pl.BlockSpec((None, r, D_MODEL), lambda b, i: (layer, b * nt + i, 0)) for r in slab_rows]
    slab_out = [pl.BlockSpec((r, D_MODEL), lambda b, i: (b * nt + i, 0)) for r in slab_rows]
    return pl.pallas_call(
        _front_kernel,
        grid=(batch, nt),
        in_specs=[row(D_MODEL), _vec_spec(layer, D_MODEL), _layer_spec(layer, D_MODEL, PK_GATE, 0),
                  _layer_spec(layer, D_MODEL, LANES, PK_LA // LANES), _layer_spec(layer, LANES, GLA_KW),
                  _vec_spec(layer, GLA_KW), row(MOBA_HEAD_DIM // 2), row(MOBA_HEAD_DIM // 2),
                  _layer_spec(layer, CONV_WIDTH, CONV_CH),
                  _vec_spec(layer, CONV_CH), _vec_spec(layer, CONV_CH), _vec_spec(layer, CONV_CH),
                  _vec_spec(layer, GLA_DV)] + slab_in,
        out_specs=[row(MOBA_WIDTH), row(MOBA_WIDTH), row(MOBA_WIDTH), row(GLA_VW), row(CONV_CH)] + slab_out,
        out_shape=[sds(MOBA_WIDTH), sds(MOBA_WIDTH), sds(MOBA_WIDTH), sds(GLA_VW), sds(CONV_CH)]
        + [jax.ShapeDtypeStruct(w.shape[1:], BF16) for w in out_ws],
        scratch_shapes=[pltpu.VMEM((FRONT_TILE + CONV_HALO, CONV_CH), F32),
                        pltpu.VMEM((SUBLANES - 1, FRONT_TILE + CONV_HALO - SUBLANES, CONV_CH), F32),
                        pltpu.VMEM((FRONT_TILE, CONV_CH), F32),
                        pltpu.VMEM((GLA_KW // LANES, GLA_DV, LANES), F32)],
        compiler_params=_params("parallel", "arbitrary"),
        name="front",
    )(x, g, wp, wp, wa2, ba, cos, sin, wdw, bdw, gln, bln, gn, *out_ws)


def _moba_attn_kernel(q_ref, k_ref, v_ref, o_ref, kmean_ref, vt_ref, s_ref, *, n_blocks):
    blk = MOBA_BLOCK
    pair = LANES // MOBA_HEAD_DIM
    hd = MOBA_HEAD_DIM
    kmean_ref[...] = jnp.zeros_like(kmean_ref)
    ones_row = lax.broadcasted_iota(jnp.int32, (MOBA_VT_ROWS - hd, blk), 0) == 0
    ones_pad = jnp.where(ones_row, 1.0, 0.0).astype(BF16)
    for j in range(n_blocks):
        kj = k_ref[j * blk:(j + 1) * blk, :].astype(F32)
        kmean_ref[j:j + 1, :] = jnp.mean(kj, axis=0, keepdims=True)
        vt = v_ref[j * blk:(j + 1) * blk, :].astype(F32).T.astype(BF16)
        for h in range(pair):
            vt_ref[j, h, 0:hd, :] = vt[h * hd:(h + 1) * hd, :]
            vt_ref[j, h, hd:, :] = ones_pad

    kpos = lax.broadcasted_iota(jnp.int32, (blk, blk), 0)
    qpos = lax.broadcasted_iota(jnp.int32, (blk, blk), 1)
    causal = kpos <= qpos
    qlane = lax.broadcasted_iota(jnp.int32, (blk, LANES), 1) // hd
    sub = lambda t: t.reshape(blk // SUBLANES, SUBLANES, blk)
    col_max = {}

    def scores(ii):
        q = q_ref[ii * blk:(ii + 1) * blk, :]
        need_gate = ii > MOBA_TOPK
        if need_gate:
            km_hi, km_mid, km_lo = _split3(kmean_ref[...])
            jidx = lax.broadcasted_iota(jnp.int32, (kmean_ref.shape[0], blk), 0)
        for h in range(pair):
            qh = jnp.where(qlane == h, q, jnp.zeros_like(q))
            valid = [None] * ii
            if need_gate:
                gate = _nt(km_hi, qh) + _nt(km_mid, qh) + _nt(km_lo, qh)
                for n in range(ii):
                    gn = gate[n:n + 1, :]
                    beats = ((gate > gn) | ((gate == gn) & (jidx < n))) & (jidx < ii)
                    rank = jnp.sum(jnp.where(beats, 1.0, 0.0), axis=0, keepdims=True)
                    valid[n] = rank < MOBA_TOPK
            m8 = None
            for j in range(ii + 1):
                s = _nt(k_ref[j * blk:(j + 1) * blk, :], qh)
                if j == ii:
                    s = jnp.where(causal, s, NEG)
                elif valid[j] is not None:
                    s = jnp.where(valid[j], s, NEG)
                s_ref[ii % 2, h, j] = s
                t = jnp.max(sub(s), axis=0)
                m8 = t if m8 is None else jnp.maximum(m8, t)
                yield
            col_max[ii, h] = jnp.max(m8, axis=0, keepdims=True)

    def values(ii):
        outs = []
        for h in range(pair):
            acc = None
            for j in range(ii + 1):
                p = jnp.exp2(s_ref[ii % 2, h, j] - col_max[ii, h])
                t = _mm(vt_ref[j, h], p.astype(BF16))
                acc = t if acc is None else acc + t
                yield
            outs.append(acc[0:hd, :] / acc[hd:hd + 1, :])
        o_ref[ii * blk:(ii + 1) * blk, :] = jnp.concatenate(outs, axis=0).T.astype(BF16)

    for _ in scores(0):
        pass
    for ii in range(n_blocks):
        live = [values(ii)]
        if ii + 1 < n_blocks:
            live.insert(0, scores(ii + 1))
        while live:
            for gen in list(live):
                if next(gen, StopIteration) is StopIteration:
                    live.remove(gen)


def _moba_attn(q, k, v, batch, seq):
    n_blocks = seq // MOBA_BLOCK
    n_pairs = MOBA_WIDTH // LANES
    pair = LANES // MOBA_HEAD_DIM
    spec = pl.BlockSpec((seq, LANES), lambda b, p: (b, p))
    gate_rows = 2 * SUBLANES
    assert n_blocks <= gate_rows
    return pl.pallas_call(
        functools.partial(_moba_attn_kernel, n_blocks=n_blocks),
        grid=(batch, n_pairs),
        in_specs=[spec, spec, spec],
        out_specs=spec,
        out_shape=jax.ShapeDtypeStruct(q.shape, BF16),
        scratch_shapes=[
            pltpu.VMEM((gate_rows, LANES), F32),
            pltpu.VMEM((n_blocks, pair, MOBA_VT_ROWS, MOBA_BLOCK), BF16),
            pltpu.VMEM((2, pair, n_blocks, MOBA_BLOCK, MOBA_BLOCK), F32),
        ],
        compiler_params=_params("parallel", "parallel"),
        name="moba_attn",
    )(q, k, v)


def _combine_kernel(x_ref, g_ref, wga_ref, wgb_ref, wgc_ref, bg_ref, a_ref, b_ref, c_ref, wa_ref, wb_ref,
                    wc_ref, wo_ref, gp_ref, wu32_ref, wd32_ref, o_ref, wu_ref, wd_ref):
    wu_ref[...] = wu32_ref[...].astype(BF16)
    wd_ref[...] = wd32_ref[...].astype(BF16)
    branches = ((wga_ref, a_ref, wa_ref), (wgb_ref, b_ref, wb_ref), (wgc_ref, c_ref, wc_ref))
    group = x_ref.shape[0] // COMBINE_SPLIT
    for r in range(COMBINE_SPLIT):
        rows = slice(r * group, (r + 1) * group)
        x = x_ref[rows, :]
        xn = _rms(x, g_ref[...]).astype(BF16)
        mix = None
        for n, (wg_ref, br_ref, wbr_ref) in enumerate(branches):
            cols = slice(n * D_MODEL, (n + 1) * D_MODEL)
            gate = jax.nn.sigmoid(_mm(xn, wg_ref[...]) + bg_ref[:, cols])
            term = gate * _mm(br_ref[rows, :], wbr_ref[...])
            mix = term if mix is None else mix + term
        y = _mm(mix.astype(BF16), wo_ref[...])
        o_ref[rows, :] = x + _rms(y, gp_ref[...])


def _combine(layer, x, g, wp, bg, a, b, c, wa, wb, wc, wo, gp, wu32, wd32):
    T = x.shape[0]
    slab = D_FF // (T // ROW_TILE)
    assert slab % LANES == 0
    row = lambda n: pl.BlockSpec((ROW_TILE, n), lambda i: (i, 0))
    gate_w = lambda n: _layer_spec(layer, D_MODEL, D_MODEL, PK_GATE // D_MODEL + n)
    return pl.pallas_call(
        _combine_kernel,
        grid=(T // ROW_TILE,),
        in_specs=[row(D_MODEL), _vec_spec(layer, D_MODEL), gate_w(0), gate_w(1), gate_w(2),
                  _vec_spec(layer, N_BRANCH * D_MODEL),
                  row(MOBA_WIDTH), row(GLA_VW), row(CONV_CH),
                  _const_spec(wa.shape), _const_spec(wb.shape), _const_spec(wc.shape), _const_spec(wo.shape),
                  _vec_spec(layer, D_MODEL),
                  pl.BlockSpec((None, D_MODEL, slab), lambda i: (layer, 0, i)),
                  pl.BlockSpec((None, slab, D_MODEL), lambda i: (layer, i, 0))],
        out_specs=[row(D_MODEL), pl.BlockSpec((D_MODEL, slab), lambda i: (0, i)),
                   pl.BlockSpec((slab, D_MODEL), lambda i: (i, 0))],
        out_shape=[jax.ShapeDtypeStruct((T, D_MODEL), F32), jax.ShapeDtypeStruct((D_MODEL, D_FF), BF16),
                   jax.ShapeDtypeStruct((D_FF, D_MODEL), BF16)],
        compiler_params=_params("parallel"),
        name="combine",
    )(x, g, wp, wp, wp, bg, a, b, c, wa, wb, wc, wo, gp, wu32, wd32)


def _mlp_kernel(x_ref, g_ref, wu_ref, wd_ref, gp_ref, o_ref):
    x = x_ref[...]
    xn = _rms(x, g_ref[...]).astype(BF16)
    acc = None
    for c in range(D_FF // D_MODEL):
        cols = slice(c * D_MODEL, (c + 1) * D_MODEL)
        u = jnp.maximum(_mm(xn, wu_ref[:, cols]), 0.0)
        t = _mm((u * u).astype(BF16), wd_ref[cols, :])
        acc = t if acc is None else acc + t
    o_ref[...] = x + _rms(acc, gp_ref[...])


def _mlp(layer, x, g, wu, wd, gp):
    T = x.shape[0]
    row = pl.BlockSpec((ROW_TILE, D_MODEL), lambda i: (i, 0))
    return pl.pallas_call(
        _mlp_kernel,
        grid=(T // ROW_TILE,),
        in_specs=[row, _vec_spec(layer, D_MODEL), _const_spec((D_MODEL, D_FF)), _const_spec((D_FF, D_MODEL)),
                  _vec_spec(layer, D_MODEL)],
        out_specs=row,
        out_shape=jax.ShapeDtypeStruct((T, D_MODEL), F32),
        compiler_params=_params("parallel"),
        name="mlp",
    )(x, g, wu, wd, gp)


def kernel(x, positions, g_mix_pre, w_in, b_gate, w_gla_a2, b_gla_a, g_gla_norm, w_dw, b_dw, g_conv_ln,
           b_conv_ln, w_moba_o, w_gla_o, w_conv_o, w_mix_out, g_mix_post, g_mlp_pre, w_mlp_up, w_mlp_down,
           g_mlp_post):
    batch, seq, _ = x.shape
    depth = w_in.shape[0]
    assert seq % MOBA_BLOCK == 0 and seq % FRONT_TILE == 0 and (batch * seq) % ROW_TILE == 0
    assert FRONT_TILE % GLA_TILE == 0 and GLA_TILE % GLA_CHUNK == 0
    h = x.reshape(batch * seq, D_MODEL)
    cos, sin = _rope_tables(positions)

    o_la = 3 * MOBA_WIDTH + 2 * GLA_KW + GLA_VW
    o_lr = o_la + GLA_RANK
    pad_la = jnp.pad(w_in[:, :, o_la:o_lr], ((0, 0), (0, 0), (0, LANES - GLA_RANK)))
    wp = jnp.concatenate([w_in[:, :, :o_la], w_in[:, :, o_lr:], pad_la], axis=-1).astype(BF16)
    assert wp.shape[-1] == PK_LA + LANES
    wa2 = jnp.pad(w_gla_a2, ((0, 0), (0, LANES - GLA_RANK), (0, 0))).astype(BF16)
    vec = lambda v: v[:, None, :]
    g_pre, g_post, g_mpre, g_mpost = vec(g_mix_pre), vec(g_mix_post), vec(g_mlp_pre), vec(g_mlp_post)
    bg, ba, gn = vec(b_gate), vec(b_gla_a), vec(g_gla_norm)
    bdw, gln, bln = vec(b_dw), vec(g_conv_ln), vec(b_conv_ln)
    out_ws = (w_moba_o, w_gla_o, w_conv_o, w_mix_out)

    for l in range(depth):
        mq, mk, mv, ob, oc, wa, wb, wc, wo = _front(l, h, g_pre, wp, wa2, ba, cos, sin, w_dw, bdw, gln, bln, gn,
                                                    out_ws, batch, seq)
        oa = _moba_attn(mq, mk, mv, batch, seq)
        h, wu, wd = _combine(l, h, g_pre, wp, bg, oa, ob, oc, wa, wb, wc, wo, g_post, w_mlp_up, w_mlp_down)
        h = _mlp(l, h, g_mpre, wu, wd, g_mpost)
    return h.reshape(batch, seq, D_MODEL)
```

```python
import functools

import jax
import jax.numpy as jnp
from jax import lax
from jax.experimental import pallas as pl
from jax.experimental.pallas import tpu as pltpu

D_MODEL = 1024
MOBA_HEADS = 8
MOBA_HEAD_DIM = 64
MOBA_WIDTH = MOBA_HEADS * MOBA_HEAD_DIM
MOBA_BLOCK = 256
MOBA_TOPK = 3
ROPE_THETA = 10000.0
GLA_HEADS = 4
GLA_DK = 64
GLA_DV = 128
GLA_KW = GLA_HEADS * GLA_DK
GLA_VW = GLA_HEADS * GLA_DV
GLA_RANK = 16
GLA_TEMP = 16.0
GLA_CHUNK = 64
CONV_CH = 512
CONV_WIDTH = 31
D_FF = 4 * D_MODEL
N_BRANCH = 3
EPS = 1e-6
NEG = -1e30
LOG2E = 1.4426950408889634
MOBA_Q_SCALE = MOBA_HEAD_DIM ** -0.5 * LOG2E

LANES = 128
SUBLANES = 8
VMEM_LIMIT_BYTES = 56 * 1024 * 1024
MOBA_VT_ROWS = MOBA_HEAD_DIM + 2 * SUBLANES

ROW_TILE = 512
COMBINE_SPLIT = 2
FRONT_TILE = 1024
GLA_TILE = 256
CONV_HALO = 32
CONV_ROW_CHUNK = 64

PK_MOBA = 0
PK_GLA = 3 * MOBA_WIDTH
PK_CONV = PK_GLA + 2 * GLA_KW + 2 * GLA_VW
PK_GATE = PK_CONV + 2 * CONV_CH
PK_LA = PK_GATE + N_BRANCH * D_MODEL

F32 = jnp.float32
BF16 = jnp.bfloat16


def _nt(a, b):
    return lax.dot_general(a, b, (((1,), (1,)), ((), ())), preferred_element_type=F32)


def _tn(a, b):
    return lax.dot_general(a, b, (((0,), (0,)), ((), ())), preferred_element_type=F32)


def _mm(a, b):
    return jnp.dot(a, b, preferred_element_type=F32)


def _rms(x, g):
    return x * lax.rsqrt(jnp.mean(x * x, axis=-1, keepdims=True) + EPS) * g


def _split3(x):
    hi = x.astype(BF16)
    r1 = x - hi.astype(F32)
    mid = r1.astype(BF16)
    lo = (r1 - mid.astype(F32)).astype(BF16)
    return hi, mid, lo


def _params(*sem):
    return pltpu.CompilerParams(dimension_semantics=sem, vmem_limit_bytes=VMEM_LIMIT_BYTES)


def _const_spec(shape):
    nd = len(shape)
    return pl.BlockSpec(shape, lambda *_: (0,) * nd)


def _layer_spec(layer, rows, cols, col_block=0):
    return pl.BlockSpec((None, rows, cols), lambda *_: (layer, 0, col_block))


def _vec_spec(layer, n):
    return _layer_spec(layer, 1, n)


def _rope_table_kernel(pos_ref, invf_ref, cos_ref, sin_ref):
    ang = pos_ref[...].astype(F32) * invf_ref[...]
    cos_ref[...] = jnp.cos(ang)
    sin_ref[...] = jnp.sin(ang)


def _rope_tables(positions):
    T = positions.size
    half = MOBA_HEAD_DIM // 2
    per_row = LANES // half
    inv_freq = ROPE_THETA ** (-jnp.arange(half, dtype=F32) / half)
    invf = jnp.tile(inv_freq, per_row)[None, :]
    pos = jnp.repeat(positions.reshape(T // per_row, per_row), half, axis=1)
    rows = T // per_row
    tile = min(ROW_TILE, rows)
    assert rows % tile == 0
    tab = jax.ShapeDtypeStruct((rows, LANES), F32)
    row = pl.BlockSpec((tile, LANES), lambda i: (i, 0))
    cos, sin = pl.pallas_call(
        _rope_table_kernel,
        grid=(rows // tile,),
        in_specs=[row, _const_spec((1, LANES))],
        out_specs=[row, row],
        out_shape=[tab, tab],
        compiler_params=_params("parallel"),
        name="rope_tables",
    )(pos, invf)
    return cos.reshape(T, half), sin.reshape(T, half)


def _rope(s, cos, sa, sb):
    return s * cos + pltpu.roll(s, LANES - 32, 1) * sa + pltpu.roll(s, 32, 1) * sb


def _gla_masks():
    ch, sc = GLA_CHUNK, GLA_TILE
    ri = lax.broadcasted_iota(jnp.int32, (sc, sc), 0)
    ci = lax.broadcasted_iota(jnp.int32, (sc, sc), 1)
    lower = (ri >= ci) & (ri // ch == ci // ch)
    tril = jnp.where(lower, 1.0, 0.0).astype(BF16)
    lane_head = lax.broadcasted_iota(jnp.int32, (sc, LANES), 1) // GLA_DK
    st_lane_head = lax.broadcasted_iota(jnp.int32, (GLA_DV, LANES), 1) // GLA_DK
    return lower, tril, lane_head, st_lane_head


def _gla_tile(q, k, v, r, d, gn, st_box, masks):
    ch, sc = GLA_CHUNK, GLA_TILE
    nc = sc // ch
    pair = LANES // GLA_DK
    lower, tril, lane_head, st_lane_head = masks
    d_hi, d_mid, d_lo = _split3(d)
    g3 = _mm(tril, jnp.concatenate([d_hi, d_mid, d_lo], axis=1))
    G = g3[:, :LANES] + g3[:, LANES:2 * LANES] + g3[:, 2 * LANES:]
    yield
    per_chunk = lambda row: jnp.concatenate(
        [jnp.broadcast_to(G[c * ch + row:c * ch + row + 1, :], (ch, LANES)) for c in range(nc)], axis=0)
    g_last = [G[(c + 1) * ch - 1:(c + 1) * ch, :] for c in range(nc)]
    g_last_rows = per_chunk(ch - 1)
    g_mid_rows = per_chunk(ch // 2 - 1)
    scale = GLA_DK ** -0.5
    qa = (q * jnp.exp(G - g_mid_rows) * scale).astype(BF16)
    kt = (k * jnp.exp(g_mid_rows - G)).astype(BF16)
    qt = (q * jnp.exp(G) * scale).astype(BF16)
    kd = (k * jnp.exp(g_last_rows - G)).astype(BF16)
    yield
    qh, vh, o_intra, upd = [], [], [], []
    for h in range(pair):
        qh.append(jnp.where(lane_head == h, qt, jnp.zeros_like(qt)))
        qah = jnp.where(lane_head == h, qa, jnp.zeros_like(qa))
        a = jnp.where(lower, _nt(qah, kt), 0.0).astype(BF16)
        vh.append(v[:, h * GLA_DV:(h + 1) * GLA_DV])
        o_intra.append(_mm(a, vh[h]))
        upd.append([_tn(vh[h][c * ch:(c + 1) * ch], kd[c * ch:(c + 1) * ch]) for c in range(nc)])
        yield
    o_inter = [[] for _ in range(pair)]
    st = st_box[0]
    for c in range(nc):
        st_b = st.astype(BF16)
        u = upd[0][c]
        for h in range(pair):
            o_inter[h].append(_nt(qh[h][c * ch:(c + 1) * ch], st_b))
            if h:
                u = jnp.where(st_lane_head == h, upd[h][c], u)
        st = st * jnp.exp(g_last[c]) + u
    st_box[0] = st
    yield
    outs = []
    for h in range(pair):
        o = _rms(o_intra[h] + jnp.concatenate(o_inter[h], axis=0), gn)
        rh = r[:, h * GLA_DV:(h + 1) * GLA_DV]
        outs.append(o * (rh * jax.nn.sigmoid(rh)))
        yield
    return outs


def _drive(*gens):
    results = [None] * len(gens)
    live = list(enumerate(gens))
    while live:
        for item in list(live):
            n, gen = item
            try:
                next(gen)
            except StopIteration as stop:
                results[n] = stop.value
                live.remove(item)
    return results


def _front_kernel(x_ref, g_ref, w_ref, wla_ref, wa2_ref, ba_ref, cos_ref, sin_ref,
                  wdw_ref, bdw_ref, gln_ref, bln_ref, gn_ref, wa32_ref, wb32_ref, wc32_ref, wo32_ref,
                  mq_ref, mk_ref, mv_ref, ob_ref, oc_ref, wa_ref, wb_ref, wc_ref, wo_ref,
                  u_ref, us_ref, y_ref, st_ref):
    ts, halo = FRONT_TILE, CONV_HALO
    i = pl.program_id(1)
    for src, dst in ((wa32_ref, wa_ref), (wb32_ref, wb_ref), (wc32_ref, wc_ref), (wo32_ref, wo_ref)):
        dst[...] = src[...].astype(BF16)

    @pl.when(i == 0)
    def _():
        u_ref[0:halo, :] = jnp.zeros((halo, CONV_CH), F32)
        st_ref[...] = jnp.zeros_like(st_ref)

    @pl.when(i > 0)
    def _():
        u_ref[0:halo, :] = u_ref[ts:ts + halo, :]

    xn = _rms(x_ref[...], g_ref[...]).astype(BF16)

    zc = _mm(xn, w_ref[:, PK_CONV:PK_CONV + 2 * CONV_CH])
    u_ref[halo:halo + ts, :] = zc[:, :CONV_CH] * jax.nn.sigmoid(zc[:, CONV_CH:])
    span = us_ref.shape[1]
    for b in range(1, SUBLANES):
        us_ref[b - 1] = u_ref[b:b + span, :]

    zm = _mm(xn, w_ref[:, PK_MOBA:PK_MOBA + 3 * MOBA_WIDTH])
    reps = LANES // cos_ref.shape[1]
    cos = jnp.tile(cos_ref[...], (1, reps))
    sin = jnp.tile(sin_ref[...], (1, reps))
    first_half = (lax.broadcasted_iota(jnp.int32, sin.shape, 1) % MOBA_HEAD_DIM) < (MOBA_HEAD_DIM // 2)
    sa = jnp.where(first_half, -sin, 0.0)
    sb = jnp.where(first_half, 0.0, sin)
    for c in range(MOBA_WIDTH // LANES):
        cols = slice(c * LANES, (c + 1) * LANES)
        mq_ref[c] = (_rope(zm[:, cols], cos, sa, sb) * MOBA_Q_SCALE).astype(BF16)
        mk_ref[c] = _rope(zm[:, MOBA_WIDTH + c * LANES:MOBA_WIDTH + (c + 1) * LANES], cos, sa, sb).astype(BF16)
        mv_ref[c] = zm[:, 2 * MOBA_WIDTH + c * LANES:2 * MOBA_WIDTH + (c + 1) * LANES].astype(BF16)

    zg = _mm(xn, w_ref[:, PK_GLA:PK_GLA + 2 * GLA_KW + 2 * GLA_VW])
    la = _mm(xn, wla_ref[...])
    pre = _mm(la.astype(BF16), wa2_ref[...]) + ba_ref[...]
    dec = (jnp.minimum(pre, 0.0) - jnp.log1p(jnp.exp(-jnp.abs(pre)))) / GLA_TEMP

    def conv():
        first = halo - (CONV_WIDTH - 1)
        for c in range(CONV_CH // LANES):
            cols = slice(c * LANES, (c + 1) * LANES)
            for r in range(0, ts, CONV_ROW_CHUNK):
                acc = jnp.broadcast_to(bdw_ref[:, cols], (CONV_ROW_CHUNK, LANES))
                for w in range(CONV_WIDTH):
                    a, b = divmod(first + w, SUBLANES)
                    lo = r + a * SUBLANES
                    tap = (u_ref[lo:lo + CONV_ROW_CHUNK, cols] if b == 0
                           else us_ref[b - 1, lo:lo + CONV_ROW_CHUNK, cols])
                    acc = acc + wdw_ref[w:w + 1, cols] * tap
                y_ref[r:r + CONV_ROW_CHUNK, cols] = acc
                yield

    masks = _gla_masks()
    gn = gn_ref[...]
    pair_dv = (LANES // GLA_DK) * GLA_DV
    n_pairs, n_sub = GLA_KW // LANES, ts // GLA_TILE
    st_boxes = [[st_ref[p]] for p in range(n_pairs)]
    glas = []
    for t in range(n_sub):
        rows = slice(t * GLA_TILE, (t + 1) * GLA_TILE)
        for p in range(n_pairs):
            kl = slice(p * LANES, (p + 1) * LANES)
            vl = slice(2 * GLA_KW + p * pair_dv, 2 * GLA_KW + (p + 1) * pair_dv)
            rl = slice(2 * GLA_KW + GLA_VW + p * pair_dv, 2 * GLA_KW + GLA_VW + (p + 1) * pair_dv)
            glas.append(_gla_tile(zg[rows, kl], zg[rows, GLA_KW + p * LANES:GLA_KW + (p + 1) * LANES],
                                  zg[rows, vl].astype(BF16), zg[rows, rl], dec[rows, kl], gn, st_boxes[p], masks))
    res = _drive(*glas, conv())
    for t in range(n_sub):
        for p in range(n_pairs):
            for h, o in enumerate(res[t * n_pairs + p]):
                ob_ref[t * GLA_TILE:(t + 1) * GLA_TILE, p * pair_dv + h * GLA_DV:p * pair_dv + (h + 1) * GLA_DV] = o.astype(BF16)
    for p in range(n_pairs):
        st_ref[p] = st_boxes[p][0]

    y = y_ref[...]
    mu = jnp.mean(y, axis=-1, keepdims=True)
    yc = y - mu
    var = jnp.mean(yc * yc, axis=-1, keepdims=True)
    yn = yc * lax.rsqrt(var + EPS) * gln_ref[...] + bln_ref[...]
    oc_ref[...] = (yn * jax.nn.sigmoid(yn)).astype(BF16)


def _front(layer, x, g, wp, wa2, ba, cos, sin, wdw, bdw, gln, bln, gn, out_ws, batch, seq):
    T = x.shape[0]
    nt = seq // FRONT_TILE
    row = lambda n: pl.BlockSpec((FRONT_TILE, n), lambda b, i: (b * nt + i, 0))
    sds = lambda n: jax.ShapeDtypeStruct((T, n), BF16)
    assert PK_MOBA == 0 and PK_GATE % LANES == 0
    steps = batch * nt
    slab_rows = [w.shape[1] // steps for w in out_ws]
    assert all(r % (2 * SUBLANES) == 0 for r in slab_rows)
    slab_in = [pl.BlockSpec((None, r, D_MODEL), lambda b, i: (layer, b * nt + i, 0)) for r in slab_rows]
    slab_out = [pl.BlockSpec((r, D_MODEL), lambda b, i: (b * nt + i, 0)) for r in slab_rows]
    n_pairs = MOBA_WIDTH // LANES
    qkv = jax.ShapeDtypeStruct((n_pairs, T, LANES), BF16)
    pair_major = pl.BlockSpec((n_pairs, FRONT_TILE, LANES), lambda b, i: (0, b * nt + i, 0))
    return pl.pallas_call(
        _front_kernel,
        grid=(batch, nt),
        in_specs=[row(D_MODEL), _vec_spec(layer, D_MODEL), _layer_spec(layer, D_MODEL, PK_GATE, 0),
                  _layer_spec(layer, D_MODEL, LANES, PK_LA // LANES), _layer_spec(layer, LANES, GLA_KW),
                  _vec_spec(layer, GLA_KW), row(MOBA_HEAD_DIM // 2), row(MOBA_HEAD_DIM // 2),
                  _layer_spec(layer, CONV_WIDTH, CONV_CH),
                  _vec_spec(layer, CONV_CH), _vec_spec(layer, CONV_CH), _vec_spec(layer, CONV_CH),
                  _vec_spec(layer, GLA_DV)] + slab_in,
        out_specs=[pair_major, pair_major, pair_major, row(GLA_VW), row(CONV_CH)] + slab_out,
        out_shape=[qkv, qkv, qkv, sds(GLA_VW), sds(CONV_CH)]
        + [jax.ShapeDtypeStruct(w.shape[1:], BF16) for w in out_ws],
        scratch_shapes=[pltpu.VMEM((FRONT_TILE + CONV_HALO, CONV_CH), F32),
                        pltpu.VMEM((SUBLANES - 1, FRONT_TILE + CONV_HALO - SUBLANES, CONV_CH), F32),
                        pltpu.VMEM((FRONT_TILE, CONV_CH), F32),
                        pltpu.VMEM((GLA_KW // LANES, GLA_DV, LANES), F32)],
        compiler_params=_params("parallel", "arbitrary"),
        name="front",
    )(x, g, wp, wp, wa2, ba, cos, sin, wdw, bdw, gln, bln, gn, *out_ws)


def _moba_attn_kernel(q_ref, k_ref, v_ref, o_ref, kmean_ref, vt_ref, s_ref, *, n_blocks):
    blk = MOBA_BLOCK
    pair = LANES // MOBA_HEAD_DIM
    hd = MOBA_HEAD_DIM
    kmean_ref[...] = jnp.zeros_like(kmean_ref)
    ones_row = lax.broadcasted_iota(jnp.int32, (MOBA_VT_ROWS - hd, blk), 0) == 0
    ones_pad = jnp.where(ones_row, 1.0, 0.0).astype(BF16)
    for j in range(n_blocks):
        kj = k_ref[j * blk:(j + 1) * blk, :].astype(F32)
        kmean_ref[j:j + 1, :] = jnp.mean(kj, axis=0, keepdims=True)
        vt = v_ref[j * blk:(j + 1) * blk, :].astype(F32).T.astype(BF16)
        for h in range(pair):
            vt_ref[j, h, 0:hd, :] = vt[h * hd:(h + 1) * hd, :]
            vt_ref[j, h, hd:, :] = ones_pad

    kpos = lax.broadcasted_iota(jnp.int32, (blk, blk), 0)
    qpos = lax.broadcasted_iota(jnp.int32, (blk, blk), 1)
    causal = kpos <= qpos
    qlane = lax.broadcasted_iota(jnp.int32, (blk, LANES), 1) // hd
    sub = lambda t: t.reshape(blk // SUBLANES, SUBLANES, blk)
    col_max = {}

    def scores(ii):
        q = q_ref[ii * blk:(ii + 1) * blk, :]
        need_gate = ii > MOBA_TOPK
        if need_gate:
            km_hi, km_mid, km_lo = _split3(kmean_ref[...])
            jidx = lax.broadcasted_iota(jnp.int32, (kmean_ref.shape[0], blk), 0)
        for h in range(pair):
            qh = jnp.where(qlane == h, q, jnp.zeros_like(q))
            valid = [None] * ii
            if need_gate:
                gate = _nt(km_hi, qh) + _nt(km_mid, qh) + _nt(km_lo, qh)
                for n in range(ii):
                    gn = gate[n:n + 1, :]
                    beats = ((gate > gn) | ((gate == gn) & (jidx < n))) & (jidx < ii)
                    rank = jnp.sum(jnp.where(beats, 1.0, 0.0), axis=0, keepdims=True)
                    valid[n] = rank < MOBA_TOPK
            m8 = None
            for j in range(ii + 1):
                s = _nt(k_ref[j * blk:(j + 1) * blk, :], qh)
                if j == ii:
                    s = jnp.where(causal, s, NEG)
                elif valid[j] is not None:
                    s = jnp.where(valid[j], s, NEG)
                s_ref[ii % 2, h, j] = s
                t = jnp.max(sub(s), axis=0)
                m8 = t if m8 is None else jnp.maximum(m8, t)
                yield
            col_max[ii, h] = jnp.max(m8, axis=0, keepdims=True)

    def values(ii):
        outs = []
        for h in range(pair):
            acc = None
            for j in range(ii + 1):
                p = jnp.exp2(s_ref[ii % 2, h, j] - col_max[ii, h])
                t = _mm(vt_ref[j, h], p.astype(BF16))
                acc = t if acc is None else acc + t
                yield
            outs.append(acc[0:hd, :] / acc[hd:hd + 1, :])
        o_ref[ii * blk:(ii + 1) * blk, :] = jnp.concatenate(outs, axis=0).T.astype(BF16)

    for _ in scores(0):
        pass
    for ii in range(n_blocks):
        live = [values(ii)]
        if ii + 1 < n_blocks:
            live.insert(0, scores(ii + 1))
        while live:
            for gen in list(live):
                if next(gen, StopIteration) is StopIteration:
                    live.remove(gen)


def _moba_attn(q, k, v, batch, seq):
    n_blocks = seq // MOBA_BLOCK
    n_pairs = MOBA_WIDTH // LANES
    pair = LANES // MOBA_HEAD_DIM
    in_spec = pl.BlockSpec((None, seq, LANES), lambda b, p: (p, b, 0))
    out_spec = pl.BlockSpec((seq, LANES), lambda b, p: (b, p))
    gate_rows = 2 * SUBLANES
    assert n_blocks <= gate_rows
    return pl.pallas_call(
        functools.partial(_moba_attn_kernel, n_blocks=n_blocks),
        grid=(batch, n_pairs),
        in_specs=[in_spec, in_spec, in_spec],
        out_specs=out_spec,
        out_shape=jax.ShapeDtypeStruct((batch * seq, MOBA_WIDTH), BF16),
        scratch_shapes=[
            pltpu.VMEM((gate_rows, LANES), F32),
            pltpu.VMEM((n_blocks, pair, MOBA_VT_ROWS, MOBA_BLOCK), BF16),
            pltpu.VMEM((2, pair, n_blocks, MOBA_BLOCK, MOBA_BLOCK), F32),
        ],
        compiler_params=_params("parallel", "parallel"),
        name="moba_attn",
    )(q, k, v)


def _combine_kernel(x_ref, g_ref, wga_ref, wgb_ref, wgc_ref, bg_ref, a_ref, b_ref, c_ref, wa_ref, wb_ref,
                    wc_ref, wo_ref, gp_ref, wu32_ref, wd32_ref, o_ref, wu_ref, wd_ref):
    wu_ref[...] = wu32_ref[...].astype(BF16)
    wd_ref[...] = wd32_ref[...].astype(BF16)
    branches = ((wga_ref, a_ref, wa_ref), (wgb_ref, b_ref, wb_ref), (wgc_ref, c_ref, wc_ref))
    group = x_ref.shape[0] // COMBINE_SPLIT
    for r in range(COMBINE_SPLIT):
        rows = slice(r * group, (r + 1) * group)
        x = x_ref[rows, :]
        xn = _rms(x, g_ref[...]).astype(BF16)
        mix = None
        for n, (wg_ref, br_ref, wbr_ref) in enumerate(branches):
            cols = slice(n * D_MODEL, (n + 1) * D_MODEL)
            gate = jax.nn.sigmoid(_mm(xn, wg_ref[...]) + bg_ref[:, cols])
            term = gate * _mm(br_ref[rows, :], wbr_ref[...])
            mix = term if mix is None else mix + term
        y = _mm(mix.astype(BF16), wo_ref[...])
        o_ref[rows, :] = x + _rms(y, gp_ref[...])


def _combine(layer, x, g, wp, bg, a, b, c, wa, wb, wc, wo, gp, wu32, wd32):
    T = x.shape[0]
    slab = D_FF // (T // ROW_TILE)
    assert slab % LANES == 0
    row = lambda n: pl.BlockSpec((ROW_TILE, n), lambda i: (i, 0))
    gate_w = lambda n: _layer_spec(layer, D_MODEL, D_MODEL, PK_GATE // D_MODEL + n)
    return pl.pallas_call(
        _combine_kernel,
        grid=(T // ROW_TILE,),
        in_specs=[row(D_MODEL), _vec_spec(layer, D_MODEL), gate_w(0), gate_w(1), gate_w(2),
                  _vec_spec(layer, N_BRANCH * D_MODEL),
                  row(MOBA_WIDTH), row(GLA_VW), row(CONV_CH),
                  _const_spec(wa.shape), _const_spec(wb.shape), _const_spec(wc.shape), _const_spec(wo.shape),
                  _vec_spec(layer, D_MODEL),
                  pl.BlockSpec((None, D_MODEL, slab), lambda i: (layer, 0, i)),
                  pl.BlockSpec((None, slab, D_MODEL), lambda i: (layer, i, 0))],
        out_specs=[row(D_MODEL), pl.BlockSpec((D_MODEL, slab), lambda i: (0, i)),
                   pl.BlockSpec((slab, D_MODEL), lambda i: (i, 0))],
        out_shape=[jax.ShapeDtypeStruct((T, D_MODEL), F32), jax.ShapeDtypeStruct((D_MODEL, D_FF), BF16),
                   jax.ShapeDtypeStruct((D_FF, D_MODEL), BF16)],
        compiler_params=_params("parallel"),
        name="combine",
    )(x, g, wp, wp, wp, bg, a, b, c, wa, wb, wc, wo, gp, wu32, wd32)


def _mlp_kernel(x_ref, g_ref, wu_ref, wd_ref, gp_ref, o_ref):
    x = x_ref[...]
    xn = _rms(x, g_ref[...]).astype(BF16)
    acc = None
    for c in range(D_FF // D_MODEL):
        cols = slice(c * D_MODEL, (c + 1) * D_MODEL)
        u = jnp.maximum(_mm(xn, wu_ref[:, cols]), 0.0)
        t = _mm((u * u).astype(BF16), wd_ref[cols, :])
        acc = t if acc is None else acc + t
    o_ref[...] = x + _rms(acc, gp_ref[...])


def _mlp(layer, x, g, wu, wd, gp):
    T = x.shape[0]
    row = pl.BlockSpec((ROW_TILE, D_MODEL), lambda i: (i, 0))
    return pl.pallas_call(
        _mlp_kernel,
        grid=(T // ROW_TILE,),
        in_specs=[row, _vec_spec(layer, D_MODEL), _const_spec((D_MODEL, D_FF)), _const_spec((D_FF, D_MODEL)),
                  _vec_spec(layer, D_MODEL)],
        out_specs=row,
        out_shape=jax.ShapeDtypeStruct((T, D_MODEL), F32),
        compiler_params=_params("parallel"),
        name="mlp",
    )(x, g, wu, wd, gp)


def kernel(x, positions, g_mix_pre, w_in, b_gate, w_gla_a2, b_gla_a, g_gla_norm, w_dw, b_dw, g_conv_ln,
           b_conv_ln, w_moba_o, w_gla_o, w_conv_o, w_mix_out, g_mix_post, g_mlp_pre, w_mlp_up, w_mlp_down,
           g_mlp_post):
    batch, seq, _ = x.shape
    depth = w_in.shape[0]
    assert seq % MOBA_BLOCK == 0 and seq % FRONT_TILE == 0 and (batch * seq) % ROW_TILE == 0
    assert FRONT_TILE % GLA_TILE == 0 and GLA_TILE % GLA_CHUNK == 0
    h = x.reshape(batch * seq, D_MODEL)
    cos, sin = _rope_tables(positions)

    o_la = 3 * MOBA_WIDTH + 2 * GLA_KW + GLA_VW
    o_lr = o_la + GLA_RANK
    pad_la = jnp.pad(w_in[:, :, o_la:o_lr], ((0, 0), (0, 0), (0, LANES - GLA_RANK)))
    wp = jnp.concatenate([w_in[:, :, :o_la], w_in[:, :, o_lr:], pad_la], axis=-1).astype(BF16)
    assert wp.shape[-1] == PK_LA + LANES
    wa2 = jnp.pad(w_gla_a2, ((0, 0), (0, LANES - GLA_RANK), (0, 0))).astype(BF16)
    vec = lambda v: v[:, None, :]
    g_pre, g_post, g_mpre, g_mpost = vec(g_mix_pre), vec(g_mix_post), vec(g_mlp_pre), vec(g_mlp_post)
    bg, ba, gn = vec(b_gate), vec(b_gla_a), vec(g_gla_norm)
    bdw, gln, bln = vec(b_dw), vec(g_conv_ln), vec(b_conv_ln)
    out_ws = (w_moba_o, w_gla_o, w_conv_o, w_mix_out)

    for l in range(depth):
        mq, mk, mv, ob, oc, wa, wb, wc, wo = _front(l, h, g_pre, wp, wa2, ba, cos, sin, w_dw, bdw, gln, bln, gn,
                                                    out_ws, batch, seq)
        oa = _moba_attn(mq, mk, mv, batch, seq)
        h, wu, wd = _combine(l, h, g_pre, wp, bg, oa, ob, oc, wa, wb, wc, wo, g_post, w_mlp_up, w_mlp_down)
        h = _mlp(l, h, g_mpre, wu, wd, g_mpost)
    return h.reshape(batch, seq, D_MODEL)
```

```python
import functools

import jax
import jax.numpy as jnp
from jax import lax
from jax.experimental import pallas as pl
from jax.experimental.pallas import tpu as pltpu

D_MODEL = 1024
MOBA_HEADS = 8
MOBA_HEAD_DIM = 64
MOBA_WIDTH = MOBA_HEADS * MOBA_HEAD_DIM
MOBA_BLOCK = 256
MOBA_TOPK = 3
ROPE_THETA = 10000.0
GLA_HEADS = 4
GLA_DK = 64
GLA_DV = 128
GLA_KW = GLA_HEADS * GLA_DK
GLA_VW = GLA_HEADS * GLA_DV
GLA_RANK = 16
GLA_TEMP = 16.0
GLA_CHUNK = 64
CONV_CH = 512
CONV_WIDTH = 31
D_FF = 4 * D_MODEL
N_BRANCH = 3
EPS = 1e-6
NEG = -1e30
LOG2E = 1.4426950408889634
MOBA_Q_SCALE = MOBA_HEAD_DIM ** -0.5 * LOG2E

LANES = 128
SUBLANES = 8
VMEM_LIMIT_BYTES = 56 * 1024 * 1024
MOBA_VT_ROWS = MOBA_HEAD_DIM + 2 * SUBLANES

ROW_TILE = 512
COMBINE_SPLIT = 2
FRONT_TILE = 1024
GLA_TILE = 256
CONV_HALO = 32
CONV_ROW_CHUNK = 64

PK_MOBA = 0
PK_GLA = 3 * MOBA_WIDTH
PK_CONV = PK_GLA + 2 * GLA_KW + 2 * GLA_VW
PK_GATE = PK_CONV + 2 * CONV_CH
PK_LA = PK_GATE + N_BRANCH * D_MODEL

F32 = jnp.float32
BF16 = jnp.bfloat16


def _nt(a, b):
    return lax.dot_general(a, b, (((1,), (1,)), ((), ())), preferred_element_type=F32)


def _tn(a, b):
    return lax.dot_general(a, b, (((0,), (0,)), ((), ())), preferred_element_type=F32)


def _mm(a, b):
    return jnp.dot(a, b, preferred_element_type=F32)


def _sigmoid(x):
    return 0.5 * jnp.tanh(0.5 * x) + 0.5


def _rms(x, g):
    return x * lax.rsqrt(jnp.mean(x * x, axis=-1, keepdims=True) + EPS) * g


def _split3(x):
    hi = x.astype(BF16)
    r1 = x - hi.astype(F32)
    mid = r1.astype(BF16)
    lo = (r1 - mid.astype(F32)).astype(BF16)
    return hi, mid, lo


def _params(*sem):
    return pltpu.CompilerParams(dimension_semantics=sem, vmem_limit_bytes=VMEM_LIMIT_BYTES)


def _const_spec(shape):
    nd = len(shape)
    return pl.BlockSpec(shape, lambda *_: (0,) * nd)


def _layer_spec(layer, rows, cols, col_block=0):
    return pl.BlockSpec((None, rows, cols), lambda *_: (layer, 0, col_block))


def _vec_spec(layer, n):
    return _layer_spec(layer, 1, n)


def _rope_table_kernel(pos_ref, invf_ref, cos_ref, sin_ref):
    ang = pos_ref[...].astype(F32) * invf_ref[...]
    cos_ref[...] = jnp.cos(ang)
    sin_ref[...] = jnp.sin(ang)


def _rope_tables(positions):
    T = positions.size
    half = MOBA_HEAD_DIM // 2
    per_row = LANES // half
    inv_freq = ROPE_THETA ** (-jnp.arange(half, dtype=F32) / half)
    invf = jnp.tile(inv_freq, per_row)[None, :]
    pos = jnp.repeat(positions.reshape(T // per_row, per_row), half, axis=1)
    rows = T // per_row
    tile = min(ROW_TILE, rows)
    assert rows % tile == 0
    tab = jax.ShapeDtypeStruct((rows, LANES), F32)
    row = pl.BlockSpec((tile, LANES), lambda i: (i, 0))
    cos, sin = pl.pallas_call(
        _rope_table_kernel,
        grid=(rows // tile,),
        in_specs=[row, _const_spec((1, LANES))],
        out_specs=[row, row],
        out_shape=[tab, tab],
        compiler_params=_params("parallel"),
        name="rope_tables",
    )(pos, invf)
    return cos.reshape(T, half), sin.reshape(T, half)


def _rope(s, cos, sa, sb):
    return s * cos + pltpu.roll(s, LANES - 32, 1) * sa + pltpu.roll(s, 32, 1) * sb


def _gla_masks():
    ch, sc = GLA_CHUNK, GLA_TILE
    ri = lax.broadcasted_iota(jnp.int32, (sc, sc), 0)
    ci = lax.broadcasted_iota(jnp.int32, (sc, sc), 1)
    lower = (ri >= ci) & (ri // ch == ci // ch)
    tril = jnp.where(lower, 1.0, 0.0).astype(BF16)
    lane_head = lax.broadcasted_iota(jnp.int32, (sc, LANES), 1) // GLA_DK
    st_lane_head = lax.broadcasted_iota(jnp.int32, (GLA_DV, LANES), 1) // GLA_DK
    return lower, tril, lane_head, st_lane_head


def _gla_tile(q, k, v, r, d, gn, st_box, masks):
    ch, sc = GLA_CHUNK, GLA_TILE
    nc = sc // ch
    pair = LANES // GLA_DK
    lower, tril, lane_head, st_lane_head = masks
    d_hi, d_mid, d_lo = _split3(d)
    g3 = _mm(tril, jnp.concatenate([d_hi, d_mid, d_lo], axis=1))
    G = g3[:, :LANES] + g3[:, LANES:2 * LANES] + g3[:, 2 * LANES:]
    yield
    per_chunk = lambda row: jnp.concatenate(
        [jnp.broadcast_to(G[c * ch + row:c * ch + row + 1, :], (ch, LANES)) for c in range(nc)], axis=0)
    g_last = [G[(c + 1) * ch - 1:(c + 1) * ch, :] for c in range(nc)]
    g_last_rows = per_chunk(ch - 1)
    g_mid_rows = per_chunk(ch // 2 - 1)
    scale = GLA_DK ** -0.5
    qa = (q * jnp.exp(G - g_mid_rows) * scale).astype(BF16)
    kt = (k * jnp.exp(g_mid_rows - G)).astype(BF16)
    qt = (q * jnp.exp(G) * scale).astype(BF16)
    kd = (k * jnp.exp(g_last_rows - G)).astype(BF16)
    yield
    qh, vh, o_intra, upd = [], [], [], []
    for h in range(pair):
        qh.append(jnp.where(lane_head == h, qt, jnp.zeros_like(qt)))
        qah = jnp.where(lane_head == h, qa, jnp.zeros_like(qa))
        a = jnp.where(lower, _nt(qah, kt), 0.0).astype(BF16)
        vh.append(v[:, h * GLA_DV:(h + 1) * GLA_DV])
        o_intra.append(_mm(a, vh[h]))
        upd.append([_tn(vh[h][c * ch:(c + 1) * ch], kd[c * ch:(c + 1) * ch]) for c in range(nc)])
        yield
    o_inter = [[] for _ in range(pair)]
    st = st_box[0]
    for c in range(nc):
        st_b = st.astype(BF16)
        u = upd[0][c]
        for h in range(pair):
            o_inter[h].append(_nt(qh[h][c * ch:(c + 1) * ch], st_b))
            if h:
                u = jnp.where(st_lane_head == h, upd[h][c], u)
        st = st * jnp.exp(g_last[c]) + u
    st_box[0] = st
    yield
    outs = []
    for h in range(pair):
        o = _rms(o_intra[h] + jnp.concatenate(o_inter[h], axis=0), gn)
        rh = r[:, h * GLA_DV:(h + 1) * GLA_DV]
        outs.append(o * (rh * _sigmoid(rh)))
        yield
    return outs


def _drive(*gens):
    results = [None] * len(gens)
    live = list(enumerate(gens))
    while live:
        for item in list(live):
            n, gen = item
            try:
                next(gen)
            except StopIteration as stop:
                results[n] = stop.value
                live.remove(item)
    return results


def _front_kernel(x_ref, g_ref, w_ref, wla_ref, wa2_ref, ba_ref, cos_ref, sin_ref,
                  wdw_ref, bdw_ref, gln_ref, bln_ref, gn_ref, wa32_ref, wb32_ref, wc32_ref, wo32_ref,
                  mq_ref, mk_ref, mv_ref, ob_ref, oc_ref, wa_ref, wb_ref, wc_ref, wo_ref,
                  u_ref, us_ref, y_ref, st_ref):
    ts, halo = FRONT_TILE, CONV_HALO
    i = pl.program_id(1)
    for src, dst in ((wa32_ref, wa_ref), (wb32_ref, wb_ref), (wc32_ref, wc_ref), (wo32_ref, wo_ref)):
        dst[...] = src[...].astype(BF16)

    @pl.when(i == 0)
    def _():
        u_ref[0:halo, :] = jnp.zeros((halo, CONV_CH), F32)
        st_ref[...] = jnp.zeros_like(st_ref)

    @pl.when(i > 0)
    def _():
        u_ref[0:halo, :] = u_ref[ts:ts + halo, :]

    xn = _rms(x_ref[...], g_ref[...]).astype(BF16)

    zc = _mm(xn, w_ref[:, PK_CONV:PK_CONV + 2 * CONV_CH])
    u_ref[halo:halo + ts, :] = zc[:, :CONV_CH] * _sigmoid(zc[:, CONV_CH:])
    span = us_ref.shape[1]
    for b in range(1, SUBLANES):
        us_ref[b - 1] = u_ref[b:b + span, :]

    zm = _mm(xn, w_ref[:, PK_MOBA:PK_MOBA + 3 * MOBA_WIDTH])
    reps = LANES // cos_ref.shape[1]
    cos = jnp.tile(cos_ref[...], (1, reps))
    sin = jnp.tile(sin_ref[...], (1, reps))
    first_half = (lax.broadcasted_iota(jnp.int32, sin.shape, 1) % MOBA_HEAD_DIM) < (MOBA_HEAD_DIM // 2)
    sa = jnp.where(first_half, -sin, 0.0)
    sb = jnp.where(first_half, 0.0, sin)
    for c in range(MOBA_WIDTH // LANES):
        cols = slice(c * LANES, (c + 1) * LANES)
        mq_ref[:, cols] = (_rope(zm[:, cols], cos, sa, sb) * MOBA_Q_SCALE).astype(BF16)
        mk_ref[:, cols] = _rope(zm[:, MOBA_WIDTH + c * LANES:MOBA_WIDTH + (c + 1) * LANES], cos, sa, sb).astype(BF16)
    mv_ref[...] = zm[:, 2 * MOBA_WIDTH:].astype(BF16)

    zg = _mm(xn, w_ref[:, PK_GLA:PK_GLA + 2 * GLA_KW + 2 * GLA_VW])
    la = _mm(xn, wla_ref[...])
    pre = _mm(la.astype(BF16), wa2_ref[...]) + ba_ref[...]
    dec = (jnp.minimum(pre, 0.0) - jnp.log1p(jnp.exp(-jnp.abs(pre)))) / GLA_TEMP

    def conv():
        first = halo - (CONV_WIDTH - 1)
        for c in range(CONV_CH // LANES):
            cols = slice(c * LANES, (c + 1) * LANES)
            for r in range(0, ts, CONV_ROW_CHUNK):
                acc = jnp.broadcast_to(bdw_ref[:, cols], (CONV_ROW_CHUNK, LANES))
                for w in range(CONV_WIDTH):
                    a, b = divmod(first + w, SUBLANES)
                    lo = r + a * SUBLANES
                    tap = (u_ref[lo:lo + CONV_ROW_CHUNK, cols] if b == 0
                           else us_ref[b - 1, lo:lo + CONV_ROW_CHUNK, cols])
                    acc = acc + wdw_ref[w:w + 1, cols] * tap
                y_ref[r:r + CONV_ROW_CHUNK, cols] = acc
                yield

    masks = _gla_masks()
    gn = gn_ref[...]
    pair_dv = (LANES // GLA_DK) * GLA_DV
    n_pairs, n_sub = GLA_KW // LANES, ts // GLA_TILE
    st_boxes = [[st_ref[p]] for p in range(n_pairs)]
    glas = []
    for t in range(n_sub):
        rows = slice(t * GLA_TILE, (t + 1) * GLA_TILE)
        for p in range(n_pairs):
            kl = slice(p * LANES, (p + 1) * LANES)
            vl = slice(2 * GLA_KW + p * pair_dv, 2 * GLA_KW + (p + 1) * pair_dv)
            rl = slice(2 * GLA_KW + GLA_VW + p * pair_dv, 2 * GLA_KW + GLA_VW + (p + 1) * pair_dv)
            glas.append(_gla_tile(zg[rows, kl], zg[rows, GLA_KW + p * LANES:GLA_KW + (p + 1) * LANES],
                                  zg[rows, vl].astype(BF16), zg[rows, rl], dec[rows, kl], gn, st_boxes[p], masks))
    res = _drive(*glas, conv())
    for t in range(n_sub):
        for p in range(n_pairs):
            for h, o in enumerate(res[t * n_pairs + p]):
                ob_ref[t * GLA_TILE:(t + 1) * GLA_TILE, p * pair_dv + h * GLA_DV:p * pair_dv + (h + 1) * GLA_DV] = o.astype(BF16)
    for p in range(n_pairs):
        st_ref[p] = st_boxes[p][0]

    y = y_ref[...]
    mu = jnp.mean(y, axis=-1, keepdims=True)
    yc = y - mu
    var = jnp.mean(yc * yc, axis=-1, keepdims=True)
    yn = yc * lax.rsqrt(var + EPS) * gln_ref[...] + bln_ref[...]
    oc_ref[...] = (yn * _sigmoid(yn)).astype(BF16)


def _front(layer, x, g, wp, wa2, ba, cos, sin, wdw, bdw, gln, bln, gn, out_ws, batch, seq):
    T = x.shape[0]
    nt = seq // FRONT_TILE
    row = lambda n: pl.BlockSpec((FRONT_TILE, n), lambda b, i: (b * nt + i, 0))
    sds = lambda n: jax.ShapeDtypeStruct((T, n), BF16)
    assert PK_MOBA == 0 and PK_GATE % LANES == 0
    steps = batch * nt
    slab_rows = [w.shape[1] // steps for w in out_ws]
    assert all(r % (2 * SUBLANES) == 0 for r in slab_rows)
    slab_in = [pl.BlockSpec((None, r, D_MODEL), lambda b, i: (layer, b * nt + i, 0)) for r in slab_rows]
    slab_out = [pl.BlockSpec((r, D_MODEL), lambda b, i: (b * nt + i, 0)) for r in slab_rows]
    return pl.pallas_call(
        _front_kernel,
        grid=(batch, nt),
        in_specs=[row(D_MODEL), _vec_spec(layer, D_MODEL), _layer_spec(layer, D_MODEL, PK_GATE, 0),
                  _layer_spec(layer, D_MODEL, LANES, PK_LA // LANES), _layer_spec(layer, LANES, GLA_KW),
                  _vec_spec(layer, GLA_KW), row(MOBA_HEAD_DIM // 2), row(MOBA_HEAD_DIM // 2),
                  _layer_spec(layer, CONV_WIDTH, CONV_CH),
                  _vec_spec(layer, CONV_CH), _vec_spec(layer, CONV_CH), _vec_spec(layer, CONV_CH),
                  _vec_spec(layer, GLA_DV)] + slab_in,
        out_specs=[row(MOBA_WIDTH), row(MOBA_WIDTH), row(MOBA_WIDTH), row(GLA_VW), row(CONV_CH)] + slab_out,
        out_shape=[sds(MOBA_WIDTH), sds(MOBA_WIDTH), sds(MOBA_WIDTH), sds(GLA_VW), sds(CONV_CH)]
        + [jax.ShapeDtypeStruct(w.shape[1:], BF16) for w in out_ws],
        scratch_shapes=[pltpu.VMEM((FRONT_TILE + CONV_HALO, CONV_CH), F32),
                        pltpu.VMEM((SUBLANES - 1, FRONT_TILE + CONV_HALO - SUBLANES, CONV_CH), F32),
                        pltpu.VMEM((FRONT_TILE, CONV_CH), F32),
                        pltpu.VMEM((GLA_KW // LANES, GLA_DV, LANES), F32)],
        compiler_params=_params("parallel", "arbitrary"),
        name="front",
    )(x, g, wp, wp, wa2, ba, cos, sin, wdw, bdw, gln, bln, gn, *out_ws)


def _moba_attn_kernel(q_ref, k_ref, v_ref, o_ref, kmean_ref, vt_ref, s_ref, *, n_blocks):
    blk = MOBA_BLOCK
    pair = LANES // MOBA_HEAD_DIM
    hd = MOBA_HEAD_DIM
    kmean_ref[...] = jnp.zeros_like(kmean_ref)
    ones_row = lax.broadcasted_iota(jnp.int32, (MOBA_VT_ROWS - hd, blk), 0) == 0
    ones_pad = jnp.where(ones_row, 1.0, 0.0).astype(BF16)
    for j in range(n_blocks):
        kj = k_ref[j * blk:(j + 1) * blk, :].astype(F32)
        kmean_ref[j:j + 1, :] = jnp.mean(kj, axis=0, keepdims=True)
        vt = v_ref[j * blk:(j + 1) * blk, :].astype(F32).T.astype(BF16)
        for h in range(pair):
            vt_ref[j, h, 0:hd, :] = vt[h * hd:(h + 1) * hd, :]
            vt_ref[j, h, hd:, :] = ones_pad

    kpos = lax.broadcasted_iota(jnp.int32, (blk, blk), 0)
    qpos = lax.broadcasted_iota(jnp.int32, (blk, blk), 1)
    causal = kpos <= qpos
    qlane = lax.broadcasted_iota(jnp.int32, (blk, LANES), 1) // hd
    sub = lambda t: t.reshape(blk // SUBLANES, SUBLANES, blk)
    col_max = {}

    def scores(ii):
        q = q_ref[ii * blk:(ii + 1) * blk, :]
        need_gate = ii > MOBA_TOPK
        if need_gate:
            km_hi, km_mid, km_lo = _split3(kmean_ref[...])
            jidx = lax.broadcasted_iota(jnp.int32, (kmean_ref.shape[0], blk), 0)
        for h in range(pair):
            qh = jnp.where(qlane == h, q, jnp.zeros_like(q))
            valid = [None] * ii
            if need_gate:
                gate = _nt(km_hi, qh) + _nt(km_mid, qh) + _nt(km_lo, qh)
                for n in range(ii):
                    gn = gate[n:n + 1, :]
                    beats = ((gate > gn) | ((gate == gn) & (jidx < n))) & (jidx < ii)
                    rank = jnp.sum(jnp.where(beats, 1.0, 0.0), axis=0, keepdims=True)
                    valid[n] = rank < MOBA_TOPK
            m8 = None
            for j in range(ii + 1):
                s = _nt(k_ref[j * blk:(j + 1) * blk, :], qh)
                if j == ii:
                    s = jnp.where(causal, s, NEG)
                elif valid[j] is not None:
                    s = jnp.where(valid[j], s, NEG)
                s_ref[ii % 2, h, j] = s
                t = jnp.max(sub(s), axis=0)
                m8 = t if m8 is None else jnp.maximum(m8, t)
                yield
            col_max[ii, h] = jnp.max(m8, axis=0, keepdims=True)

    def values(ii):
        outs = []
        for h in range(pair):
            acc = None
            for j in range(ii + 1):
                p = jnp.exp2((s_ref[ii % 2, h, j] - col_max[ii, h]).astype(BF16))
                t = _mm(vt_ref[j, h], p)
                acc = t if acc is None else acc + t
                yield
            outs.append(acc[0:hd, :] / acc[hd:hd + 1, :])
        o_ref[ii * blk:(ii + 1) * blk, :] = jnp.concatenate(outs, axis=0).T.astype(BF16)

    for _ in scores(0):
        pass
    for ii in range(n_blocks):
        live = [values(ii)]
        if ii + 1 < n_blocks:
            live.insert(0, scores(ii + 1))
        while live:
            for gen in list(live):
                if next(gen, StopIteration) is StopIteration:
                    live.remove(gen)


def _moba_attn(q, k, v, batch, seq):
    n_blocks = seq // MOBA_BLOCK
    n_pairs = MOBA_WIDTH // LANES
    pair = LANES // MOBA_HEAD_DIM
    spec = pl.BlockSpec((seq, LANES), lambda b, p: (b, p))
    gate_rows = 2 * SUBLANES
    assert n_blocks <= gate_rows
    return pl.pallas_call(
        functools.partial(_moba_attn_kernel, n_blocks=n_blocks),
        grid=(batch, n_pairs),
        in_specs=[spec, spec, spec],
        out_specs=spec,
        out_shape=jax.ShapeDtypeStruct(q.shape, BF16),
        scratch_shapes=[
            pltpu.VMEM((gate_rows, LANES), F32),
            pltpu.VMEM((n_blocks, pair, MOBA_VT_ROWS, MOBA_BLOCK), BF16),
            pltpu.VMEM((2, pair, n_blocks, MOBA_BLOCK, MOBA_BLOCK), F32),
        ],
        compiler_params=_params("parallel", "parallel"),
        name="moba_attn",
    )(q, k, v)


def _combine_kernel(x_ref, g_ref, wga_ref, wgb_ref, wgc_ref, bg_ref, a_ref, b_ref, c_ref, wa_ref, wb_ref,
                    wc_ref, wo_ref, gp_ref, wu32_ref, wd32_ref, o_ref, wu_ref, wd_ref):
    wu_ref[...] = wu32_ref[...].astype(BF16)
    wd_ref[...] = wd32_ref[...].astype(BF16)
    branches = ((wga_ref, a_ref, wa_ref), (wgb_ref, b_ref, wb_ref), (wgc_ref, c_ref, wc_ref))
    group = x_ref.shape[0] // COMBINE_SPLIT
    for r in range(COMBINE_SPLIT):
        rows = slice(r * group, (r + 1) * group)
        x = x_ref[rows, :]
        xn = _rms(x, g_ref[...]).astype(BF16)
        mix = None
        for n, (wg_ref, br_ref, wbr_ref) in enumerate(branches):
            cols = slice(n * D_MODEL, (n + 1) * D_MODEL)
            gate = jax.nn.sigmoid(_mm(xn, wg_ref[...]) + bg_ref[:, cols])
            term = gate * _mm(br_ref[rows, :], wbr_ref[...])
            mix = term if mix is None else mix + term
        y = _mm(mix.astype(BF16), wo_ref[...])
        o_ref[rows, :] = x + _rms(y, gp_ref[...])


def _combine(layer, x, g, wp, bg, a, b, c, wa, wb, wc, wo, gp, wu32, wd32):
    T = x.shape[0]
    slab = D_FF // (T // ROW_TILE)
    assert slab % LANES == 0
    row = lambda n: pl.BlockSpec((ROW_TILE, n), lambda i: (i, 0))
    gate_w = lambda n: _layer_spec(layer, D_MODEL, D_MODEL, PK_GATE // D_MODEL + n)
    return pl.pallas_call(
        _combine_kernel,
        grid=(T // ROW_TILE,),
        in_specs=[row(D_MODEL), _vec_spec(layer, D_MODEL), gate_w(0), gate_w(1), gate_w(2),
                  _vec_spec(layer, N_BRANCH * D_MODEL),
                  row(MOBA_WIDTH), row(GLA_VW), row(CONV_CH),
                  _const_spec(wa.shape), _const_spec(wb.shape), _const_spec(wc.shape), _const_spec(wo.shape),
                  _vec_spec(layer, D_MODEL),
                  pl.BlockSpec((None, D_MODEL, slab), lambda i: (layer, 0, i)),
                  pl.BlockSpec((None, slab, D_MODEL), lambda i: (layer, i, 0))],
        out_specs=[row(D_MODEL), pl.BlockSpec((D_MODEL, slab), lambda i: (0, i)),
                   pl.BlockSpec((slab, D_MODEL), lambda i: (i, 0))],
        out_shape=[jax.ShapeDtypeStruct((T, D_MODEL), F32), jax.ShapeDtypeStruct((D_MODEL, D_FF), BF16),
                   jax.ShapeDtypeStruct((D_FF, D_MODEL), BF16)],
        compiler_params=_params("parallel"),
        name="combine",
    )(x, g, wp, wp, wp, bg, a, b, c, wa, wb, wc, wo, gp, wu32, wd32)


def _mlp_kernel(x_ref, g_ref, wu_ref, wd_ref, gp_ref, o_ref):
    x = x_ref[...]
    xn = _rms(x, g_ref[...]).astype(BF16)
    acc = None
    for c in range(D_FF // D_MODEL):
        cols = slice(c * D_MODEL, (c + 1) * D_MODEL)
        u = jnp.maximum(_mm(xn, wu_ref[:, cols]), 0.0)
        t = _mm((u * u).astype(BF16), wd_ref[cols, :])
        acc = t if acc is None else acc + t
    o_ref[...] = x + _rms(acc, gp_ref[...])


def _mlp(layer, x, g, wu, wd, gp):
    T = x.shape[0]
    row = pl.BlockSpec((ROW_TILE, D_MODEL), lambda i: (i, 0))
    return pl.pallas_call(
        _mlp_kernel,
        grid=(T // ROW_TILE,),
        in_specs=[row, _vec_spec(layer, D_MODEL), _const_spec((D_MODEL, D_FF)), _const_spec((D_FF, D_MODEL)),
                  _vec_spec(layer, D_MODEL)],
        out_specs=row,
        out_shape=jax.ShapeDtypeStruct((T, D_MODEL), F32),
        compiler_params=_params("parallel"),
        name="mlp",
    )(x, g, wu, wd, gp)


def kernel(x, positions, g_mix_pre, w_in, b_gate, w_gla_a2, b_gla_a, g_gla_norm, w_dw, b_dw, g_conv_ln,
           b_conv_ln, w_moba_o, w_gla_o, w_conv_o, w_mix_out, g_mix_post, g_mlp_pre, w_mlp_up, w_mlp_down,
           g_mlp_post):
    batch, seq, _ = x.shape
    depth = w_in.shape[0]
    assert seq % MOBA_BLOCK == 0 and seq % FRONT_TILE == 0 and (batch * seq) % ROW_TILE == 0
    assert FRONT_TILE % GLA_TILE == 0 and GLA_TILE % GLA_CHUNK == 0
    h = x.reshape(batch * seq, D_MODEL)
    cos, sin = _rope_tables(positions)

    o_la = 3 * MOBA_WIDTH + 2 * GLA_KW + GLA_VW
    o_lr = o_la + GLA_RANK
    pad_la = jnp.pad(w_in[:, :, o_la:o_lr], ((0, 0), (0, 0), (0, LANES - GLA_RANK)))
    wp = jnp.concatenate([w_in[:, :, :o_la], w_in[:, :, o_lr:], pad_la], axis=-1).astype(BF16)
    assert wp.shape[-1] == PK_LA + LANES
    wa2 = jnp.pad(w_gla_a2, ((0, 0), (0, LANES - GLA_RANK), (0, 0))).astype(BF16)
    vec = lambda v: v[:, None, :]
    g_pre, g_post, g_mpre, g_mpost = vec(g_mix_pre), vec(g_mix_post), vec(g_mlp_pre), vec(g_mlp_post)
    bg, ba, gn = vec(b_gate), vec(b_gla_a), vec(g_gla_norm)
    bdw, gln, bln = vec(b_dw), vec(g_conv_ln), vec(b_conv_ln)
    out_ws = (w_moba_o, w_gla_o, w_conv_o, w_mix_out)

    for l in range(depth):
        mq, mk, mv, ob, oc, wa, wb, wc, wo = _front(l, h, g_pre, wp, wa2, ba, cos, sin, w_dw, bdw, gln, bln, gn,
                                                    out_ws, batch, seq)
        oa = _moba_attn(mq, mk, mv, batch, seq)
        h, wu, wd = _combine(l, h, g_pre, wp, bg, oa, ob, oc, wa, wb, wc, wo, g_post, w_mlp_up, w_mlp_down)
        h = _mlp(l, h, g_mpre, wu, wd, g_mpost)
    return h.reshape(batch, seq, D_MODEL)
```

```python
import functools

import jax
import jax.numpy as jnp
from jax import lax
from jax.experimental import pallas as pl
from jax.experimental.pallas import tpu as pltpu

D_MODEL = 1024
MOBA_HEADS = 8
MOBA_HEAD_DIM = 64
MOBA_WIDTH = MOBA_HEADS * MOBA_HEAD_DIM
MOBA_BLOCK = 256
MOBA_TOPK = 3
ROPE_THETA = 10000.0
GLA_HEADS = 4
GLA_DK = 64
GLA_DV = 128
GLA_KW = GLA_HEADS * GLA_DK
GLA_VW = GLA_HEADS * GLA_DV
GLA_RANK = 16
GLA_TEMP = 16.0
GLA_CHUNK = 64
CONV_CH = 512
CONV_WIDTH = 31
D_FF = 4 * D_MODEL
N_BRANCH = 3
EPS = 1e-6
NEG = -1e30
LOG2E = 1.4426950408889634
MOBA_Q_SCALE = MOBA_HEAD_DIM ** -0.5 * LOG2E

LANES = 128
SUBLANES = 8
VMEM_LIMIT_BYTES = 56 * 1024 * 1024
MOBA_VT_ROWS = MOBA_HEAD_DIM + 2 * SUBLANES

ROW_TILE = 1024
COMBINE_SPLIT = 4
FRONT_TILE = 1024
GLA_TILE = 256
CONV_HALO = 32
CONV_ROW_CHUNK = 64

PK_MOBA = 0
PK_GLA = 3 * MOBA_WIDTH
PK_CONV = PK_GLA + 2 * GLA_KW + 2 * GLA_VW
PK_GATE = PK_CONV + 2 * CONV_CH
PK_LA = PK_GATE + N_BRANCH * D_MODEL

F32 = jnp.float32
BF16 = jnp.bfloat16


def _nt(a, b):
    return lax.dot_general(a, b, (((1,), (1,)), ((), ())), preferred_element_type=F32)


def _tn(a, b):
    return lax.dot_general(a, b, (((0,), (0,)), ((), ())), preferred_element_type=F32)


def _mm(a, b):
    return jnp.dot(a, b, preferred_element_type=F32)


def _sigmoid(x):
    return 0.5 * jnp.tanh(0.5 * x) + 0.5


def _rms(x, g):
    return x * lax.rsqrt(jnp.mean(x * x, axis=-1, keepdims=True) + EPS) * g


def _split3(x):
    hi = x.astype(BF16)
    r1 = x - hi.astype(F32)
    mid = r1.astype(BF16)
    lo = (r1 - mid.astype(F32)).astype(BF16)
    return hi, mid, lo


def _params(*sem):
    return pltpu.CompilerParams(dimension_semantics=sem, vmem_limit_bytes=VMEM_LIMIT_BYTES)


def _const_spec(shape):
    nd = len(shape)
    return pl.BlockSpec(shape, lambda *_: (0,) * nd)


def _resident_spec(shape):
    nd = len(shape)
    return pl.BlockSpec(shape, lambda *_: (0,) * nd, pipeline_mode=pl.Buffered(1))


def _layer_spec(layer, rows, cols, col_block=0, buffers=None):
    mode = {} if buffers is None else {"pipeline_mode": pl.Buffered(buffers)}
    return pl.BlockSpec((None, rows, cols), lambda *_: (layer, 0, col_block), **mode)


def _vec_spec(layer, n):
    return _layer_spec(layer, 1, n)


def _rope_table_kernel(pos_ref, invf_ref, cos_ref, sin_ref):
    ang = pos_ref[...].astype(F32) * invf_ref[...]
    cos_ref[...] = jnp.cos(ang)
    sin_ref[...] = jnp.sin(ang)


def _rope_tables(positions):
    T = positions.size
    half = MOBA_HEAD_DIM // 2
    per_row = LANES // half
    inv_freq = ROPE_THETA ** (-jnp.arange(half, dtype=F32) / half)
    invf = jnp.tile(inv_freq, per_row)[None, :]
    pos = jnp.repeat(positions.reshape(T // per_row, per_row), half, axis=1)
    rows = T // per_row
    tile = min(ROW_TILE, rows)
    assert rows % tile == 0
    tab = jax.ShapeDtypeStruct((rows, LANES), F32)
    row = pl.BlockSpec((tile, LANES), lambda i: (i, 0))
    cos, sin = pl.pallas_call(
        _rope_table_kernel,
        grid=(rows // tile,),
        in_specs=[row, _const_spec((1, LANES))],
        out_specs=[row, row],
        out_shape=[tab, tab],
        compiler_params=_params("parallel"),
        name="rope_tables",
    )(pos, invf)
    return cos.reshape(T, half), sin.reshape(T, half)


def _rope(s, cos, sa, sb):
    return s * cos + pltpu.roll(s, LANES - 32, 1) * sa + pltpu.roll(s, 32, 1) * sb


def _gla_masks():
    ch, sc = GLA_CHUNK, GLA_TILE
    ri = lax.broadcasted_iota(jnp.int32, (sc, sc), 0)
    ci = lax.broadcasted_iota(jnp.int32, (sc, sc), 1)
    lower = (ri >= ci) & (ri // ch == ci // ch)
    tril = jnp.where(lower, 1.0, 0.0).astype(BF16)
    lane_head = lax.broadcasted_iota(jnp.int32, (sc, LANES), 1) // GLA_DK
    st_lane_head = lax.broadcasted_iota(jnp.int32, (GLA_DV, LANES), 1) // GLA_DK
    return lower, tril, lane_head, st_lane_head


def _gla_tile(q, k, v, r, d, gn, st_box, masks):
    ch, sc = GLA_CHUNK, GLA_TILE
    nc = sc // ch
    pair = LANES // GLA_DK
    lower, tril, lane_head, st_lane_head = masks
    d_hi, d_mid, d_lo = _split3(d)
    g3 = _mm(tril, jnp.concatenate([d_hi, d_mid, d_lo], axis=1))
    G = g3[:, :LANES] + g3[:, LANES:2 * LANES] + g3[:, 2 * LANES:]
    yield
    per_chunk = lambda row: jnp.concatenate(
        [jnp.broadcast_to(G[c * ch + row:c * ch + row + 1, :], (ch, LANES)) for c in range(nc)], axis=0)
    g_last = [G[(c + 1) * ch - 1:(c + 1) * ch, :] for c in range(nc)]
    g_last_rows = per_chunk(ch - 1)
    g_mid_rows = per_chunk(ch // 2 - 1)
    scale = GLA_DK ** -0.5
    qa = (q * jnp.exp(G - g_mid_rows) * scale).astype(BF16)
    kt = (k * jnp.exp(g_mid_rows - G)).astype(BF16)
    qt = (q * jnp.exp(G) * scale).astype(BF16)
    kd = (k * jnp.exp(g_last_rows - G)).astype(BF16)
    yield
    qh, vh, o_intra, upd = [], [], [], []
    for h in range(pair):
        qh.append(jnp.where(lane_head == h, qt, jnp.zeros_like(qt)))
        qah = jnp.where(lane_head == h, qa, jnp.zeros_like(qa))
        a = jnp.where(lower, _nt(qah, kt), 0.0).astype(BF16)
        vh.append(v[:, h * GLA_DV:(h + 1) * GLA_DV])
        o_intra.append(_mm(a, vh[h]))
        upd.append([_tn(vh[h][c * ch:(c + 1) * ch], kd[c * ch:(c + 1) * ch]) for c in range(nc)])
        yield
    o_inter = [[] for _ in range(pair)]
    st = st_box[0]
    for c in range(nc):
        st_b = st.astype(BF16)
        u = upd[0][c]
        for h in range(pair):
            o_inter[h].append(_nt(qh[h][c * ch:(c + 1) * ch], st_b))
            if h:
                u = jnp.where(st_lane_head == h, upd[h][c], u)
        st = st * jnp.exp(g_last[c]) + u
    st_box[0] = st
    yield
    outs = []
    for h in range(pair):
        o = _rms(o_intra[h] + jnp.concatenate(o_inter[h], axis=0), gn)
        rh = r[:, h * GLA_DV:(h + 1) * GLA_DV]
        outs.append(o * (rh * _sigmoid(rh)))
        yield
    return outs


def _drive(*gens):
    results = [None] * len(gens)
    live = list(enumerate(gens))
    while live:
        for item in list(live):
            n, gen = item
            try:
                next(gen)
            except StopIteration as stop:
                results[n] = stop.value
                live.remove(item)
    return results


def _front_kernel(x_ref, g_ref, w_ref, wla_ref, wa2_ref, ba_ref, cos_ref, sin_ref,
                  wdw_ref, bdw_ref, gln_ref, bln_ref, gn_ref, wa32_ref, wb32_ref, wc32_ref, wo32_ref,
                  mq_ref, mk_ref, mv_ref, ob_ref, oc_ref, wa_ref, wb_ref, wc_ref, wo_ref,
                  u_ref, us_ref, y_ref, st_ref):
    ts, halo = FRONT_TILE, CONV_HALO
    i = pl.program_id(1)
    for src, dst in ((wa32_ref, wa_ref), (wb32_ref, wb_ref), (wc32_ref, wc_ref), (wo32_ref, wo_ref)):
        dst[...] = src[...].astype(BF16)

    @pl.when(i == 0)
    def _():
        u_ref[0:halo, :] = jnp.zeros((halo, CONV_CH), F32)
        st_ref[...] = jnp.zeros_like(st_ref)

    @pl.when(i > 0)
    def _():
        u_ref[0:halo, :] = u_ref[ts:ts + halo, :]

    xn = _rms(x_ref[...], g_ref[...]).astype(BF16)

    zc = _mm(xn, w_ref[:, PK_CONV:PK_CONV + 2 * CONV_CH])
    u_ref[halo:halo + ts, :] = zc[:, :CONV_CH] * _sigmoid(zc[:, CONV_CH:])
    span = us_ref.shape[1]
    for b in range(1, SUBLANES):
        us_ref[b - 1] = u_ref[b:b + span, :]

    zm = _mm(xn, w_ref[:, PK_MOBA:PK_MOBA + 3 * MOBA_WIDTH])
    reps = LANES // cos_ref.shape[1]
    cos = jnp.tile(cos_ref[...], (1, reps))
    sin = jnp.tile(sin_ref[...], (1, reps))
    first_half = (lax.broadcasted_iota(jnp.int32, sin.shape, 1) % MOBA_HEAD_DIM) < (MOBA_HEAD_DIM // 2)
    sa = jnp.where(first_half, -sin, 0.0)
    sb = jnp.where(first_half, 0.0, sin)
    for c in range(MOBA_WIDTH // LANES):
        cols = slice(c * LANES, (c + 1) * LANES)
        mq_ref[:, cols] = (_rope(zm[:, cols], cos, sa, sb) * MOBA_Q_SCALE).astype(BF16)
        mk_ref[:, cols] = _rope(zm[:, MOBA_WIDTH + c * LANES:MOBA_WIDTH + (c + 1) * LANES], cos, sa, sb).astype(BF16)
    mv_ref[...] = zm[:, 2 * MOBA_WIDTH:].astype(BF16)

    zg = _mm(xn, w_ref[:, PK_GLA:PK_GLA + 2 * GLA_KW + 2 * GLA_VW])
    la = _mm(xn, wla_ref[...])
    pre = _mm(la.astype(BF16), wa2_ref[...]) + ba_ref[...]
    dec = (jnp.minimum(pre, 0.0) - jnp.log1p(jnp.exp(-jnp.abs(pre)))) / GLA_TEMP

    def conv():
        first = halo - (CONV_WIDTH - 1)
        for c in range(CONV_CH // LANES):
            cols = slice(c * LANES, (c + 1) * LANES)
            for r in range(0, ts, CONV_ROW_CHUNK):
                acc = jnp.broadcast_to(bdw_ref[:, cols], (CONV_ROW_CHUNK, LANES))
                for w in range(CONV_WIDTH):
                    a, b = divmod(first + w, SUBLANES)
                    lo = r + a * SUBLANES
                    tap = (u_ref[lo:lo + CONV_ROW_CHUNK, cols] if b == 0
                           else us_ref[b - 1, lo:lo + CONV_ROW_CHUNK, cols])
                    acc = acc + wdw_ref[w:w + 1, cols] * tap
                y_ref[r:r + CONV_ROW_CHUNK, cols] = acc
                yield

    masks = _gla_masks()
    gn = gn_ref[...]
    pair_dv = (LANES // GLA_DK) * GLA_DV
    n_pairs, n_sub = GLA_KW // LANES, ts // GLA_TILE
    st_boxes = [[st_ref[p]] for p in range(n_pairs)]
    glas = []
    for t in range(n_sub):
        rows = slice(t * GLA_TILE, (t + 1) * GLA_TILE)
        for p in range(n_pairs):
            kl = slice(p * LANES, (p + 1) * LANES)
            vl = slice(2 * GLA_KW + p * pair_dv, 2 * GLA_KW + (p + 1) * pair_dv)
            rl = slice(2 * GLA_KW + GLA_VW + p * pair_dv, 2 * GLA_KW + GLA_VW + (p + 1) * pair_dv)
            glas.append(_gla_tile(zg[rows, kl], zg[rows, GLA_KW + p * LANES:GLA_KW + (p + 1) * LANES],
                                  zg[rows, vl].astype(BF16), zg[rows, rl], dec[rows, kl], gn, st_boxes[p], masks))
    res = _drive(*glas, conv())
    for t in range(n_sub):
        for p in range(n_pairs):
            for h, o in enumerate(res[t * n_pairs + p]):
                ob_ref[t * GLA_TILE:(t + 1) * GLA_TILE, p * pair_dv + h * GLA_DV:p * pair_dv + (h + 1) * GLA_DV] = o.astype(BF16)
    for p in range(n_pairs):
        st_ref[p] = st_boxes[p][0]

    y = y_ref[...]
    mu = jnp.mean(y, axis=-1, keepdims=True)
    yc = y - mu
    var = jnp.mean(yc * yc, axis=-1, keepdims=True)
    yn = yc * lax.rsqrt(var + EPS) * gln_ref[...] + bln_ref[...]
    oc_ref[...] = (yn * _sigmoid(yn)).astype(BF16)


def _front(layer, x, g, wp, wa2, ba, cos, sin, wdw, bdw, gln, bln, gn, out_ws, batch, seq):
    T = x.shape[0]
    nt = seq // FRONT_TILE
    row = lambda n: pl.BlockSpec((FRONT_TILE, n), lambda b, i: (b * nt + i, 0))
    sds = lambda n: jax.ShapeDtypeStruct((T, n), BF16)
    assert PK_MOBA == 0 and PK_GATE % LANES == 0
    steps = batch * nt
    slab_rows = [w.shape[1] // steps for w in out_ws]
    assert all(r % (2 * SUBLANES) == 0 for r in slab_rows)
    slab_in = [pl.BlockSpec((None, r, D_MODEL), lambda b, i: (layer, b * nt + i, 0)) for r in slab_rows]
    slab_out = [pl.BlockSpec((r, D_MODEL), lambda b, i: (b * nt + i, 0)) for r in slab_rows]
    return pl.pallas_call(
        _front_kernel,
        grid=(batch, nt),
        in_specs=[row(D_MODEL), _vec_spec(layer, D_MODEL), _layer_spec(layer, D_MODEL, PK_GATE, 0),
                  _layer_spec(layer, D_MODEL, LANES, PK_LA // LANES), _layer_spec(layer, LANES, GLA_KW),
                  _vec_spec(layer, GLA_KW), row(MOBA_HEAD_DIM // 2), row(MOBA_HEAD_DIM // 2),
                  _layer_spec(layer, CONV_WIDTH, CONV_CH),
                  _vec_spec(layer, CONV_CH), _vec_spec(layer, CONV_CH), _vec_spec(layer, CONV_CH),
                  _vec_spec(layer, GLA_DV)] + slab_in,
        out_specs=[row(MOBA_WIDTH), row(MOBA_WIDTH), row(MOBA_WIDTH), row(GLA_VW), row(CONV_CH)] + slab_out,
        out_shape=[sds(MOBA_WIDTH), sds(MOBA_WIDTH), sds(MOBA_WIDTH), sds(GLA_VW), sds(CONV_CH)]
        + [jax.ShapeDtypeStruct(w.shape[1:], BF16) for w in out_ws],
        scratch_shapes=[pltpu.VMEM((FRONT_TILE + CONV_HALO, CONV_CH), F32),
                        pltpu.VMEM((SUBLANES - 1, FRONT_TILE + CONV_HALO - SUBLANES, CONV_CH), F32),
                        pltpu.VMEM((FRONT_TILE, CONV_CH), F32),
                        pltpu.VMEM((GLA_KW // LANES, GLA_DV, LANES), F32)],
        compiler_params=_params("parallel", "arbitrary"),
        name="front",
    )(x, g, wp, wp, wa2, ba, cos, sin, wdw, bdw, gln, bln, gn, *out_ws)


def _moba_attn_kernel(q_ref, k_ref, v_ref, o_ref, kmean_ref, vt_ref, s_ref, *, n_blocks):
    blk = MOBA_BLOCK
    pair = LANES // MOBA_HEAD_DIM
    hd = MOBA_HEAD_DIM
    kmean_ref[...] = jnp.zeros_like(kmean_ref)
    ones_row = lax.broadcasted_iota(jnp.int32, (MOBA_VT_ROWS - hd, blk), 0) == 0
    ones_pad = jnp.where(ones_row, 1.0, 0.0).astype(BF16)
    for j in range(n_blocks):
        kj = k_ref[j * blk:(j + 1) * blk, :].astype(F32)
        kmean_ref[j:j + 1, :] = jnp.mean(kj, axis=0, keepdims=True)
        vt = v_ref[j * blk:(j + 1) * blk, :].astype(F32).T.astype(BF16)
        for h in range(pair):
            vt_ref[j, h, 0:hd, :] = vt[h * hd:(h + 1) * hd, :]
            vt_ref[j, h, hd:, :] = ones_pad

    kpos = lax.broadcasted_iota(jnp.int32, (blk, blk), 0)
    qpos = lax.broadcasted_iota(jnp.int32, (blk, blk), 1)
    causal = kpos <= qpos
    qlane = lax.broadcasted_iota(jnp.int32, (blk, LANES), 1) // hd
    sub = lambda t: t.reshape(blk // SUBLANES, SUBLANES, blk)
    col_max = {}

    def scores(ii):
        q = q_ref[ii * blk:(ii + 1) * blk, :]
        need_gate = ii > MOBA_TOPK
        if need_gate:
            km_hi, km_mid, km_lo = _split3(kmean_ref[...])
            jidx = lax.broadcasted_iota(jnp.int32, (kmean_ref.shape[0], blk), 0)
        for h in range(pair):
            qh = jnp.where(qlane == h, q, jnp.zeros_like(q))
            valid = [None] * ii
            if need_gate:
                gate = _nt(km_hi, qh) + _nt(km_mid, qh) + _nt(km_lo, qh)
                for n in range(ii):
                    gn = gate[n:n + 1, :]
                    beats = ((gate > gn) | ((gate == gn) & (jidx < n))) & (jidx < ii)
                    rank = jnp.sum(jnp.where(beats, 1.0, 0.0), axis=0, keepdims=True)
                    valid[n] = rank < MOBA_TOPK
            m8 = None
            for j in range(ii + 1):
                s = _nt(k_ref[j * blk:(j + 1) * blk, :], qh)
                if j == ii:
                    s = jnp.where(causal, s, NEG)
                elif valid[j] is not None:
                    s = jnp.where(valid[j], s, NEG)
                s_ref[ii % 2, h, j] = s
                t = jnp.max(sub(s), axis=0)
                m8 = t if m8 is None else jnp.maximum(m8, t)
                yield
            col_max[ii, h] = jnp.max(m8, axis=0, keepdims=True)

    def values(ii):
        outs = []
        for h in range(pair):
            acc = None
            for j in range(ii + 1):
                p = jnp.exp2((s_ref[ii % 2, h, j] - col_max[ii, h]).astype(BF16))
                t = _mm(vt_ref[j, h], p)
                acc = t if acc is None else acc + t
                yield
            outs.append(acc[0:hd, :] / acc[hd:hd + 1, :])
        o_ref[ii * blk:(ii + 1) * blk, :] = jnp.concatenate(outs, axis=0).T.astype(BF16)

    for _ in scores(0):
        pass
    for ii in range(n_blocks):
        live = [values(ii)]
        if ii + 1 < n_blocks:
            live.insert(0, scores(ii + 1))
        while live:
            for gen in list(live):
                if next(gen, StopIteration) is StopIteration:
                    live.remove(gen)


def _moba_attn(q, k, v, batch, seq):
    n_blocks = seq // MOBA_BLOCK
    n_pairs = MOBA_WIDTH // LANES
    pair = LANES // MOBA_HEAD_DIM
    spec = pl.BlockSpec((seq, LANES), lambda b, p: (b, p))
    gate_rows = 2 * SUBLANES
    assert n_blocks <= gate_rows
    return pl.pallas_call(
        functools.partial(_moba_attn_kernel, n_blocks=n_blocks),
        grid=(batch, n_pairs),
        in_specs=[spec, spec, spec],
        out_specs=spec,
        out_shape=jax.ShapeDtypeStruct(q.shape, BF16),
        scratch_shapes=[
            pltpu.VMEM((gate_rows, LANES), F32),
            pltpu.VMEM((n_blocks, pair, MOBA_VT_ROWS, MOBA_BLOCK), BF16),
            pltpu.VMEM((2, pair, n_blocks, MOBA_BLOCK, MOBA_BLOCK), F32),
        ],
        compiler_params=_params("parallel", "parallel"),
        name="moba_attn",
    )(q, k, v)


def _combine_kernel(x_ref, g_ref, wga_ref, wgb_ref, wgc_ref, bg_ref, a_ref, b_ref, c_ref, wa_ref, wb_ref,
                    wc_ref, wo_ref, gp_ref, wu32_ref, wd32_ref, o_ref, wu_ref, wd_ref):
    wu_ref[...] = wu32_ref[...].astype(BF16)
    wd_ref[...] = wd32_ref[...].astype(BF16)
    branches = ((wga_ref, a_ref, wa_ref), (wgb_ref, b_ref, wb_ref), (wgc_ref, c_ref, wc_ref))
    group = x_ref.shape[0] // COMBINE_SPLIT
    for r in range(COMBINE_SPLIT):
        rows = slice(r * group, (r + 1) * group)
        x = x_ref[rows, :]
        xn = _rms(x, g_ref[...]).astype(BF16)
        mix = None
        for n, (wg_ref, br_ref, wbr_ref) in enumerate(branches):
            cols = slice(n * D_MODEL, (n + 1) * D_MODEL)
            gate = jax.nn.sigmoid(_mm(xn, wg_ref[...]) + bg_ref[:, cols])
            term = gate * _mm(br_ref[rows, :], wbr_ref[...])
            mix = term if mix is None else mix + term
        y = _mm(mix.astype(BF16), wo_ref[...])
        o_ref[rows, :] = x + _rms(y, gp_ref[...])


def _combine(layer, x, g, wp, bg, a, b, c, wa, wb, wc, wo, gp, wu32, wd32):
    T = x.shape[0]
    slab = D_FF // (T // ROW_TILE)
    assert slab % LANES == 0
    row = lambda n: pl.BlockSpec((ROW_TILE, n), lambda i: (i, 0))
    gate_w = lambda n: _layer_spec(layer, D_MODEL, D_MODEL, PK_GATE // D_MODEL + n, buffers=1)
    return pl.pallas_call(
        _combine_kernel,
        grid=(T // ROW_TILE,),
        in_specs=[row(D_MODEL), _vec_spec(layer, D_MODEL), gate_w(0), gate_w(1), gate_w(2),
                  _vec_spec(layer, N_BRANCH * D_MODEL),
                  row(MOBA_WIDTH), row(GLA_VW), row(CONV_CH),
                  _resident_spec(wa.shape), _resident_spec(wb.shape), _resident_spec(wc.shape),
                  _resident_spec(wo.shape),
                  _vec_spec(layer, D_MODEL),
                  pl.BlockSpec((None, D_MODEL, slab), lambda i: (layer, 0, i)),
                  pl.BlockSpec((None, slab, D_MODEL), lambda i: (layer, i, 0))],
        out_specs=[row(D_MODEL), pl.BlockSpec((D_MODEL, slab), lambda i: (0, i)),
                   pl.BlockSpec((slab, D_MODEL), lambda i: (i, 0))],
        out_shape=[jax.ShapeDtypeStruct((T, D_MODEL), F32), jax.ShapeDtypeStruct((D_MODEL, D_FF), BF16),
                   jax.ShapeDtypeStruct((D_FF, D_MODEL), BF16)],
        compiler_params=_params("parallel"),
        name="combine",
    )(x, g, wp, wp, wp, bg, a, b, c, wa, wb, wc, wo, gp, wu32, wd32)


def _mlp_kernel(x_ref, g_ref, wu_ref, wd_ref, gp_ref, o_ref):
    x = x_ref[...]
    xn = _rms(x, g_ref[...]).astype(BF16)
    acc = None
    for c in range(D_FF // D_MODEL):
        cols = slice(c * D_MODEL, (c + 1) * D_MODEL)
        u = jnp.maximum(_mm(xn, wu_ref[:, cols]), 0.0)
        t = _mm((u * u).astype(BF16), wd_ref[cols, :])
        acc = t if acc is None else acc + t
    o_ref[...] = x + _rms(acc, gp_ref[...])


def _mlp(layer, x, g, wu, wd, gp):
    T = x.shape[0]
    row = pl.BlockSpec((ROW_TILE, D_MODEL), lambda i: (i, 0))
    return pl.pallas_call(
        _mlp_kernel,
        grid=(T // ROW_TILE,),
        in_specs=[row, _vec_spec(layer, D_MODEL), _resident_spec((D_MODEL, D_FF)), _resident_spec((D_FF, D_MODEL)),
                  _vec_spec(layer, D_MODEL)],
        out_specs=row,
        out_shape=jax.ShapeDtypeStruct((T, D_MODEL), F32),
        compiler_params=_params("parallel"),
        name="mlp",
    )(x, g, wu, wd, gp)


def kernel(x, positions, g_mix_pre, w_in, b_gate, w_gla_a2, b_gla_a, g_gla_norm, w_dw, b_dw, g_conv_ln,
           b_conv_ln, w_moba_o, w_gla_o, w_conv_o, w_mix_out, g_mix_post, g_mlp_pre, w_mlp_up, w_mlp_down,
           g_mlp_post):
    batch, seq, _ = x.shape
    depth = w_in.shape[0]
    assert seq % MOBA_BLOCK == 0 and seq % FRONT_TILE == 0 and (batch * seq) % ROW_TILE == 0
    assert FRONT_TILE % GLA_TILE == 0 and GLA_TILE % GLA_CHUNK == 0
    h = x.reshape(batch * seq, D_MODEL)
    cos, sin = _rope_tables(positions)

    o_la = 3 * MOBA_WIDTH + 2 * GLA_KW + GLA_VW
    o_lr = o_la + GLA_RANK
    pad_la = jnp.pad(w_in[:, :, o_la:o_lr], ((0, 0), (0, 0), (0, LANES - GLA_RANK)))
    wp = jnp.concatenate([w_in[:, :, :o_la], w_in[:, :, o_lr:], pad_la], axis=-1).astype(BF16)
    assert wp.shape[-1] == PK_LA + LANES
    wa2 = jnp.pad(w_gla_a2, ((0, 0), (0, LANES - GLA_RANK), (0, 0))).astype(BF16)
    vec = lambda v: v[:, None, :]
    g_pre, g_post, g_mpre, g_mpost = vec(g_mix_pre), vec(g_mix_post), vec(g_mlp_pre), vec(g_mlp_post)
    bg, ba, gn = vec(b_gate), vec(b_gla_a), vec(g_gla_norm)
    bdw, gln, bln = vec(b_dw), vec(g_conv_ln), vec(b_conv_ln)
    out_ws = (w_moba_o, w_gla_o, w_conv_o, w_mix_out)

    for l in range(depth):
        mq, mk, mv, ob, oc, wa, wb, wc, wo = _front(l, h, g_pre, wp, wa2, ba, cos, sin, w_dw, bdw, gln, bln, gn,
                                                    out_ws, batch, seq)
        oa = _moba_attn(mq, mk, mv, batch, seq)
        h, wu, wd = _combine(l, h, g_pre, wp, bg, oa, ob, oc, wa, wb, wc, wo, g_post, w_mlp_up, w_mlp_down)
        h = _mlp(l, h, g_mpre, wu, wd, g_mpost)
    return h.reshape(batch, seq, D_MODEL)
```
